```python
import jax, jax.numpy as jnp
from jax import lax
import numpy as np

D_MODEL = 1024
BATCH = 16
SEQ = 256
DEPTH = 4
DEC_BATCH = 4
DEC_SEQ = 2048
PAST_LEN = 512

GRID_W = 64
N_BRANCH = 4
BRANCH_W = D_MODEL // 2
N_MOD = 9
D_FF = ((8 * D_MODEL // 3 + 127) // 128) * 128
EPS = 1e-6
CONV_W = 4
CONV_PAD_LEFT = CONV_W // 2
LRU_W = BRANCH_W
LRU_BLOCKS = 8
LRU_BW = LRU_W // LRU_BLOCKS
LRU_C = 8.0
HEAD_DIM = 64
GQA_HEADS = BRANCH_W // HEAD_DIM
GQA_KV = GQA_HEADS // 4
ROPE_BASE = 10000.0
Q_BLOCK = 128
NA_HEADS = BRANCH_W // HEAD_DIM
NA_WIN_R = 8
NA_WIN_C = 16
DN_DK = 128
DN_DV = 128
DN_HEADS = BRANCH_W // DN_DV
DN_CHUNK = 64
IN_WIDTHS = (
    LRU_W, LRU_W,
    GQA_HEADS * HEAD_DIM, GQA_KV * HEAD_DIM, GQA_KV * HEAD_DIM,
    NA_HEADS * HEAD_DIM, NA_HEADS * HEAD_DIM, NA_HEADS * HEAD_DIM,
    DN_HEADS * DN_DK, DN_HEADS * DN_DK, DN_HEADS * DN_DV, DN_HEADS * DN_DV,
    2 * DN_HEADS, 2 * DN_HEADS,
    N_BRANCH * D_MODEL,
)
N_IN = sum(IN_WIDTHS)
DN_CONV_CH = 2 * DN_HEADS * DN_DK + DN_HEADS * DN_DV

kernel_name = 'hybrid_prefix_diffusion_trunk_step'


def rmsnorm(x, g):
    xf = x.astype(jnp.float32)
    y = xf * lax.rsqrt(jnp.mean(xf * xf, axis=-1, keepdims=True) + EPS)
    return (y * g.astype(jnp.float32)).astype(x.dtype)


def l2norm(x):
    return x * lax.rsqrt(jnp.sum(x * x, axis=-1, keepdims=True) + EPS)


def swiglu(h, w_gate, w_up, w_down):
    return (jax.nn.silu(h @ w_gate) * (h @ w_up)) @ w_down


def centred_dwconv(x, w):
    ch = x.shape[-1]
    return lax.conv_general_dilated(
        x, w[:, None, :].astype(x.dtype), window_strides=(1,),
        padding=[(CONV_PAD_LEFT, CONV_W - 1 - CONV_PAD_LEFT)],
        dimension_numbers=('NWC', 'WIO', 'NWC'), feature_group_count=ch)


def _lin_combine(left, right):
    return (left[0] * right[0], right[0] * left[1] + right[1])


def rglru_scan(x, w_r, b_r, w_i, b_i, lam, h0):
    f32 = jnp.float32
    b, t, w = x.shape
    xb = x.reshape(b, t, LRU_BLOCKS, LRU_BW)
    r = jax.nn.sigmoid(jnp.einsum('btnk,nkj->btnj', xb, w_r.astype(f32)).reshape(b, t, w) + b_r.astype(f32))
    i = jax.nn.sigmoid(jnp.einsum('btnk,nkj->btnj', xb, w_i.astype(f32)).reshape(b, t, w) + b_i.astype(f32))
    log_a = LRU_C * r * jax.nn.log_sigmoid(lam.astype(f32))
    a = jnp.exp(log_a)
    u = jnp.sqrt(-jnp.expm1(2.0 * log_a)) * (i * x)
    a_cum, u_cum = lax.associative_scan(_lin_combine, (a, u), axis=1)
    return a_cum * h0[:, None, :] + u_cum


def axial_angles(t):
    pos = jnp.arange(t)
    half = HEAD_DIM // 2
    inv = jnp.power(ROPE_BASE, -jnp.arange(0, half, 2, dtype=jnp.float32) / half)
    ang_r = (pos // GRID_W).astype(jnp.float32)[:, None] * inv[None, :]
    ang_c = (pos % GRID_W).astype(jnp.float32)[:, None] * inv[None, :]
    return ang_r, ang_c


def _rotate(x, ang):
    x1, x2 = jnp.split(x, 2, axis=-1)
    cos = jnp.cos(ang)[None, :, None, :]
    sin = jnp.sin(ang)[None, :, None, :]
    return jnp.concatenate([x1 * cos - x2 * sin, x2 * cos + x1 * sin], axis=-1)


def axial_rope(x, ang_r, ang_c):
    xf = x.astype(jnp.float32)
    half = x.shape[-1] // 2
    return jnp.concatenate([_rotate(xf[..., :half], ang_r), _rotate(xf[..., half:], ang_c)], axis=-1).astype(x.dtype)


def blocked_attention(q, k, v):
    b, t, h, hd = q.shape
    g = k.shape[2]
    rep = h // g
    nb = t // Q_BLOCK
    scale = hd ** -0.5
    qb = jnp.moveaxis(q.reshape(b, nb, Q_BLOCK, g, rep, hd), 1, 0)

    def one_block(qi):
        s = jnp.einsum('bqgrd,bkgd->bgrqk', qi, k, preferred_element_type=jnp.float32) * scale
        pr = jax.nn.softmax(s, axis=-1).astype(v.dtype)
        return jnp.einsum('bgrqk,bkgd->bqgrd', pr, v)

    o = lax.map(one_block, qb)
    return jnp.moveaxis(o, 0, 1).reshape(b, t, h * hd)


def neighbourhood_attention(q, k, v, k_ctx, v_ctx, rpb):
    b, t, h, hd = q.shape
    rows = t // GRID_W
    kr = min(NA_WIN_R, rows)
    kc = NA_WIN_C
    scale = hd ** -0.5
    qg = q.reshape(b, rows, GRID_W, h, hd)
    kg = k.reshape(b, rows, GRID_W, h, hd)
    vg = v.reshape(b, rows, GRID_W, h, hd)
    k_ctx = k_ctx.astype(q.dtype)
    v_ctx = v_ctx.astype(v.dtype)
    col = jnp.arange(GRID_W)
    col_idx = jnp.clip(col - kc // 2, 0, GRID_W - kc)[:, None] + jnp.arange(kc)[None, :]
    col_off = col_idx - col[:, None] + (NA_WIN_C - 1)
    rpb = rpb.astype(jnp.float32)

    def one_row(r):
        rs = jnp.clip(r - kr // 2, 0, rows - kr)
        k_rows = lax.dynamic_slice_in_dim(kg, rs, kr, axis=1)
        v_rows = lax.dynamic_slice_in_dim(vg, rs, kr, axis=1)
        k_nb = k_rows[:, :, col_idx]
        v_nb = v_rows[:, :, col_idx]
        q_r = lax.dynamic_index_in_dim(qg, r, axis=1, keepdims=False)
        row_off = rs + jnp.arange(kr) - r + (NA_WIN_R - 1)
        bias = rpb[:, row_off[None, :, None], col_off[:, None, :]]
        s_loc = jnp.einsum('bqhd,bjqchd->bhqjc', q_r, k_nb, preferred_element_type=jnp.float32) * scale + bias[None]
        s_ctx = jnp.einsum('bqhd,bkhd->bhqk', q_r, k_ctx, preferred_element_type=jnp.float32) * scale
        s = jnp.concatenate([s_loc.reshape(b, h, GRID_W, kr * kc), s_ctx], axis=-1)
        pr = jax.nn.softmax(s, axis=-1).astype(v.dtype)
        p_loc = pr[..., :kr * kc].reshape(b, h, GRID_W, kr, kc)
        p_ctx = pr[..., kr * kc:]
        return (jnp.einsum('bhqjc,bjqchd->bqhd', p_loc, v_nb)
                + jnp.einsum('bhqk,bkhd->bqhd', p_ctx, v_ctx))

    o = lax.map(one_row, jnp.arange(rows))
    return jnp.moveaxis(o, 0, 1).reshape(b, t, h * hd)


def gated_delta_chunked(q, k, v, beta, g, s0):
    b, t, h, dk = q.shape
    dv = v.shape[-1]
    n = t // DN_CHUNK

    def to_chunks(x):
        x = x.reshape((b, n, DN_CHUNK, h) + x.shape[3:])
        return jnp.moveaxis(x, (1, 3), (0, 2))

    qc, kc, vc = to_chunks(q), to_chunks(k), to_chunks(v)
    bc = to_chunks(beta)
    gc = jnp.cumsum(to_chunks(g), axis=-1)
    causal = jnp.tril(jnp.ones((DN_CHUNK, DN_CHUNK), dtype=bool))
    strict = jnp.tril(jnp.ones((DN_CHUNK, DN_CHUNK), dtype=bool), -1)
    decay = jnp.exp(jnp.where(causal, gc[..., :, None] - gc[..., None, :], -jnp.inf))
    kb = kc * bc[..., None]
    lower = jnp.where(strict, jnp.einsum('...id,...jd->...ij', kb, kc) * decay, 0.0)
    eye = jnp.eye(DN_CHUNK, dtype=jnp.float32)
    tmat = lax.linalg.triangular_solve(eye + lower, jnp.broadcast_to(eye, lower.shape),
                                       left_side=True, lower=True, unit_diagonal=True)
    u = tmat @ (vc * bc[..., None])
    w = tmat @ (kb * jnp.exp(gc)[..., None])
    intra = jnp.einsum('...id,...jd->...ij', qc, kc) * decay

    def step(s, xs):
        q_i, k_i, u_i, w_i, g_i, a_i = xs
        v_new = u_i - w_i @ s
        o_i = (q_i * jnp.exp(g_i)[..., None]) @ s + a_i @ v_new
        g_last = g_i[..., -1:]
        s = s * jnp.exp(g_last)[..., None] + jnp.einsum(
            'bhcd,bhce->bhde', k_i * jnp.exp(g_last - g_i)[..., None], v_new)
        return s, o_i

    s_fin, o = lax.scan(step, s0, (qc, kc, u, w, gc, intra))
    o = jnp.moveaxis(o, (0, 2), (1, 3)).reshape(b, t, h, dv)
    return o, s_fin


def mixer(hn, p, ctx):
    b, t, _ = hn.shape
    f32 = jnp.float32
    is_ctx = ctx is None
    split_at = np.cumsum(IN_WIDTHS)[:-1].tolist()
    (a_x, a_y, b_q, b_k, b_v, c_q, c_k, c_v,
     d_q, d_k, d_v, d_z, d_b, d_a, g_lin) = jnp.split(hn @ p['w_in'], split_at, axis=-1)

    xa = (centred_dwconv(a_x, p['lru_conv_w']) + p['lru_conv_b']).astype(f32)
    h0 = jnp.zeros((b, 2, LRU_W), f32) if is_ctx else ctx[4].astype(f32)
    lru_out, lru_fin = [], []
    for dr in range(2):
        xs = xa if dr == 0 else xa[:, ::-1]
        hs = rglru_scan(xs, p['lru_w_r'][dr], p['lru_b_r'][dr], p['lru_w_i'][dr], p['lru_b_i'][dr],
                        p['lru_lambda'][dr], h0[:, dr])
        lru_fin.append(hs[:, -1])
        lru_out.append(hs if dr == 0 else hs[:, ::-1])
    o_a = (lru_out[0] + lru_out[1]).astype(hn.dtype) * jax.nn.gelu(a_y)

    q_b = rmsnorm(b_q.reshape(b, t, GQA_HEADS, HEAD_DIM), p['gqa_q_norm'])
    k_b = rmsnorm(b_k.reshape(b, t, GQA_KV, HEAD_DIM), p['gqa_k_norm'])
    v_b = b_v.reshape(b, t, GQA_KV, HEAD_DIM)
    if is_ctx:
        o_b = blocked_attention(q_b, k_b, v_b)
    else:
        ang_r, ang_c = axial_angles(t)
        o_b = blocked_attention(axial_rope(q_b, ang_r, ang_c),
                                jnp.concatenate([axial_rope(k_b, ang_r, ang_c), ctx[0].astype(k_b.dtype)], axis=1),
                                jnp.concatenate([v_b, ctx[1].astype(v_b.dtype)], axis=1))

    q_c = c_q.reshape(b, t, NA_HEADS, HEAD_DIM)
    k_c = c_k.reshape(b, t, NA_HEADS, HEAD_DIM)
    v_c = c_v.reshape(b, t, NA_HEADS, HEAD_DIM)
    if is_ctx:
        o_c = blocked_attention(q_c, k_c, v_c)
    else:
        o_c = neighbourhood_attention(q_c, k_c, v_c, ctx[2], ctx[3], p['na_rpb'])

    qkv = jax.nn.silu(centred_dwconv(jnp.concatenate([d_q, d_k, d_v], axis=-1), p['dn_conv_w'])).astype(f32)
    q_d, k_d, v_d = jnp.split(qkv, [DN_HEADS * DN_DK, 2 * DN_HEADS * DN_DK], axis=-1)
    q_d = l2norm(q_d.reshape(b, t, DN_HEADS, DN_DK)) * (DN_DK ** -0.5)
    k_d = l2norm(k_d.reshape(b, t, DN_HEADS, DN_DK))
    v_d = v_d.reshape(b, t, DN_HEADS, DN_DV)
    beta_in = d_b.astype(f32).reshape(b, t, 2, DN_HEADS)
    dec_in = d_a.astype(f32).reshape(b, t, 2, DN_HEADS)
    s0 = jnp.zeros((b, 2, DN_HEADS, DN_DK, DN_DV), f32) if is_ctx else ctx[5].astype(f32)
    dn_out, dn_fin = [], []
    for dr in range(2):
        beta = jax.nn.sigmoid(beta_in[:, :, dr])
        g = -jnp.exp(p['dn_a_log'][dr].astype(f32)) * jax.nn.softplus(dec_in[:, :, dr] + p['dn_dt_bias'][dr].astype(f32))
        seq = (q_d, k_d, v_d, beta, g)
        if dr == 1:
            seq = tuple(z[:, ::-1] for z in seq)
        o_dr, s_dr = gated_delta_chunked(seq[0], seq[1], seq[2], seq[3], seq[4], s0[:, dr])
        dn_out.append(o_dr if dr == 0 else o_dr[:, ::-1])
        dn_fin.append(s_dr)
    o_dn = rmsnorm(dn_out[0] + dn_out[1], p['dn_norm_g']) * jax.nn.silu(d_z.astype(f32).reshape(b, t, DN_HEADS, DN_DV))
    o_d = o_dn.reshape(b, t, BRANCH_W).astype(hn.dtype)

    o = jnp.stack([o_a, o_b, o_c, o_d], axis=2)
    br = jnp.einsum('btnw,nwd->btnd', o, p['w_branch'])
    gate = jax.nn.sigmoid(g_lin.reshape(b, t, N_BRANCH, D_MODEL))
    y = jnp.sum(gate * br, axis=2) @ p['w_out']
    if is_ctx:
        new_ctx = (k_b, v_b, k_c, v_c, jnp.stack(lru_fin, axis=1), jnp.stack(dn_fin, axis=1))
    else:
        new_ctx = None
    return y, new_ctx


def trunk_layer(x, mod, p, ctx):
    m = [mod[:, i][:, None, :] for i in range(N_MOD)]
    h = rmsnorm(x, p['norm_g'][0]) * (1 + m[1]) + m[0]
    x = x + 0.5 * m[2] * swiglu(h, p['w_ffn_gate'][0], p['w_ffn_up'][0], p['w_ffn_down'][0])
    h = rmsnorm(x, p['norm_g'][1]) * (1 + m[4]) + m[3]
    y, new_ctx = mixer(h, p, ctx)
    x = x + m[5] * y
    h = rmsnorm(x, p['norm_g'][2]) * (1 + m[7]) + m[6]
    x = x + 0.5 * m[8] * swiglu(h, p['w_ffn_gate'][1], p['w_ffn_up'][1], p['w_ffn_down'][1])
    return x, new_ctx


def setup_inputs(seed: int = 0) -> dict:
    key = jax.random.key(seed)
    ks = list(jax.random.split(key, 40))
    f32 = jnp.float32

    def nrm(i, shape, s):
        return jax.random.normal(ks[i], shape, f32) * s

    u_lam = jax.random.uniform(ks[30], (DEPTH, 2, LRU_W), f32, 0.9, 0.999)
    dt = jnp.exp(jax.random.uniform(ks[31], (DEPTH, 2, DN_HEADS), f32, float(np.log(1e-3)), float(np.log(1e-1))))
    return {
        'x_prompt': nrm(0, (BATCH, SEQ, D_MODEL), 1.0),
        'x_sample': nrm(1, (DEC_BATCH, DEC_SEQ, D_MODEL), 1.0),
        'c': nrm(2, (DEC_BATCH, D_MODEL), 1.0),
        'cache_attn_k': nrm(3, (DEC_BATCH, DEPTH, PAST_LEN, GQA_KV, HEAD_DIM), 1.0),
        'cache_attn_v': nrm(4, (DEC_BATCH, DEPTH, PAST_LEN, GQA_KV, HEAD_DIM), 1.0),
        'cache_na_k': nrm(5, (DEC_BATCH, DEPTH, PAST_LEN, NA_HEADS, HEAD_DIM), 1.0),
        'cache_na_v': nrm(6, (DEC_BATCH, DEPTH, PAST_LEN, NA_HEADS, HEAD_DIM), 1.0),
        'state_lru': nrm(7, (DEC_BATCH, DEPTH, 2, LRU_W), 0.5),
        'state_delta': nrm(8, (DEC_BATCH, DEPTH, 2, DN_HEADS, DN_DK, DN_DV), DN_DK ** -0.5),
        'c_ctx': nrm(9, (D_MODEL,), 1.0),
        'w_mod': nrm(10, (DEPTH, D_MODEL, N_MOD * D_MODEL), 0.5 * D_MODEL ** -0.5),
        'b_mod': nrm(11, (DEPTH, N_MOD * D_MODEL), 0.02),
        'norm_g': 1.0 + nrm(12, (DEPTH, 3, D_MODEL), 0.02),
        'w_ffn_gate': nrm(13, (DEPTH, 2, D_MODEL, D_FF), D_MODEL ** -0.5),
        'w_ffn_up': nrm(14, (DEPTH, 2, D_MODEL, D_FF), D_MODEL ** -0.5),
        'w_ffn_down': nrm(15, (DEPTH, 2, D_FF, D_MODEL), D_FF ** -0.5),
        'w_in': nrm(16, (DEPTH, D_MODEL, N_IN), D_MODEL ** -0.5),
        'lru_conv_w': nrm(17, (DEPTH, CONV_W, LRU_W), CONV_W ** -0.5),
        'lru_conv_b': nrm(18, (DEPTH, LRU_W), 0.02),
        'lru_w_r': nrm(19, (DEPTH, 2, LRU_BLOCKS, LRU_BW, LRU_BW), LRU_BW ** -0.5),
        'lru_b_r': nrm(20, (DEPTH, 2, LRU_W), 0.02),
        'lru_w_i': nrm(21, (DEPTH, 2, LRU_BLOCKS, LRU_BW, LRU_BW), LRU_BW ** -0.5),
        'lru_b_i': nrm(22, (DEPTH, 2, LRU_W), 0.02),
        'lru_lambda': jnp.log(u_lam) - jnp.log1p(-u_lam),
        'gqa_q_norm': 1.0 + nrm(23, (DEPTH, HEAD_DIM), 0.02),
        'gqa_k_norm': 1.0 + nrm(24, (DEPTH, HEAD_DIM), 0.02),
        'na_rpb': nrm(25, (DEPTH, NA_HEADS, 2 * NA_WIN_R - 1, 2 * NA_WIN_C - 1), 0.1),
        'dn_conv_w': nrm(26, (DEPTH, CONV_W, DN_CONV_CH), CONV_W ** -0.5),
        'dn_a_log': jnp.log(jax.random.uniform(ks[27], (DEPTH, 2, DN_HEADS), f32, 1.0, 16.0)),
        'dn_dt_bias': dt + jnp.log(-jnp.expm1(-dt)),
        'dn_norm_g': 1.0 + nrm(28, (DEPTH, DN_DV), 0.02),
        'w_branch': nrm(29, (DEPTH, N_BRANCH, BRANCH_W, D_MODEL), BRANCH_W ** -0.5),
        'w_out': nrm(32, (DEPTH, D_MODEL, D_MODEL), D_MODEL ** -0.5),
        'final_norm_g': 1.0 + nrm(33, (D_MODEL,), 0.02),
    }


def reference(x_prompt, x_sample, c, cache_attn_k, cache_attn_v, cache_na_k, cache_na_v,
              state_lru, state_delta, c_ctx, w_mod, b_mod, norm_g, w_ffn_gate, w_ffn_up,
              w_ffn_down, w_in, lru_conv_w, lru_conv_b, lru_w_r, lru_b_r, lru_w_i, lru_b_i,
              lru_lambda, gqa_q_norm, gqa_k_norm, na_rpb, dn_conv_w, dn_a_log, dn_dt_bias,
              dn_norm_g, w_branch, w_out, final_norm_g):
    xp, xs = x_prompt, x_sample
    silu_c = jax.nn.silu(c)
    silu_ctx = jax.nn.silu(c_ctx)[None, :]
    new_ak, new_av, new_nk, new_nv, new_lru, new_dn = [], [], [], [], [], []
    for l in range(DEPTH):
        p = {
            'norm_g': norm_g[l], 'w_ffn_gate': w_ffn_gate[l], 'w_ffn_up': w_ffn_up[l],
            'w_ffn_down': w_ffn_down[l], 'w_in': w_in[l], 'lru_conv_w': lru_conv_w[l],
            'lru_conv_b': lru_conv_b[l], 'lru_w_r': lru_w_r[l], 'lru_b_r': lru_b_r[l],
            'lru_w_i': lru_w_i[l], 'lru_b_i': lru_b_i[l], 'lru_lambda': lru_lambda[l],
            'gqa_q_norm': gqa_q_norm[l], 'gqa_k_norm': gqa_k_norm[l], 'na_rpb': na_rpb[l],
            'dn_conv_w': dn_conv_w[l], 'dn_a_log': dn_a_log[l], 'dn_dt_bias': dn_dt_bias[l],
            'dn_norm_g': dn_norm_g[l], 'w_branch': w_branch[l], 'w_out': w_out[l],
        }
        mod_ctx = (silu_ctx @ w_mod[l] + b_mod[l]).reshape(1, N_MOD, D_MODEL)
        mod_lat = (silu_c @ w_mod[l] + b_mod[l]).reshape(-1, N_MOD, D_MODEL)
        xp, ctx_new = trunk_layer(xp, mod_ctx, p, None)
        new_ak.append(ctx_new[0])
        new_av.append(ctx_new[1])
        new_nk.append(ctx_new[2])
        new_nv.append(ctx_new[3])
        new_lru.append(ctx_new[4])
        new_dn.append(ctx_new[5])
        ctx_cached = (cache_attn_k[:, l], cache_attn_v[:, l], cache_na_k[:, l], cache_na_v[:, l],
                      state_lru[:, l], state_delta[:, l])
        xs, _ = trunk_layer(xs, mod_lat, p, ctx_cached)
    y_prompt = rmsnorm(xp, final_norm_g)
    y_sample = rmsnorm(xs, final_norm_g)
    return (y_prompt, y_sample, jnp.stack(new_ak, axis=1), jnp.stack(new_av, axis=1),
            jnp.stack(new_nk, axis=1), jnp.stack(new_nv, axis=1),
            jnp.stack(new_lru, axis=1), jnp.stack(new_dn, axis=1))
```

```python
import functools

import numpy as np
import jax
import jax.numpy as jnp
from jax import lax
from jax.experimental import pallas as pl
from jax.experimental.pallas import tpu as pltpu

F32 = jnp.float32
BF16 = jnp.bfloat16

D_MODEL = 1024
DEPTH = 4
GRID_W = 64
N_BRANCH = 4
BRANCH_W = 512
N_MOD = 9
D_FF = 2816
EPS = 1e-6
CONV_W = 4
LRU_BLOCKS = 8
LRU_BW = 64
LRU_C = 8.0
HEAD_DIM = 64
GQA_HEADS = 8
GQA_KV = 2
ROPE_BASE = 10000.0
NA_HEADS = 8
NA_WIN_R = 8
NA_WIN_C = 16
DN_DK = 128
DN_HEADS = 4
DN_CHUNK = 64
PAST_LEN = 512
IN_WIDTHS = (512, 512, 512, 128, 128, 512, 512, 512, 512, 512, 512, 512, 8, 8, 4096)

LANES = 128
SUBLANES = 8
PROJ_W = 5632
COL_BQ, COL_CQ, COL_CK, COL_CV, COL_DQ, COL_DZ = 2, 3, 4, 5, 6, 9
COL_BK128, COL_BV128, COL_SC128 = 40, 41, 42
VMEM_LIMIT = 56 * 1024 * 1024
NEG_BIG = -1e30
GQA_PERM = (0, 4, 1, 5, 2, 6, 3, 7)


def _params(n):
    return pltpu.CompilerParams(dimension_semantics=("arbitrary",) * n, vmem_limit_bytes=VMEM_LIMIT)


def _const_spec(shape):
    nd = len(shape)
    return pl.BlockSpec(shape, lambda *_: (0,) * nd, pipeline_mode=pl.Buffered(1))


def _dot(a, b):
    return jnp.dot(a, b, preferred_element_type=F32)


def _dot_nt(a, b):
    return lax.dot_general(a, b, (((1,), (1,)), ((), ())), preferred_element_type=F32)


def _dot_tn(a, b):
    return lax.dot_general(a, b, (((0,), (0,)), ((), ())), preferred_element_type=F32)


def _split(x):
    hi = x.astype(BF16)
    lo = (x - hi.astype(F32)).astype(BF16)
    return hi, lo


def _dot3(a, b):
    ah, al = _split(a)
    bh, bl = _split(b)
    return _dot(ah, bh) + (_dot(al, bh) + _dot(ah, bl))


def _sigmoid(x):
    return jax.nn.sigmoid(x)


def _modnorm(x, g, shift, scale):
    ms = jnp.mean(x * x, axis=-1, keepdims=True)
    return (x * lax.rsqrt(ms + EPS) * g) * (1.0 + scale) + shift


def _mod_kernel(c_ref, w_ref, b_ref, o_ref):
    c = c_ref[...]
    o_ref[0] = _dot3(c * _sigmoid(c), w_ref[0]) + b_ref[0]


def _mod_call(cs, w_mod, b_mod):
    tn = 1024
    n = N_MOD * D_MODEL
    return pl.pallas_call(
        _mod_kernel,
        grid=(DEPTH, n // tn),
        in_specs=[pl.BlockSpec((SUBLANES, D_MODEL), lambda l, j: (0, 0)),
                  pl.BlockSpec((1, D_MODEL, tn), lambda l, j: (l, 0, j)),
                  pl.BlockSpec((1, 1, tn), lambda l, j: (l, 0, j))],
        out_specs=pl.BlockSpec((1, SUBLANES, tn), lambda l, j: (l, 0, j)),
        out_shape=jax.ShapeDtypeStruct((DEPTH, SUBLANES, n), F32),
        compiler_params=_params(2),
        name="mod",
    )(cs, w_mod, b_mod.reshape(DEPTH, 1, n))


def _ffn_kernel(x_ref, mod_ref, g_ref, wg_ref, wu_ref, wd_ref, gf_ref, o_ref, *, mi, final):
    x = x_ref[0]
    mod = mod_ref[0]
    h = _modnorm(x, g_ref[...], mod[mi:mi + 1], mod[mi + 1:mi + 2]).astype(BF16)
    gt = _dot(h, wg_ref[...])
    up = _dot(h, wu_ref[...])
    a = (gt * _sigmoid(gt) * up).astype(BF16)
    y = x + 0.5 * mod[mi + 2:mi + 3] * _dot(a, wd_ref[...])
    if final:
        ms = jnp.mean(y * y, axis=-1, keepdims=True)
        y = y * lax.rsqrt(ms + EPS) * gf_ref[...]
    o_ref[0] = y


def _ffn_call(x, mod, g, wg, wu, wd, gf, *, mi, final, tm):
    nb, rows, _ = x.shape
    return pl.pallas_call(
        functools.partial(_ffn_kernel, mi=mi, final=final),
        grid=(nb, rows // tm),
        in_specs=[pl.BlockSpec((1, tm, D_MODEL), lambda b, i: (b, i, 0)),
                  pl.BlockSpec((1, N_MOD, D_MODEL), lambda b, i: (b, 0, 0)),
                  _const_spec((1, D_MODEL)),
                  _const_spec((D_MODEL, D_FF)),
                  _const_spec((D_MODEL, D_FF)),
                  _const_spec((D_FF, D_MODEL)),
                  _const_spec((1, D_MODEL))],
        out_specs=pl.BlockSpec((1, tm, D_MODEL), lambda b, i: (b, i, 0)),
        out_shape=jax.ShapeDtypeStruct(x.shape, F32),
        compiler_params=_params(2),
        name="ffn",
    )(x, mod, g, wg, wu, wd, gf)


def _inproj_kernel(x_ref, mod_ref, g_ref, w_ref, o_ref):
    mod = mod_ref[0]
    h = _modnorm(x_ref[0], g_ref[...], mod[3:4], mod[4:5]).astype(BF16)
    o_ref[...] = _dot(h, w_ref[...])


def _inproj_call(x, mod, g, w, *, tm):
    nb, rows, _ = x.shape
    nt = rows // tm
    return pl.pallas_call(
        _inproj_kernel,
        grid=(nb, nt),
        in_specs=[pl.BlockSpec((1, tm, D_MODEL), lambda b, i: (b, i, 0)),
                  pl.BlockSpec((1, N_MOD, D_MODEL), lambda b, i: (b, 0, 0)),
                  _const_spec((1, D_MODEL)),
                  _const_spec((D_MODEL, PROJ_W))],
        out_specs=pl.BlockSpec((tm, PROJ_W), lambda b, i: (b * nt + i, 0)),
        out_shape=jax.ShapeDtypeStruct((nb * rows, PROJ_W), F32),
        compiler_params=_params(2),
        name="inproj",
    )(x, mod, g, w)


def _merge_kernel(x_ref, mod_ref, g_ref, oa_ref, ob_ref, oc_ref, od_ref, wgate_ref, wb_ref, wout_ref, o_ref):
    x = x_ref[0]
    mod = mod_ref[0]
    h = _modnorm(x, g_ref[...], mod[3:4], mod[4:5]).astype(BF16)
    acc = None
    for n, ref in enumerate((oa_ref, ob_ref, oc_ref, od_ref)):
        gate = _sigmoid(_dot(h, wgate_ref[:, n * D_MODEL:(n + 1) * D_MODEL]))
        term = gate * _dot(ref[...].astype(BF16), wb_ref[n])
        acc = term if acc is None else acc + term
    o_ref[0] = x + mod[5:6] * _dot(acc.astype(BF16), wout_ref[...])


def _merge_call(x, mod, g, outs, wgate, wb, wout, *, tm):
    nb, rows, _ = x.shape
    nt = rows // tm
    ospec = pl.BlockSpec((tm, BRANCH_W), lambda b, i: (b * nt + i, 0))
    return pl.pallas_call(
        _merge_kernel,
        grid=(nb, nt),
        in_specs=[pl.BlockSpec((1, tm, D_MODEL), lambda b, i: (b, i, 0)),
                  pl.BlockSpec((1, N_MOD, D_MODEL), lambda b, i: (b, 0, 0)),
                  _const_spec((1, D_MODEL)),
                  ospec, ospec, ospec, ospec,
                  _const_spec((D_MODEL, N_BRANCH * D_MODEL)),
                  _const_spec((N_BRANCH, BRANCH_W, D_MODEL)),
                  _const_spec((D_MODEL, D_MODEL))],
        out_specs=pl.BlockSpec((1, tm, D_MODEL), lambda b, i: (b, i, 0)),
        out_shape=jax.ShapeDtypeStruct(x.shape, F32),
        compiler_params=_params(2),
        name="merge",
    )(x, mod, g, *outs, wgate, wb, wout)


def _log_sigmoid(x):
    return jnp.minimum(x, 0.0) - jnp.log1p(jnp.exp(-jnp.abs(x)))


def _gelu_tanh(x):
    return x * (0.5 * (1.0 + jnp.tanh(0.7978845608028654 * (x + 0.044715 * (x * x * x)))))


def _lru_kernel(ax_ref, ay_ref, h0_ref, cw_ref, cb_ref, wg_ref, bg_ref, lam_ref, o_ref, fin_ref,
                xp_ref, a_ref, u_ref, *, seq):
    lc = seq // SUBLANES
    zero8 = jnp.zeros((SUBLANES, LANES), F32)
    xp_ref[0:SUBLANES, :] = zero8
    xp_ref[SUBLANES + seq:2 * SUBLANES + seq, :] = zero8
    xp_ref[SUBLANES:SUBLANES + seq, :] = ax_ref[...]
    cw = cw_ref[...]
    xa = cb_ref[...] + cw[0:1] * xp_ref[pl.ds(SUBLANES - 2, seq), :]
    for k in range(1, CONV_W):
        xa = xa + cw[k:k + 1] * xp_ref[pl.ds(SUBLANES - 2 + k, seq), :]
    gates = _dot(xa.astype(BF16), wg_ref[0]) + bg_ref[0]
    lam = lam_ref[...]
    for dr in range(2):
        r = _sigmoid(gates[:, (2 * dr) * LANES:(2 * dr + 1) * LANES])
        i = _sigmoid(gates[:, (2 * dr + 1) * LANES:(2 * dr + 2) * LANES])
        log_a = (LRU_C * r) * _log_sigmoid(lam[dr:dr + 1])
        a = jnp.exp(log_a)
        a_ref[dr] = a
        u_ref[dr] = jnp.sqrt(-jnp.tanh(log_a) * (a * a + 1.0)) * (i * xa)

    sub = lax.broadcasted_iota(jnp.int32, (SUBLANES, LANES), 0)
    h0 = h0_ref[0]
    hf0 = jnp.where(sub == 0, h0[0:1], 0.0)
    hb0 = jnp.where(sub == SUBLANES - 1, h0[1:2], 0.0)
    ones = jnp.ones((SUBLANES, LANES), F32)

    def body(s, carry):
        hf, pf, hb, pb = carry
        rows_f = pl.ds(s, SUBLANES, stride=lc)
        rows_b = pl.ds(lc - 1 - s, SUBLANES, stride=lc)
        af = a_ref[0, rows_f, :]
        hf = af * hf + u_ref[0, rows_f, :]
        pf = af * pf
        u_ref[0, rows_f, :] = hf
        a_ref[0, rows_f, :] = pf
        ab = a_ref[1, rows_b, :]
        hb = ab * hb + u_ref[1, rows_b, :]
        pb = ab * pb
        u_ref[1, rows_b, :] = hb
        a_ref[1, rows_b, :] = pb
        return hf, pf, hb, pb

    hf, pf, hb, pb = lax.fori_loop(0, lc, body, (hf0, ones, hb0, ones))

    cf = [jnp.zeros((1, LANES), F32)]
    for k in range(1, SUBLANES):
        cf.append(hf[k - 1:k] + pf[k - 1:k] * cf[k - 1])
    fin_ref[0, 0:1, :] = hf[SUBLANES - 1:SUBLANES] + pf[SUBLANES - 1:SUBLANES] * cf[SUBLANES - 1]
    cb = [None] * SUBLANES
    cb[SUBLANES - 1] = jnp.zeros((1, LANES), F32)
    for k in range(SUBLANES - 2, -1, -1):
        cb[k] = hb[k + 1:k + 2] + pb[k + 1:k + 2] * cb[k + 1]
    fin_ref[0, 1:2, :] = hb[0:1] + pb[0:1] * cb[0]

    for k in range(SUBLANES):
        rows = pl.ds(k * lc, lc)
        h = (u_ref[0, rows, :] + a_ref[0, rows, :] * cf[k]) + (u_ref[1, rows, :] + a_ref[1, rows, :] * cb[k])
        o_ref[rows, :] = h * _gelu_tanh(ay_ref[rows, :])


def _lru_call(proj, h0, cw, cb, wg, bg, lam, *, batch, seq):
    ncol = BRANCH_W // LANES
    return pl.pallas_call(
        functools.partial(_lru_kernel, seq=seq),
        grid=(batch, ncol),
        in_specs=[pl.BlockSpec((seq, LANES), lambda b, c: (b, c)),
                  pl.BlockSpec((seq, LANES), lambda b, c: (b, ncol + c)),
                  pl.BlockSpec((1, 2, LANES), lambda b, c: (b, 0, c)),
                  pl.BlockSpec((CONV_W, LANES), lambda b, c: (0, c)),
                  pl.BlockSpec((1, LANES), lambda b, c: (0, c)),
                  pl.BlockSpec((1, LANES, 4 * LANES), lambda b, c: (c, 0, 0)),
                  pl.BlockSpec((1, 1, 4 * LANES), lambda b, c: (c, 0, 0)),
                  pl.BlockSpec((2, LANES), lambda b, c: (0, c))],
        out_specs=[pl.BlockSpec((seq, LANES), lambda b, c: (b, c)),
                   pl.BlockSpec((1, 2, LANES), lambda b, c: (b, 0, c))],
        out_shape=[jax.ShapeDtypeStruct((batch * seq, BRANCH_W), F32),
                   jax.ShapeDtypeStruct((batch, 2, BRANCH_W), F32)],
        scratch_shapes=[pltpu.VMEM((seq + 2 * SUBLANES, LANES), F32),
                        pltpu.VMEM((2, seq, LANES), F32),
                        pltpu.VMEM((2, seq, LANES), F32)],
        compiler_params=_params(2),
        name="lru",
    )(proj, proj, h0, cw, cb, wg, bg, lam)


def _prep_kernel(*refs, rope):
    if rope:
        q_ref, kv_ref, gq_ref, gk_ref, cos_ref, sin_ref, qn_ref, kn_ref = refs
    else:
        q_ref, kv_ref, gq_ref, gk_ref, qn_ref, kn_ref = refs
    rows = q_ref.shape[0]
    lane = lax.broadcasted_iota(jnp.int32, (rows, LANES), 1)
    lo = lane < HEAD_DIM
    first16 = (lane & 16) == 0

    def head_norm(x, g):
        sq = x * x
        s_lo = jnp.sum(jnp.where(lo, sq, 0.0), axis=-1, keepdims=True)
        s_hi = jnp.sum(jnp.where(lo, 0.0, sq), axis=-1, keepdims=True)
        ms = jnp.where(lo, s_lo, s_hi) * (1.0 / HEAD_DIM)
        return x * lax.rsqrt(ms + EPS) * g

    def rotate(y):
        if not rope:
            return y
        partner = jnp.where(first16, pltpu.roll(y, LANES - 16, 1), pltpu.roll(y, 16, 1))
        return y * cos_ref[...] + partner * sin_ref[...]

    for p in range(BRANCH_W // LANES):
        cols = slice(p * LANES, (p + 1) * LANES)
        qn_ref[:, cols] = rotate(head_norm(q_ref[:, cols], gq_ref[...])) * (HEAD_DIM ** -0.5)
    kn_ref[...] = rotate(head_norm(kv_ref[:, 0:LANES], gk_ref[...]))


def _prep_call(proj, gq, gk, cos, sin, *, batch, seq, rope):
    tab = pl.BlockSpec((seq, LANES), lambda b: (0, 0))
    vec = pl.BlockSpec((1, LANES), lambda b: (0, 0))
    in_specs = [pl.BlockSpec((seq, BRANCH_W), lambda b: (b, COL_BQ)),
                pl.BlockSpec((seq, BRANCH_W), lambda b: (b, COL_BK128 // 4)),
                vec, vec]
    args = [proj, proj, gq, gk]
    if rope:
        in_specs += [tab, tab]
        args += [cos, sin]
    return pl.pallas_call(
        functools.partial(_prep_kernel, rope=rope),
        grid=(batch,),
        in_specs=in_specs,
        out_specs=[pl.BlockSpec((seq, BRANCH_W), lambda b: (b, 0)),
                   pl.BlockSpec((seq, LANES), lambda b: (b, 0))],
        out_shape=[jax.ShapeDtypeStruct((batch * seq, BRANCH_W), F32),
                   jax.ShapeDtypeStruct((batch * seq, LANES), F32)],
        compiler_params=_params(1),
        name="attn_prep",
    )(*args)


def _softmax_pv(scores, values):
    m = None
    for s in scores:
        sm = jnp.max(s, axis=-1, keepdims=True)
        m = sm if m is None else jnp.maximum(m, sm)
    den = None
    out = None
    for s, v in zip(scores, values):
        p = jnp.exp(s - m)
        ps = jnp.sum(p, axis=-1, keepdims=True)
        den = ps if den is None else den + ps
        o = _dot(p.astype(BF16), v)
        out = o if out is None else out + o
    return out / den


def _attn_kernel(*refs, nsrc, nkb, qscale):
    q_ref = refs[0]
    src = refs[1:1 + 2 * nsrc]
    o_ref = refs[1 + 2 * nsrc]
    tq = q_ref.shape[0]
    lo = lax.broadcasted_iota(jnp.int32, (tq, LANES), 1) < HEAD_DIM

    def load_kv(col):
        cols = slice(col * LANES, (col + 1) * LANES)
        return ([src[2 * i][:, cols].astype(BF16) for i in range(nsrc)],
                [src[2 * i + 1][:, cols].astype(BF16) for i in range(nsrc)])

    if nkb == 1:
        ks, vs = load_kv(0)
    for p in range(BRANCH_W // LANES):
        cols = slice(p * LANES, (p + 1) * LANES)
        if nkb != 1:
            ks, vs = load_kv(p)
        qb = q_ref[:, cols]
        if qscale != 1.0:
            qb = qb * qscale
        halves = []
        for half in range(2):
            qm = jnp.where(lo if half == 0 else jnp.logical_not(lo), qb, 0.0).astype(BF16)
            halves.append(_softmax_pv([_dot_nt(qm, k) for k in ks], vs))
        o_ref[:, cols] = jnp.where(lo, halves[0], halves[1])


def _attn_call(q_arr, q_col, sources, *, batch, seq, tq, nkb, qscale):
    nq = seq // tq
    in_specs = [pl.BlockSpec((tq, BRANCH_W), lambda b, i: (b * nq + i, q_col))]
    args = [q_arr]
    for k_arr, k_spec, v_arr, v_spec in sources:
        in_specs += [k_spec, v_spec]
        args += [k_arr, v_arr]
    return pl.pallas_call(
        functools.partial(_attn_kernel, nsrc=len(sources), nkb=nkb, qscale=qscale),
        grid=(batch, nq),
        in_specs=in_specs,
        out_specs=pl.BlockSpec((tq, BRANCH_W), lambda b, i: (b * nq + i, 0)),
        out_shape=jax.ShapeDtypeStruct((batch * seq, BRANCH_W), F32),
        compiler_params=_params(2),
        name="attn",
    )(*args)


def _na_row_start(r, rows):
    return jnp.clip(r - NA_WIN_R // 2, 0, rows - NA_WIN_R)


def _na_kernel(q_ref, k_ref, v_ref, ck_ref, cv_ref, bias_ref, o_ref, *, rows):
    r = pl.program_id(1)
    start = pl.multiple_of(_na_row_start(r, rows) * GRID_W, GRID_W)
    win = pl.ds(start, NA_WIN_R * GRID_W)
    lo = lax.broadcasted_iota(jnp.int32, (GRID_W, LANES), 1) < HEAD_DIM
    for p in range(BRANCH_W // LANES):
        cols = slice(p * LANES, (p + 1) * LANES)
        qb = q_ref[:, cols] * (HEAD_DIM ** -0.5)
        kw = k_ref[win, cols].astype(BF16)
        vw = v_ref[win, cols].astype(BF16)
        kc = ck_ref[:, cols].astype(BF16)
        vc = cv_ref[:, cols].astype(BF16)
        halves = []
        for half in range(2):
            qm = jnp.where(lo if half == 0 else jnp.logical_not(lo), qb, 0.0).astype(BF16)
            s_loc = _dot_nt(qm, kw) + bias_ref[2 * p + half]
            halves.append(_softmax_pv([s_loc, _dot_nt(qm, kc)], [vw, vc]))
        o_ref[:, cols] = jnp.where(lo, halves[0], halves[1])


def _na_call(proj, cache_k, cache_v, bias, layer, *, batch, seq):
    rows = seq // GRID_W
    nwin = NA_WIN_R * GRID_W

    def bias_map(b, r):
        return (_na_row_start(r, rows) - r + NA_WIN_R - 1, 0, 0, 0)

    cache_spec = pl.BlockSpec((None, None, PAST_LEN, BRANCH_W), lambda b, r: (b, layer, 0, 0))
    return pl.pallas_call(
        functools.partial(_na_kernel, rows=rows),
        grid=(batch, rows),
        in_specs=[pl.BlockSpec((GRID_W, BRANCH_W), lambda b, r: (b * rows + r, COL_CQ)),
                  pl.BlockSpec((seq, BRANCH_W), lambda b, r: (b, COL_CK)),
                  pl.BlockSpec((seq, BRANCH_W), lambda b, r: (b, COL_CV)),
                  cache_spec, cache_spec,
                  pl.BlockSpec((None, NA_HEADS, GRID_W, nwin), bias_map)],
        out_specs=pl.BlockSpec((GRID_W, BRANCH_W), lambda b, r: (b * rows + r, 0)),
        out_shape=jax.ShapeDtypeStruct((batch * seq, BRANCH_W), F32),
        compiler_params=_params(2),
        name="na",
    )(proj, proj, proj, cache_k, cache_v, bias)


def _unit_tri_inverse(lm, level_masks, eye):
    x = eye - jnp.where(level_masks[0], lm, 0.0)
    for mask in level_masks[1:]:
        y = _dot3(jnp.where(mask, lm, 0.0), x)
        x = x - _dot3(x, y)
    return x


def _dn_kernel(q_ref, k_ref, v_ref, z_ref, sl_ref, cw_ref, alog_ref, dtb_ref, ng_ref, s0_ref,
               o_ref, sfin_ref,
               xp_ref, qn_ref, kn_ref, vn_ref, col_ref, u_ref, wq_ref, kd_ref, in_ref, gl_ref, of_ref,
               *, seq):
    c_len = DN_CHUNK
    n_chunks = seq // c_len
    head = pl.program_id(1)

    zero8 = jnp.zeros((SUBLANES, LANES), F32)
    for j, (src, dst) in enumerate(((q_ref, qn_ref), (k_ref, kn_ref), (v_ref, vn_ref))):
        xp_ref[0:SUBLANES, :] = zero8
        xp_ref[SUBLANES + seq:2 * SUBLANES + seq, :] = zero8
        xp_ref[SUBLANES:SUBLANES + seq, :] = src[...]
        cw = cw_ref[j]
        y = cw[0:1] * xp_ref[pl.ds(SUBLANES - 2, seq), :]
        for t in range(1, CONV_W):
            y = y + cw[t:t + 1] * xp_ref[pl.ds(SUBLANES - 2 + t, seq), :]
        y = y * _sigmoid(y)
        if j < 2:
            y = y * lax.rsqrt(jnp.sum(y * y, axis=-1, keepdims=True) + EPS)
        if j == 0:
            y = y * (DN_DK ** -0.5)
        dst[...] = y

    sl = sl_ref[...]
    lane = lax.broadcasted_iota(jnp.int32, (seq, LANES), 1)
    beta_all = _sigmoid(sl)
    xs = sl + dtb_ref[...]
    softplus = jnp.maximum(xs, 0.0) + jnp.log1p(jnp.exp(-jnp.abs(xs)))
    g_all = -jnp.exp(alog_ref[...]) * softplus
    cols = jnp.zeros((seq, LANES), F32)
    for dr in range(2):
        beta = jnp.sum(jnp.where(lane == dr * DN_HEADS + head, beta_all, 0.0), axis=-1, keepdims=True)
        g = jnp.sum(jnp.where(lane == 2 * DN_HEADS + dr * DN_HEADS + head, g_all, 0.0), axis=-1, keepdims=True)
        cols = jnp.where(lane == dr, beta, cols)
        cols = jnp.where(lane == 2 + dr, g, cols)
    col_ref[...] = cols

    ii = lax.broadcasted_iota(jnp.int32, (c_len, c_len), 0)
    jj = lax.broadcasted_iota(jnp.int32, (c_len, c_len), 1)
    eye = jnp.where(ii == jj, 1.0, 0.0).astype(F32)
    causal = (ii >= jj, ii <= jj)
    strict = (ii > jj, ii < jj)
    level_masks = []
    for dr in range(2):
        hi_idx, lo_idx = (ii, jj) if dr == 0 else (jj, ii)
        masks = []
        for lvl in range(6):
            masks.append(((hi_idx >> (lvl + 1)) == (lo_idx >> (lvl + 1)))
                         & ((hi_idx >> lvl) == (lo_idx >> lvl) + 1))
        level_masks.append(masks)

    def chunk_body(c, carry):
        rows = pl.ds(pl.multiple_of(c * c_len, c_len), c_len)
        q = qn_ref[rows, :]
        k = kn_ref[rows, :]
        v = vn_ref[rows, :]
        blk = col_ref[rows, :]
        q16 = q.astype(BF16)
        k16 = k.astype(BF16)
        qk = _dot_nt(q16, k16)
        kk = _dot_nt(k16, k16)
        for dr in range(2):
            beta = blk[:, dr:dr + 1]
            g = blk[:, 2 + dr:3 + dr]
            g_b = jnp.broadcast_to(g, (c_len, c_len))
            g_row = jnp.sum(jnp.where(ii == jj, g_b, 0.0), axis=0, keepdims=True)
            gc_row = jnp.sum(jnp.where(causal[1 - dr], g_b, 0.0), axis=0, keepdims=True)
            gc_col = jnp.sum(jnp.where(causal[dr], jnp.broadcast_to(g_row, (c_len, c_len)), 0.0),
                             axis=1, keepdims=True)
            decay = jnp.where(causal[dr], jnp.exp(jnp.where(causal[dr], gc_col - gc_row, 0.0)), 0.0)
            lm = jnp.where(strict[dr], (beta * kk) * decay, 0.0)
            tmat = _unit_tri_inverse(lm, level_masks[dr], eye).astype(BF16)
            eg = jnp.exp(gc_col)
            g_last = gc_col[c_len - 1:c_len] if dr == 0 else gc_col[0:1]
            u_ref[dr, rows, :] = _dot(tmat, (v * beta).astype(BF16))
            wq_ref[dr, pl.ds(pl.multiple_of(2 * c * c_len, c_len), c_len), :] = _dot(
                tmat, (k * (beta * eg)).astype(BF16))
            wq_ref[dr, pl.ds(pl.multiple_of(2 * c * c_len + c_len, c_len), c_len), :] = q * eg
            kd_ref[dr, rows, :] = k * jnp.exp(g_last - gc_col)
            in_ref[dr, rows, :] = qk * decay
            gl_ref[dr, pl.ds(c, 1), :] = jnp.broadcast_to(jnp.exp(g_last), (1, LANES))
        return carry

    lax.fori_loop(0, n_chunks, chunk_body, 0)

    def step(i, states):
        new_states = []
        for dr in range(2):
            c = i if dr == 0 else n_chunks - 1 - i
            rows = pl.ds(pl.multiple_of(c * c_len, c_len), c_len)
            s = states[dr]
            ws_qs = _dot(wq_ref[dr, pl.ds(pl.multiple_of(2 * c * c_len, 2 * c_len), 2 * c_len), :].astype(BF16),
                         s.astype(BF16))
            v_new = u_ref[dr, rows, :] - ws_qs[0:c_len]
            v16 = v_new.astype(BF16)
            of_ref[dr, rows, :] = ws_qs[c_len:2 * c_len] + _dot(in_ref[dr, rows, :].astype(BF16), v16)
            new_states.append(s * gl_ref[dr, pl.ds(c, 1), :]
                              + _dot_tn(kd_ref[dr, rows, :].astype(BF16), v16))
        return tuple(new_states)

    s_fin = lax.fori_loop(0, n_chunks, step, (s0_ref[0], s0_ref[1]))
    sfin_ref[0] = s_fin[0]
    sfin_ref[1] = s_fin[1]

    o = of_ref[0] + of_ref[1]
    y = o * lax.rsqrt(jnp.mean(o * o, axis=-1, keepdims=True) + EPS) * ng_ref[...]
    z = z_ref[...]
    o_ref[...] = y * (z * _sigmoid(z))


def _dn_call(proj, cw, alog, dtb, ng, s0, s0_map, *, batch, seq):
    nh = DN_HEADS
    n_chunks = seq // DN_CHUNK
    vec = pl.BlockSpec((1, LANES), lambda b, h: (0, 0))
    nsub = max(n_chunks, SUBLANES)
    return pl.pallas_call(
        functools.partial(_dn_kernel, seq=seq),
        grid=(batch, nh),
        in_specs=[pl.BlockSpec((seq, LANES), lambda b, h: (b, COL_DQ * 4 + h)),
                  pl.BlockSpec((seq, LANES), lambda b, h: (b, COL_DQ * 4 + nh + h)),
                  pl.BlockSpec((seq, LANES), lambda b, h: (b, COL_DQ * 4 + 2 * nh + h)),
                  pl.BlockSpec((seq, LANES), lambda b, h: (b, COL_DZ * 4 + h)),
                  pl.BlockSpec((seq, LANES), lambda b, h: (b, COL_SC128)),
                  pl.BlockSpec((3, CONV_W, LANES), lambda b, h: (0, 0, h)),
                  vec, vec, vec,
                  pl.BlockSpec((None, None, 2, None, DN_DK, DN_DK), s0_map)],
        out_specs=[pl.BlockSpec((seq, LANES), lambda b, h: (b, h)),
                   pl.BlockSpec((None, 2, None, DN_DK, DN_DK), lambda b, h: (b, 0, h, 0, 0))],
        out_shape=[jax.ShapeDtypeStruct((batch * seq, BRANCH_W), F32),
                   jax.ShapeDtypeStruct((batch, 2, nh, DN_DK, DN_DK), F32)],
        scratch_shapes=[pltpu.VMEM((seq + 2 * SUBLANES, LANES), F32),
                        pltpu.VMEM((seq, LANES), F32),
                        pltpu.VMEM((seq, LANES), F32),
                        pltpu.VMEM((seq, LANES), F32),
                        pltpu.VMEM((seq, LANES), F32),
                        pltpu.VMEM((2, seq, LANES), F32),
                        pltpu.VMEM((2, 2 * seq, LANES), F32),
                        pltpu.VMEM((2, seq, LANES), F32),
                        pltpu.VMEM((2, seq, DN_CHUNK), F32),
                        pltpu.VMEM((2, nsub, LANES), F32),
                        pltpu.VMEM((2, seq, LANES), F32)],
        compiler_params=_params(2),
        name="deltanet",
    )(proj, proj, proj, proj, proj, cw, alog, dtb, ng, s0)


def _layer_params(l, w_ffn_gate, w_ffn_up, w_ffn_down, w_in, lru_conv_w, lru_conv_b, lru_w_r, lru_b_r,
                  lru_w_i, lru_b_i, lru_lambda, gqa_q_norm, gqa_k_norm, na_rpb, dn_conv_w, dn_a_log,
                  dn_dt_bias, dn_norm_g, w_branch, w_out, norm_g):
    offs = np.cumsum((0,) + IN_WIDTHS)
    w = w_in[l]
    seg = [w[:, offs[i]:offs[i + 1]] for i in range(len(IN_WIDTHS))]
    (a_x, a_y, b_q, b_k, b_v, c_q, c_k, c_v, d_q, d_k, d_v, d_z, d_b, d_a, g_lin) = seg
    perm = np.asarray(GQA_PERM)
    b_q = b_q.reshape(D_MODEL, GQA_HEADS, HEAD_DIM)[:, perm].reshape(D_MODEL, BRANCH_W)
    pad = jnp.zeros((D_MODEL, PROJ_W - 5120 - 2 * LANES - 16), F32)
    w_main = jnp.concatenate([a_x, a_y, b_q, c_q, c_k, c_v, d_q, d_k, d_v, d_z, b_k, b_v, d_b, d_a, pad],
                             axis=1).astype(BF16)

    def block_diag(wb):
        wb = wb.reshape(2, LRU_BLOCKS // 2, 2, LRU_BW, LRU_BW)
        z = jnp.zeros_like(wb[:, :, 0])
        return jnp.concatenate([jnp.concatenate([wb[:, :, 0], z], axis=-1),
                                jnp.concatenate([z, wb[:, :, 1]], axis=-1)], axis=-2)

    wr, wi = block_diag(lru_w_r[l]), block_diag(lru_w_i[l])
    lru_wg = jnp.concatenate([wr[0], wi[0], wr[1], wi[1]], axis=-1).astype(BF16)
    ncol = BRANCH_W // LANES

    def col_blocks(v):
        return v.reshape(ncol, LANES)

    lru_bg = jnp.concatenate([col_blocks(lru_b_r[l, 0]), col_blocks(lru_b_i[l, 0]),
                              col_blocks(lru_b_r[l, 1]), col_blocks(lru_b_i[l, 1])], axis=-1)[:, None, :]

    wb = w_branch[l]
    wb_b = wb[1].reshape(GQA_HEADS, HEAD_DIM, D_MODEL)[perm].reshape(BRANCH_W, D_MODEL)
    wb = jnp.stack([wb[0], wb_b, wb[2], wb[3]], axis=0).astype(BF16)

    lane_pad = jnp.zeros((LANES - 4 * DN_HEADS,), F32)
    alog = jnp.concatenate([jnp.zeros((2 * DN_HEADS,), F32), dn_a_log[l].reshape(-1), lane_pad])[None, :]
    dtb = jnp.concatenate([jnp.zeros((2 * DN_HEADS,), F32), dn_dt_bias[l].reshape(-1), lane_pad])[None, :]

    return dict(
        norm_g=norm_g[l][:, None, :],
        wg=w_ffn_gate[l].astype(BF16), wu=w_ffn_up[l].astype(BF16), wd=w_ffn_down[l].astype(BF16),
        w_main=w_main, w_gate=g_lin.astype(BF16),
        lru_cw=lru_conv_w[l], lru_cb=lru_conv_b[l][None, :], lru_wg=lru_wg, lru_bg=lru_bg,
        lru_lam=lru_lambda[l],
        gq=jnp.tile(gqa_q_norm[l], 2)[None, :], gk=jnp.tile(gqa_k_norm[l], 2)[None, :],
        rpb=na_rpb[l],
        dn_cw=dn_conv_w[l].reshape(CONV_W, 3, BRANCH_W).transpose(1, 0, 2),
        dn_alog=alog, dn_dtb=dtb, dn_ng=dn_norm_g[l][None, :],
        wb=wb, w_out=w_out[l].astype(BF16),
    )


def _rope_tables(seq):
    pos = jnp.arange(seq)
    half = HEAD_DIM // 2
    inv = jnp.power(ROPE_BASE, -jnp.arange(0, half, 2, dtype=F32) / half)
    ang_r = (pos // GRID_W).astype(F32)[:, None] * inv[None, :]
    ang_c = (pos % GRID_W).astype(F32)[:, None] * inv[None, :]
    cos = jnp.concatenate([jnp.cos(ang_r)] * 2 + [jnp.cos(ang_c)] * 2, axis=-1)
    sin = jnp.concatenate([-jnp.sin(ang_r), jnp.sin(ang_r), -jnp.sin(ang_c), jnp.sin(ang_c)], axis=-1)
    return jnp.tile(cos, (1, 2)), jnp.tile(sin, (1, 2))


def _na_bias_table(rpb):
    qc = np.arange(GRID_W)
    cs = np.clip(qc - NA_WIN_C // 2, 0, GRID_W - NA_WIN_C)
    kc = np.arange(GRID_W)
    inwin = (kc[None, :] >= cs[:, None]) & (kc[None, :] < cs[:, None] + NA_WIN_C)
    coff = np.clip(kc[None, :] - qc[:, None] + NA_WIN_C - 1, 0, 2 * NA_WIN_C - 2)
    roff = np.arange(NA_WIN_R)[:, None] + np.arange(NA_WIN_R)[None, :]
    t = rpb.astype(F32)[:, roff][:, :, :, coff]
    t = jnp.where(inwin[None, None, None], t, NEG_BIG)
    return jnp.transpose(t, (1, 0, 3, 2, 4)).reshape(NA_WIN_R, NA_HEADS, GRID_W, NA_WIN_R * GRID_W)


def _layer(x, mod, p, *, batch, seq, latent, layer, caches, tables, final_g, tm):
    g = p["norm_g"]
    x = _ffn_call(x, mod, g[0], p["wg"][0], p["wu"][0], p["wd"][0], final_g, mi=0, final=False, tm=tm)
    proj = _inproj_call(x, mod, g[1], p["w_main"], tm=tm)

    if latent:
        cache_ak, cache_av, cache_nk, cache_nv, state_lru, state_delta = caches
        h0 = state_lru[:, layer]
        s0 = state_delta
        s0_map = lambda b, h: (b, layer, 0, h, 0, 0)
    else:
        h0 = jnp.zeros((batch, 2, BRANCH_W), F32)
        s0 = jnp.zeros((1, 1, 2, 1, DN_DK, DN_DK), F32)
        s0_map = lambda b, h: (0, 0, 0, 0, 0, 0)

    o_a, lru_fin = _lru_call(proj, h0, p["lru_cw"], p["lru_cb"], p["lru_wg"], p["lru_bg"], p["lru_lam"],
                             batch=batch, seq=seq)

    cos, sin = tables["rope"] if latent else (None, None)
    qn, kn = _prep_call(proj, p["gq"], p["gk"], cos, sin, batch=batch, seq=seq, rope=latent)
    tq = min(seq, 256)
    kv_new = (kn, pl.BlockSpec((seq, LANES), lambda b, i: (b, 0)),
              proj, pl.BlockSpec((seq, LANES), lambda b, i: (b, COL_BV128)))
    if latent:
        gqa_cache = pl.BlockSpec((None, None, PAST_LEN, LANES), lambda b, i: (b, layer, 0, 0))
        sources = [kv_new, (cache_ak, gqa_cache, cache_av, gqa_cache)]
    else:
        sources = [kv_new]
    o_b = _attn_call(qn, 0, sources, batch=batch, seq=seq, tq=tq, nkb=1, qscale=1.0)

    if latent:
        o_c = _na_call(proj, cache_nk, cache_nv, _na_bias_table(p["rpb"]), layer, batch=batch, seq=seq)
    else:
        src = (proj, pl.BlockSpec((seq, BRANCH_W), lambda b, i: (b, COL_CK)),
               proj, pl.BlockSpec((seq, BRANCH_W), lambda b, i: (b, COL_CV)))
        o_c = _attn_call(proj, COL_CQ, [src], batch=batch, seq=seq, tq=tq, nkb=4, qscale=HEAD_DIM ** -0.5)

    o_d, dn_fin = _dn_call(proj, p["dn_cw"], p["dn_alog"], p["dn_dtb"], p["dn_ng"], s0, s0_map,
                           batch=batch, seq=seq)

    x = _merge_call(x, mod, g[1], (o_a, o_b, o_c, o_d), p["w_gate"], p["wb"], p["w_out"], tm=tm)
    x = _ffn_call(x, mod, g[2], p["wg"][1], p["wu"][1], p["wd"][1], final_g, mi=6,
                  final=(layer == DEPTH - 1), tm=tm)

    new_ctx = None
    if not latent:
        new_ctx = (kn.reshape(batch, seq, GQA_KV, HEAD_DIM),
                   proj[:, COL_BV128 * LANES:(COL_BV128 + 1) * LANES].reshape(batch, seq, GQA_KV, HEAD_DIM),
                   proj[:, COL_CK * BRANCH_W:(COL_CK + 1) * BRANCH_W].reshape(batch, seq, NA_HEADS, HEAD_DIM),
                   proj[:, COL_CV * BRANCH_W:(COL_CV + 1) * BRANCH_W].reshape(batch, seq, NA_HEADS, HEAD_DIM),
                   lru_fin, dn_fin)
    return x, new_ctx


def kernel(x_prompt, x_sample, c, cache_attn_k, cache_attn_v, cache_na_k, cache_na_v, state_lru, state_delta, c_ctx, w_mod, b_mod, norm_g, w_ffn_gate, w_ffn_up, w_ffn_down, w_in, lru_conv_w, lru_conv_b, lru_w_r, lru_b_r, lru_w_i, lru_b_i, lru_lambda, gqa_q_norm, gqa_k_norm, na_rpb, dn_conv_w, dn_a_log, dn_dt_bias, dn_norm_g, w_branch, w_out, final_norm_g):
    batch_c, seq_c, _ = x_prompt.shape
    batch_l, seq_l, _ = x_sample.shape
    assert batch_l + 1 <= SUBLANES

    cs = jnp.concatenate([c_ctx[None, :], c, jnp.zeros((SUBLANES - 1 - batch_l, D_MODEL), F32)], axis=0)
    mod_all = _mod_call(cs, w_mod, b_mod)

    caches = (cache_attn_k.reshape(batch_l, DEPTH, PAST_LEN, GQA_KV * HEAD_DIM),
              cache_attn_v.reshape(batch_l, DEPTH, PAST_LEN, GQA_KV * HEAD_DIM),
              cache_na_k.reshape(batch_l, DEPTH, PAST_LEN, BRANCH_W),
              cache_na_v.reshape(batch_l, DEPTH, PAST_LEN, BRANCH_W),
              state_lru, state_delta)
    tables = {"rope": _rope_tables(seq_l)}
    final_g = final_norm_g[None, :]

    xc = x_prompt.reshape(1, batch_c * seq_c, D_MODEL)
    xl = x_sample
    ctx_out = []
    for l in range(DEPTH):
        p = _layer_params(l, w_ffn_gate, w_ffn_up, w_ffn_down, w_in, lru_conv_w, lru_conv_b, lru_w_r,
                          lru_b_r, lru_w_i, lru_b_i, lru_lambda, gqa_q_norm, gqa_k_norm, na_rpb, dn_conv_w,
                          dn_a_log, dn_dt_bias, dn_norm_g, w_branch, w_out, norm_g)
        mod_c = mod_all[l, 0:1].reshape(1, N_MOD, D_MODEL)
        mod_l = mod_all[l, 1:1 + batch_l].reshape(batch_l, N_MOD, D_MODEL)
        xc, new_ctx = _layer(xc, mod_c, p, batch=batch_c, seq=seq_c, latent=False, layer=l, caches=None,
                             tables=tables, final_g=final_g, tm=256)
        ctx_out.append(new_ctx)
        xl, _ = _layer(xl, mod_l, p, batch=batch_l, seq=seq_l, latent=True, layer=l, caches=caches,
                       tables=tables, final_g=final_g, tm=256)

    stacked = [jnp.stack([ctx_out[l][i] for l in range(DEPTH)], axis=1) for i in range(6)]
    return (xc.reshape(batch_c, seq_c, D_MODEL), xl, *stacked)
```

```python
import functools

import numpy as np
import jax
import jax.numpy as jnp
from jax import lax
from jax.experimental import pallas as pl
from jax.experimental.pallas import tpu as pltpu

F32 = jnp.float32
BF16 = jnp.bfloat16

D_MODEL = 1024
DEPTH = 4
GRID_W = 64
N_BRANCH = 4
BRANCH_W = 512
N_MOD = 9
D_FF = 2816
EPS = 1e-6
CONV_W = 4
LRU_BLOCKS = 8
LRU_BW = 64
LRU_C = 8.0
HEAD_DIM = 64
GQA_HEADS = 8
GQA_KV = 2
ROPE_BASE = 10000.0
NA_HEADS = 8
NA_WIN_R = 8
NA_WIN_C = 16
DN_DK = 128
DN_HEADS = 4
DN_TILE = 128
DN_GROUP = 4
PAST_LEN = 512
IN_WIDTHS = (512, 512, 512, 128, 128, 512, 512, 512, 512, 512, 512, 512, 8, 8, 4096)

LANES = 128
SUBLANES = 8
PROJ_W = 5632
COL_BQ, COL_CQ, COL_CK, COL_CV, COL_DQ, COL_DZ = 2, 3, 4, 5, 6, 9
COL_BK128, COL_BV128, COL_SC128 = 40, 41, 42
VMEM_LIMIT = 56 * 1024 * 1024
NEG_BIG = -1e30
GQA_PERM = (0, 4, 1, 5, 2, 6, 3, 7)


def _params(n):
    return pltpu.CompilerParams(dimension_semantics=("arbitrary",) * n, vmem_limit_bytes=VMEM_LIMIT)


def _const_spec(shape):
    nd = len(shape)
    return pl.BlockSpec(shape, lambda *_: (0,) * nd, pipeline_mode=pl.Buffered(1))


def _dot(a, b):
    return jnp.dot(a, b, preferred_element_type=F32)


def _dot_nt(a, b):
    return lax.dot_general(a, b, (((1,), (1,)), ((), ())), preferred_element_type=F32)


def _dot_tn(a, b):
    return lax.dot_general(a, b, (((0,), (0,)), ((), ())), preferred_element_type=F32)


def _split(x):
    hi = x.astype(BF16)
    lo = (x - hi.astype(F32)).astype(BF16)
    return hi, lo


def _dot3(a, b):
    ah, al = _split(a)
    bh, bl = _split(b)
    return _dot(ah, bh) + (_dot(al, bh) + _dot(ah, bl))


def _sigmoid(x):
    return jax.nn.sigmoid(x)


def _modnorm(x, g, shift, scale):
    ms = jnp.mean(x * x, axis=-1, keepdims=True)
    return (x * lax.rsqrt(ms + EPS) * g) * (1.0 + scale) + shift


def _mod_kernel(c_ref, w_ref, b_ref, o_ref):
    c = c_ref[...]
    o_ref[0] = _dot3(c * _sigmoid(c), w_ref[0]) + b_ref[0]


def _mod_call(cs, w_mod, b_mod):
    tn = 1024
    n = N_MOD * D_MODEL
    return pl.pallas_call(
        _mod_kernel,
        grid=(DEPTH, n // tn),
        in_specs=[pl.BlockSpec((SUBLANES, D_MODEL), lambda l, j: (0, 0)),
                  pl.BlockSpec((1, D_MODEL, tn), lambda l, j: (l, 0, j)),
                  pl.BlockSpec((1, 1, tn), lambda l, j: (l, 0, j))],
        out_specs=pl.BlockSpec((1, SUBLANES, tn), lambda l, j: (l, 0, j)),
        out_shape=jax.ShapeDtypeStruct((DEPTH, SUBLANES, n), F32),
        compiler_params=_params(2),
        name="mod",
    )(cs, w_mod, b_mod.reshape(DEPTH, 1, n))


def _ffn_kernel(x_ref, mod_ref, g_ref, wg_ref, wu_ref, wd_ref, gf_ref, o_ref, *, mi, final):
    x = x_ref[0]
    mod = mod_ref[0]
    h = _modnorm(x, g_ref[...], mod[mi:mi + 1], mod[mi + 1:mi + 2]).astype(BF16)
    gt = _dot(h, wg_ref[...])
    up = _dot(h, wu_ref[...])
    a = (gt * _sigmoid(gt) * up).astype(BF16)
    y = x + 0.5 * mod[mi + 2:mi + 3] * _dot(a, wd_ref[...])
    if final:
        ms = jnp.mean(y * y, axis=-1, keepdims=True)
        y = y * lax.rsqrt(ms + EPS) * gf_ref[...]
    o_ref[0] = y


def _ffn_call(x, mod, g, wg, wu, wd, gf, *, mi, final, tm):
    nb, rows, _ = x.shape
    return pl.pallas_call(
        functools.partial(_ffn_kernel, mi=mi, final=final),
        grid=(nb, rows // tm),
        in_specs=[pl.BlockSpec((1, tm, D_MODEL), lambda b, i: (b, i, 0)),
                  pl.BlockSpec((1, N_MOD, D_MODEL), lambda b, i: (b, 0, 0)),
                  _const_spec((1, D_MODEL)),
                  _const_spec((D_MODEL, D_FF)),
                  _const_spec((D_MODEL, D_FF)),
                  _const_spec((D_FF, D_MODEL)),
                  _const_spec((1, D_MODEL))],
        out_specs=pl.BlockSpec((1, tm, D_MODEL), lambda b, i: (b, i, 0)),
        out_shape=jax.ShapeDtypeStruct(x.shape, F32),
        compiler_params=_params(2),
        name="ffn",
    )(x, mod, g, wg, wu, wd, gf)


def _inproj_kernel(x_ref, mod_ref, g_ref, w_ref, o_ref):
    mod = mod_ref[0]
    h = _modnorm(x_ref[0], g_ref[...], mod[3:4], mod[4:5]).astype(BF16)
    o_ref[...] = _dot(h, w_ref[...])


def _inproj_call(x, mod, g, w, *, tm):
    nb, rows, _ = x.shape
    nt = rows // tm
    return pl.pallas_call(
        _inproj_kernel,
        grid=(nb, nt),
        in_specs=[pl.BlockSpec((1, tm, D_MODEL), lambda b, i: (b, i, 0)),
                  pl.BlockSpec((1, N_MOD, D_MODEL), lambda b, i: (b, 0, 0)),
                  _const_spec((1, D_MODEL)),
                  _const_spec((D_MODEL, PROJ_W))],
        out_specs=pl.BlockSpec((tm, PROJ_W), lambda b, i: (b * nt + i, 0)),
        out_shape=jax.ShapeDtypeStruct((nb * rows, PROJ_W), F32),
        compiler_params=_params(2),
        name="inproj",
    )(x, mod, g, w)


def _merge_kernel(x_ref, mod_ref, g_ref, oa_ref, ob_ref, oc_ref, od_ref, wgate_ref, wb_ref, wout_ref, o_ref):
    x = x_ref[0]
    mod = mod_ref[0]
    h = _modnorm(x, g_ref[...], mod[3:4], mod[4:5]).astype(BF16)
    acc = None
    for n, ref in enumerate((oa_ref, ob_ref, oc_ref, od_ref)):
        gate = _sigmoid(_dot(h, wgate_ref[:, n * D_MODEL:(n + 1) * D_MODEL]))
        term = gate * _dot(ref[...].astype(BF16), wb_ref[n])
        acc = term if acc is None else acc + term
    o_ref[0] = x + mod[5:6] * _dot(acc.astype(BF16), wout_ref[...])


def _merge_call(x, mod, g, outs, wgate, wb, wout, *, tm):
    nb, rows, _ = x.shape
    nt = rows // tm
    ospec = pl.BlockSpec((tm, BRANCH_W), lambda b, i: (b * nt + i, 0))
    return pl.pallas_call(
        _merge_kernel,
        grid=(nb, nt),
        in_specs=[pl.BlockSpec((1, tm, D_MODEL), lambda b, i: (b, i, 0)),
                  pl.BlockSpec((1, N_MOD, D_MODEL), lambda b, i: (b, 0, 0)),
                  _const_spec((1, D_MODEL)),
                  ospec, ospec, ospec, ospec,
                  _const_spec((D_MODEL, N_BRANCH * D_MODEL)),
                  _const_spec((N_BRANCH, BRANCH_W, D_MODEL)),
                  _const_spec((D_MODEL, D_MODEL))],
        out_specs=pl.BlockSpec((1, tm, D_MODEL), lambda b, i: (b, i, 0)),
        out_shape=jax.ShapeDtypeStruct(x.shape, F32),
        compiler_params=_params(2),
        name="merge",
    )(x, mod, g, *outs, wgate, wb, wout)


def _log_sigmoid(x):
    return jnp.minimum(x, 0.0) - jnp.log1p(jnp.exp(-jnp.abs(x)))


def _gelu_tanh(x):
    return x * (0.5 * (1.0 + jnp.tanh(0.7978845608028654 * (x + 0.044715 * (x * x * x)))))


def _lru_kernel(ax_ref, ay_ref, h0_ref, cw_ref, cb_ref, wg_ref, bg_ref, lam_ref, o_ref, fin_ref,
                xp_ref, a_ref, u_ref, *, seq):
    lc = seq // SUBLANES
    zero8 = jnp.zeros((SUBLANES, LANES), F32)
    xp_ref[0:SUBLANES, :] = zero8
    xp_ref[SUBLANES + seq:2 * SUBLANES + seq, :] = zero8
    xp_ref[SUBLANES:SUBLANES + seq, :] = ax_ref[...]
    cw = cw_ref[...]
    xa = cb_ref[...] + cw[0:1] * xp_ref[pl.ds(SUBLANES - 2, seq), :]
    for k in range(1, CONV_W):
        xa = xa + cw[k:k + 1] * xp_ref[pl.ds(SUBLANES - 2 + k, seq), :]
    gates = _dot(xa.astype(BF16), wg_ref[0]) + bg_ref[0]
    lam = lam_ref[...]
    for dr in range(2):
        r = _sigmoid(gates[:, (2 * dr) * LANES:(2 * dr + 1) * LANES])
        i = _sigmoid(gates[:, (2 * dr + 1) * LANES:(2 * dr + 2) * LANES])
        log_a = (LRU_C * r) * _log_sigmoid(lam[dr:dr + 1])
        a = jnp.exp(log_a)
        a_ref[dr] = a
        u_ref[dr] = jnp.sqrt(-jnp.tanh(log_a) * (a * a + 1.0)) * (i * xa)

    sub = lax.broadcasted_iota(jnp.int32, (SUBLANES, LANES), 0)
    h0 = h0_ref[0]
    hf0 = jnp.where(sub == 0, h0[0:1], 0.0)
    hb0 = jnp.where(sub == SUBLANES - 1, h0[1:2], 0.0)
    ones = jnp.ones((SUBLANES, LANES), F32)

    def body(s, carry):
        hf, pf, hb, pb = carry
        rows_f = pl.ds(s, SUBLANES, stride=lc)
        rows_b = pl.ds(lc - 1 - s, SUBLANES, stride=lc)
        af = a_ref[0, rows_f, :]
        hf = af * hf + u_ref[0, rows_f, :]
        pf = af * pf
        u_ref[0, rows_f, :] = hf
        a_ref[0, rows_f, :] = pf
        ab = a_ref[1, rows_b, :]
        hb = ab * hb + u_ref[1, rows_b, :]
        pb = ab * pb
        u_ref[1, rows_b, :] = hb
        a_ref[1, rows_b, :] = pb
        return hf, pf, hb, pb

    hf, pf, hb, pb = lax.fori_loop(0, lc, body, (hf0, ones, hb0, ones))

    cf = [jnp.zeros((1, LANES), F32)]
    for k in range(1, SUBLANES):
        cf.append(hf[k - 1:k] + pf[k - 1:k] * cf[k - 1])
    fin_ref[0, 0:1, :] = hf[SUBLANES - 1:SUBLANES] + pf[SUBLANES - 1:SUBLANES] * cf[SUBLANES - 1]
    cb = [None] * SUBLANES
    cb[SUBLANES - 1] = jnp.zeros((1, LANES), F32)
    for k in range(SUBLANES - 2, -1, -1):
        cb[k] = hb[k + 1:k + 2] + pb[k + 1:k + 2] * cb[k + 1]
    fin_ref[0, 1:2, :] = hb[0:1] + pb[0:1] * cb[0]

    for k in range(SUBLANES):
        rows = pl.ds(k * lc, lc)
        h = (u_ref[0, rows, :] + a_ref[0, rows, :] * cf[k]) + (u_ref[1, rows, :] + a_ref[1, rows, :] * cb[k])
        o_ref[rows, :] = h * _gelu_tanh(ay_ref[rows, :])


def _lru_call(proj, h0, cw, cb, wg, bg, lam, *, batch, seq):
    ncol = BRANCH_W // LANES
    return pl.pallas_call(
        functools.partial(_lru_kernel, seq=seq),
        grid=(batch, ncol),
        in_specs=[pl.BlockSpec((seq, LANES), lambda b, c: (b, c)),
                  pl.BlockSpec((seq, LANES), lambda b, c: (b, ncol + c)),
                  pl.BlockSpec((1, 2, LANES), lambda b, c: (b, 0, c)),
                  pl.BlockSpec((CONV_W, LANES), lambda b, c: (0, c)),
                  pl.BlockSpec((1, LANES), lambda b, c: (0, c)),
                  pl.BlockSpec((1, LANES, 4 * LANES), lambda b, c: (c, 0, 0)),
                  pl.BlockSpec((1, 1, 4 * LANES), lambda b, c: (c, 0, 0)),
                  pl.BlockSpec((2, LANES), lambda b, c: (0, c))],
        out_specs=[pl.BlockSpec((seq, LANES), lambda b, c: (b, c)),
                   pl.BlockSpec((1, 2, LANES), lambda b, c: (b, 0, c))],
        out_shape=[jax.ShapeDtypeStruct((batch * seq, BRANCH_W), F32),
                   jax.ShapeDtypeStruct((batch, 2, BRANCH_W), F32)],
        scratch_shapes=[pltpu.VMEM((seq + 2 * SUBLANES, LANES), F32),
                        pltpu.VMEM((2, seq, LANES), F32),
                        pltpu.VMEM((2, seq, LANES), F32)],
        compiler_params=_params(2),
        name="lru",
    )(proj, proj, h0, cw, cb, wg, bg, lam)


def _prep_kernel(*refs, rope):
    if rope:
        q_ref, kv_ref, gq_ref, gk_ref, cos_ref, sin_ref, qn_ref, kn_ref = refs
    else:
        q_ref, kv_ref, gq_ref, gk_ref, qn_ref, kn_ref = refs
    rows = q_ref.shape[0]
    lane = lax.broadcasted_iota(jnp.int32, (rows, LANES), 1)
    lo = lane < HEAD_DIM
    first16 = (lane & 16) == 0

    def head_norm(x, g):
        sq = x * x
        s_lo = jnp.sum(jnp.where(lo, sq, 0.0), axis=-1, keepdims=True)
        s_hi = jnp.sum(jnp.where(lo, 0.0, sq), axis=-1, keepdims=True)
        ms = jnp.where(lo, s_lo, s_hi) * (1.0 / HEAD_DIM)
        return x * lax.rsqrt(ms + EPS) * g

    def rotate(y):
        if not rope:
            return y
        partner = jnp.where(first16, pltpu.roll(y, LANES - 16, 1), pltpu.roll(y, 16, 1))
        return y * cos_ref[...] + partner * sin_ref[...]

    for p in range(BRANCH_W // LANES):
        cols = slice(p * LANES, (p + 1) * LANES)
        qn_ref[:, cols] = rotate(head_norm(q_ref[:, cols], gq_ref[...])) * (HEAD_DIM ** -0.5)
    kn_ref[...] = rotate(head_norm(kv_ref[:, 0:LANES], gk_ref[...]))


def _prep_call(proj, gq, gk, cos, sin, *, batch, seq, rope):
    tab = pl.BlockSpec((seq, LANES), lambda b: (0, 0))
    vec = pl.BlockSpec((1, LANES), lambda b: (0, 0))
    in_specs = [pl.BlockSpec((seq, BRANCH_W), lambda b: (b, COL_BQ)),
                pl.BlockSpec((seq, BRANCH_W), lambda b: (b, COL_BK128 // 4)),
                vec, vec]
    args = [proj, proj, gq, gk]
    if rope:
        in_specs += [tab, tab]
        args += [cos, sin]
    return pl.pallas_call(
        functools.partial(_prep_kernel, rope=rope),
        grid=(batch,),
        in_specs=in_specs,
        out_specs=[pl.BlockSpec((seq, BRANCH_W), lambda b: (b, 0)),
                   pl.BlockSpec((seq, LANES), lambda b: (b, 0))],
        out_shape=[jax.ShapeDtypeStruct((batch * seq, BRANCH_W), F32),
                   jax.ShapeDtypeStruct((batch * seq, LANES), F32)],
        compiler_params=_params(1),
        name="attn_prep",
    )(*args)


def _softmax_pv(scores, values):
    m = None
    for s in scores:
        sm = jnp.max(s, axis=-1, keepdims=True)
        m = sm if m is None else jnp.maximum(m, sm)
    den = None
    out = None
    for s, v in zip(scores, values):
        p = jnp.exp(s - m)
        ps = jnp.sum(p, axis=-1, keepdims=True)
        den = ps if den is None else den + ps
        o = _dot(p.astype(BF16), v)
        out = o if out is None else out + o
    return out / den


def _attn_kernel(*refs, nsrc, nkb, qscale):
    q_ref = refs[0]
    src = refs[1:1 + 2 * nsrc]
    o_ref = refs[1 + 2 * nsrc]
    tq = q_ref.shape[0]
    lo = lax.broadcasted_iota(jnp.int32, (tq, LANES), 1) < HEAD_DIM

    def load_kv(col):
        cols = slice(col * LANES, (col + 1) * LANES)
        return ([src[2 * i][:, cols].astype(BF16) for i in range(nsrc)],
                [src[2 * i + 1][:, cols].astype(BF16) for i in range(nsrc)])

    if nkb == 1:
        ks, vs = load_kv(0)
    for p in range(BRANCH_W // LANES):
        cols = slice(p * LANES, (p + 1) * LANES)
        if nkb != 1:
            ks, vs = load_kv(p)
        qb = q_ref[:, cols]
        if qscale != 1.0:
            qb = qb * qscale
        halves = []
        for half in range(2):
            qm = jnp.where(lo if half == 0 else jnp.logical_not(lo), qb, 0.0).astype(BF16)
            halves.append(_softmax_pv([_dot_nt(qm, k) for k in ks], vs))
        o_ref[:, cols] = jnp.where(lo, halves[0], halves[1])


def _attn_call(q_arr, q_col, sources, *, batch, seq, tq, nkb, qscale):
    nq = seq // tq
    in_specs = [pl.BlockSpec((tq, BRANCH_W), lambda b, i: (b * nq + i, q_col))]
    args = [q_arr]
    for k_arr, k_spec, v_arr, v_spec in sources:
        in_specs += [k_spec, v_spec]
        args += [k_arr, v_arr]
    return pl.pallas_call(
        functools.partial(_attn_kernel, nsrc=len(sources), nkb=nkb, qscale=qscale),
        grid=(batch, nq),
        in_specs=in_specs,
        out_specs=pl.BlockSpec((tq, BRANCH_W), lambda b, i: (b * nq + i, 0)),
        out_shape=jax.ShapeDtypeStruct((batch * seq, BRANCH_W), F32),
        compiler_params=_params(2),
        name="attn",
    )(*args)


def _na_row_start(r, rows):
    return jnp.clip(r - NA_WIN_R // 2, 0, rows - NA_WIN_R)


def _na_kernel(q_ref, k_ref, v_ref, ck_ref, cv_ref, bias_ref, o_ref, *, rows):
    r = pl.program_id(1)
    start = pl.multiple_of(_na_row_start(r, rows) * GRID_W, GRID_W)
    win = pl.ds(start, NA_WIN_R * GRID_W)
    lo = lax.broadcasted_iota(jnp.int32, (GRID_W, LANES), 1) < HEAD_DIM
    for p in range(BRANCH_W // LANES):
        cols = slice(p * LANES, (p + 1) * LANES)
        qb = q_ref[:, cols] * (HEAD_DIM ** -0.5)
        kw = k_ref[win, cols].astype(BF16)
        vw = v_ref[win, cols].astype(BF16)
        kc = ck_ref[:, cols].astype(BF16)
        vc = cv_ref[:, cols].astype(BF16)
        halves = []
        for half in range(2):
            qm = jnp.where(lo if half == 0 else jnp.logical_not(lo), qb, 0.0).astype(BF16)
            s_loc = _dot_nt(qm, kw) + bias_ref[2 * p + half]
            halves.append(_softmax_pv([s_loc, _dot_nt(qm, kc)], [vw, vc]))
        o_ref[:, cols] = jnp.where(lo, halves[0], halves[1])


def _na_call(proj, cache_k, cache_v, bias, layer, *, batch, seq):
    rows = seq // GRID_W
    nwin = NA_WIN_R * GRID_W

    def bias_map(b, r):
        return (_na_row_start(r, rows) - r + NA_WIN_R - 1, 0, 0, 0)

    cache_spec = pl.BlockSpec((None, None, PAST_LEN, BRANCH_W), lambda b, r: (b, layer, 0, 0))
    return pl.pallas_call(
        functools.partial(_na_kernel, rows=rows),
        grid=(batch, rows),
        in_specs=[pl.BlockSpec((GRID_W, BRANCH_W), lambda b, r: (b * rows + r, COL_CQ)),
                  pl.BlockSpec((seq, BRANCH_W), lambda b, r: (b, COL_CK)),
                  pl.BlockSpec((seq, BRANCH_W), lambda b, r: (b, COL_CV)),
                  cache_spec, cache_spec,
                  pl.BlockSpec((None, NA_HEADS, GRID_W, nwin), bias_map)],
        out_specs=pl.BlockSpec((GRID_W, BRANCH_W), lambda b, r: (b * rows + r, 0)),
        out_shape=jax.ShapeDtypeStruct((batch * seq, BRANCH_W), F32),
        compiler_params=_params(2),
        name="na",
    )(proj, proj, proj, cache_k, cache_v, bias)


def _split3(x):
    hi = x.astype(BF16)
    r = x - hi.astype(F32)
    mid = r.astype(BF16)
    lo = (r - mid.astype(F32)).astype(BF16)
    return hi, mid, lo


def _dn_kernel(q_ref, k_ref, v_ref, z_ref, sl_ref, cw_ref, alog_ref, dtb_ref, ng_ref, s0_ref,
               o_ref, sfin_ref,
               xp_ref, qn_ref, kn_ref, vn_ref, col_ref, u_ref, wq_ref, kd_ref, in_ref, gl_ref, of_ref,
               *, seq):
    c_len = DN_TILE
    n_chunks = seq // c_len
    head = pl.program_id(1)

    zero8 = jnp.zeros((SUBLANES, LANES), F32)
    for j, (src, dst) in enumerate(((q_ref, qn_ref), (k_ref, kn_ref), (v_ref, vn_ref))):
        xp_ref[0:SUBLANES, :] = zero8
        xp_ref[SUBLANES + seq:2 * SUBLANES + seq, :] = zero8
        xp_ref[SUBLANES:SUBLANES + seq, :] = src[...]
        cw = cw_ref[j]
        y = cw[0:1] * xp_ref[pl.ds(SUBLANES - 2, seq), :]
        for t in range(1, CONV_W):
            y = y + cw[t:t + 1] * xp_ref[pl.ds(SUBLANES - 2 + t, seq), :]
        y = y * _sigmoid(y)
        if j < 2:
            y = y * lax.rsqrt(jnp.sum(y * y, axis=-1, keepdims=True) + EPS)
        if j == 0:
            y = y * (DN_DK ** -0.5)
        dst[...] = y

    sl = sl_ref[...]
    lane = lax.broadcasted_iota(jnp.int32, (seq, LANES), 1)
    beta_all = _sigmoid(sl)
    xs = sl + dtb_ref[...]
    softplus = jnp.maximum(xs, 0.0) + jnp.log1p(jnp.exp(-jnp.abs(xs)))
    g_all = -jnp.exp(alog_ref[...]) * softplus
    cols = jnp.zeros((seq, LANES), F32)
    for dr in range(2):
        beta = jnp.sum(jnp.where(lane == dr * DN_HEADS + head, beta_all, 0.0), axis=-1, keepdims=True)
        g = jnp.sum(jnp.where(lane == 2 * DN_HEADS + dr * DN_HEADS + head, g_all, 0.0), axis=-1, keepdims=True)
        cols = jnp.where(lane == dr, beta, cols)
        cols = jnp.where(lane == 2 + dr, g, cols)
    col_ref[...] = cols

    ii = lax.broadcasted_iota(jnp.int32, (c_len, c_len), 0)
    jj = lax.broadcasted_iota(jnp.int32, (c_len, c_len), 1)
    eye = jnp.where(ii == jj, 1.0, 0.0).astype(F32)
    causal = (ii >= jj, ii <= jj)
    strict = (ii > jj, ii < jj)
    tri16 = tuple(jnp.where(m, 1.0, 0.0).astype(BF16) for m in causal)
    level_masks = []
    for dr in range(2):
        hi_idx, lo_idx = (ii, jj) if dr == 0 else (jj, ii)
        masks = []
        lvl = 0
        while (1 << lvl) < c_len:
            masks.append(((hi_idx >> (lvl + 1)) == (lo_idx >> (lvl + 1)))
                         & ((hi_idx >> lvl) == (lo_idx >> lvl) + 1))
            lvl += 1
        level_masks.append(masks)

    group = min(DN_GROUP, n_chunks)
    chains = [(t, dr) for t in range(group) for dr in range(2)]

    def group_body(i, carry):
        cs = [i * group + t for t in range(group)]
        rows = [pl.ds(pl.multiple_of(c * c_len, c_len), c_len) for c in cs]
        q = [qn_ref[r, :] for r in rows]
        k = [kn_ref[r, :] for r in rows]
        v = [vn_ref[r, :] for r in rows]
        blk = [col_ref[r, :] for r in rows]
        q16 = [x.astype(BF16) for x in q]
        k16 = [x.astype(BF16) for x in k]
        qk = [_dot_nt(a, b) for a, b in zip(q16, k16)]
        kk = [_dot_nt(b, b) for b in k16]
        parts = [_split3(x) for x in blk]
        gc_col = [sum(_dot(tri16[dr], p) for p in parts[t])[:, 2 + dr:3 + dr] for t, dr in chains]
        gc_row = [sum(_dot_tn(p, tri16[1 - dr]) for p in parts[t])[2 + dr:3 + dr, :] for t, dr in chains]
        beta = [blk[t][:, dr:dr + 1] for t, dr in chains]
        decay = [jnp.where(causal[dr], jnp.exp(jnp.where(causal[dr], gc_col[n] - gc_row[n], 0.0)), 0.0)
                 for n, (t, dr) in enumerate(chains)]
        lm = [jnp.where(strict[dr], (beta[n] * kk[t]) * decay[n], 0.0) for n, (t, dr) in enumerate(chains)]
        lm16 = [x.astype(BF16) for x in lm]
        zero16 = jnp.zeros((c_len, c_len), BF16)
        xs = [eye - jnp.where(level_masks[dr][0], lm[n], 0.0) for n, (t, dr) in enumerate(chains)]
        for lvl in range(1, len(level_masks[0])):
            ys = [_dot(jnp.where(level_masks[dr][lvl], lm16[n], zero16), xs[n].astype(BF16))
                  for n, (t, dr) in enumerate(chains)]
            xs = [xs[n] - _dot(xs[n].astype(BF16), ys[n].astype(BF16)) for n in range(len(chains))]
        tmat = [x.astype(BF16) for x in xs]
        eg = [jnp.exp(x) for x in gc_col]
        g_last = [gc_col[n][c_len - 1:c_len] if dr == 0 else gc_col[n][0:1] for n, (t, dr) in enumerate(chains)]
        us = [_dot(tmat[n], (v[t] * beta[n]).astype(BF16)) for n, (t, dr) in enumerate(chains)]
        ws = [_dot(tmat[n], (k[t] * (beta[n] * eg[n])).astype(BF16)) for n, (t, dr) in enumerate(chains)]
        for n, (t, dr) in enumerate(chains):
            c = cs[t]
            u_ref[dr, rows[t], :] = us[n]
            wq_ref[dr, pl.ds(pl.multiple_of(2 * c * c_len, c_len), c_len), :] = ws[n].astype(BF16)
            wq_ref[dr, pl.ds(pl.multiple_of(2 * c * c_len + c_len, c_len), c_len), :] = (q[t] * eg[n]).astype(BF16)
            kd_ref[dr, rows[t], :] = (k[t] * jnp.exp(g_last[n] - gc_col[n])).astype(BF16)
            in_ref[dr, rows[t], :] = (qk[t] * decay[n]).astype(BF16)
            gl_ref[dr, pl.ds(c, 1), :] = jnp.broadcast_to(jnp.exp(g_last[n]), (1, LANES))
        return carry

    lax.fori_loop(0, n_chunks // group, group_body, 0)

    def step(i, states):
        cs = [i, n_chunks - 1 - i]
        rows = [pl.ds(pl.multiple_of(c * c_len, c_len), c_len) for c in cs]
        ws_qs = [_dot(wq_ref[dr, pl.ds(pl.multiple_of(2 * cs[dr] * c_len, 2 * c_len), 2 * c_len), :],
                      states[dr].astype(BF16)) for dr in range(2)]
        v16 = [(u_ref[dr, rows[dr], :] - ws_qs[dr][0:c_len]).astype(BF16) for dr in range(2)]
        intra = [_dot(in_ref[dr, rows[dr], :], v16[dr]) for dr in range(2)]
        upd = [_dot_tn(kd_ref[dr, rows[dr], :], v16[dr]) for dr in range(2)]
        for dr in range(2):
            of_ref[dr, rows[dr], :] = ws_qs[dr][c_len:2 * c_len] + intra[dr]
        return tuple(states[dr] * gl_ref[dr, pl.ds(cs[dr], 1), :] + upd[dr] for dr in range(2))

    s_fin = lax.fori_loop(0, n_chunks, step, (s0_ref[0], s0_ref[1]))
    sfin_ref[0] = s_fin[0]
    sfin_ref[1] = s_fin[1]

    o = of_ref[0] + of_ref[1]
    y = o * lax.rsqrt(jnp.mean(o * o, axis=-1, keepdims=True) + EPS) * ng_ref[...]
    z = z_ref[...]
    o_ref[...] = y * (z * _sigmoid(z))


def _dn_call(proj, cw, alog, dtb, ng, s0, s0_map, *, batch, seq):
    nh = DN_HEADS
    n_chunks = seq // DN_TILE
    vec = pl.BlockSpec((1, LANES), lambda b, h: (0, 0))
    nsub = max(n_chunks, SUBLANES)
    return pl.pallas_call(
        functools.partial(_dn_kernel, seq=seq),
        grid=(batch, nh),
        in_specs=[pl.BlockSpec((seq, LANES), lambda b, h: (b, COL_DQ * 4 + h)),
                  pl.BlockSpec((seq, LANES), lambda b, h: (b, COL_DQ * 4 + nh + h)),
                  pl.BlockSpec((seq, LANES), lambda b, h: (b, COL_DQ * 4 + 2 * nh + h)),
                  pl.BlockSpec((seq, LANES), lambda b, h: (b, COL_DZ * 4 + h)),
                  pl.BlockSpec((seq, LANES), lambda b, h: (b, COL_SC128)),
                  pl.BlockSpec((3, CONV_W, LANES), lambda b, h: (0, 0, h)),
                  vec, vec, vec,
                  pl.BlockSpec((None, None, 2, None, DN_DK, DN_DK), s0_map)],
        out_specs=[pl.BlockSpec((seq, LANES), lambda b, h: (b, h)),
                   pl.BlockSpec((None, 2, None, DN_DK, DN_DK), lambda b, h: (b, 0, h, 0, 0))],
        out_shape=[jax.ShapeDtypeStruct((batch * seq, BRANCH_W), F32),
                   jax.ShapeDtypeStruct((batch, 2, nh, DN_DK, DN_DK), F32)],
        scratch_shapes=[pltpu.VMEM((seq + 2 * SUBLANES, LANES), F32),
                        pltpu.VMEM((seq, LANES), F32),
                        pltpu.VMEM((seq, LANES), F32),
                        pltpu.VMEM((seq, LANES), F32),
                        pltpu.VMEM((seq, LANES), F32),
                        pltpu.VMEM((2, seq, LANES), F32),
                        pltpu.VMEM((2, 2 * seq, LANES), BF16),
                        pltpu.VMEM((2, seq, LANES), BF16),
                        pltpu.VMEM((2, seq, DN_TILE), BF16),
                        pltpu.VMEM((2, nsub, LANES), F32),
                        pltpu.VMEM((2, seq, LANES), F32)],
        compiler_params=_params(2),
        name="deltanet",
    )(proj, proj, proj, proj, proj, cw, alog, dtb, ng, s0)


def _layer_params(l, w_ffn_gate, w_ffn_up, w_ffn_down, w_in, lru_conv_w, lru_conv_b, lru_w_r, lru_b_r,
                  lru_w_i, lru_b_i, lru_lambda, gqa_q_norm, gqa_k_norm, na_rpb, dn_conv_w, dn_a_log,
                  dn_dt_bias, dn_norm_g, w_branch, w_out, norm_g):
    offs = np.cumsum((0,) + IN_WIDTHS)
    w = w_in[l]
    seg = [w[:, offs[i]:offs[i + 1]] for i in range(len(IN_WIDTHS))]
    (a_x, a_y, b_q, b_k, b_v, c_q, c_k, c_v, d_q, d_k, d_v, d_z, d_b, d_a, g_lin) = seg
    perm = np.asarray(GQA_PERM)
    b_q = b_q.reshape(D_MODEL, GQA_HEADS, HEAD_DIM)[:, perm].reshape(D_MODEL, BRANCH_W)
    pad = jnp.zeros((D_MODEL, PROJ_W - 5120 - 2 * LANES - 16), F32)
    w_main = jnp.concatenate([a_x, a_y, b_q, c_q, c_k, c_v, d_q, d_k, d_v, d_z, b_k, b_v, d_b, d_a, pad],
                             axis=1).astype(BF16)

    def block_diag(wb):
        wb = wb.reshape(2, LRU_BLOCKS // 2, 2, LRU_BW, LRU_BW)
        z = jnp.zeros_like(wb[:, :, 0])
        return jnp.concatenate([jnp.concatenate([wb[:, :, 0], z], axis=-1),
                                jnp.concatenate([z, wb[:, :, 1]], axis=-1)], axis=-2)

    wr, wi = block_diag(lru_w_r[l]), block_diag(lru_w_i[l])
    lru_wg = jnp.concatenate([wr[0], wi[0], wr[1], wi[1]], axis=-1).astype(BF16)
    ncol = BRANCH_W // LANES

    def col_blocks(v):
        return v.reshape(ncol, LANES)

    lru_bg = jnp.concatenate([col_blocks(lru_b_r[l, 0]), col_blocks(lru_b_i[l, 0]),
                              col_blocks(lru_b_r[l, 1]), col_blocks(lru_b_i[l, 1])], axis=-1)[:, None, :]

    wb = w_branch[l]
    wb_b = wb[1].reshape(GQA_HEADS, HEAD_DIM, D_MODEL)[perm].reshape(BRANCH_W, D_MODEL)
    wb = jnp.stack([wb[0], wb_b, wb[2], wb[3]], axis=0).astype(BF16)

    lane_pad = jnp.zeros((LANES - 4 * DN_HEADS,), F32)
    alog = jnp.concatenate([jnp.zeros((2 * DN_HEADS,), F32), dn_a_log[l].reshape(-1), lane_pad])[None, :]
    dtb = jnp.concatenate([jnp.zeros((2 * DN_HEADS,), F32), dn_dt_bias[l].reshape(-1), lane_pad])[None, :]

    return dict(
        norm_g=norm_g[l][:, None, :],
        wg=w_ffn_gate[l].astype(BF16), wu=w_ffn_up[l].astype(BF16), wd=w_ffn_down[l].astype(BF16),
        w_main=w_main, w_gate=g_lin.astype(BF16),
        lru_cw=lru_conv_w[l], lru_cb=lru_conv_b[l][None, :], lru_wg=lru_wg, lru_bg=lru_bg,
        lru_lam=lru_lambda[l],
        gq=jnp.tile(gqa_q_norm[l], 2)[None, :], gk=jnp.tile(gqa_k_norm[l], 2)[None, :],
        rpb=na_rpb[l],
        dn_cw=dn_conv_w[l].reshape(CONV_W, 3, BRANCH_W).transpose(1, 0, 2),
        dn_alog=alog, dn_dtb=dtb, dn_ng=dn_norm_g[l][None, :],
        wb=wb, w_out=w_out[l].astype(BF16),
    )


def _rope_tables(seq):
    pos = jnp.arange(seq)
    half = HEAD_DIM // 2
    inv = jnp.power(ROPE_BASE, -jnp.arange(0, half, 2, dtype=F32) / half)
    ang_r = (pos // GRID_W).astype(F32)[:, None] * inv[None, :]
    ang_c = (pos % GRID_W).astype(F32)[:, None] * inv[None, :]
    cos = jnp.concatenate([jnp.cos(ang_r)] * 2 + [jnp.cos(ang_c)] * 2, axis=-1)
    sin = jnp.concatenate([-jnp.sin(ang_r), jnp.sin(ang_r), -jnp.sin(ang_c), jnp.sin(ang_c)], axis=-1)
    return jnp.tile(cos, (1, 2)), jnp.tile(sin, (1, 2))


def _na_bias_table(rpb):
    qc = np.arange(GRID_W)
    cs = np.clip(qc - NA_WIN_C // 2, 0, GRID_W - NA_WIN_C)
    kc = np.arange(GRID_W)
    inwin = (kc[None, :] >= cs[:, None]) & (kc[None, :] < cs[:, None] + NA_WIN_C)
    coff = np.clip(kc[None, :] - qc[:, None] + NA_WIN_C - 1, 0, 2 * NA_WIN_C - 2)
    roff = np.arange(NA_WIN_R)[:, None] + np.arange(NA_WIN_R)[None, :]
    t = rpb.astype(F32)[:, roff][:, :, :, coff]
    t = jnp.where(inwin[None, None, None], t, NEG_BIG)
    return jnp.transpose(t, (1, 0, 3, 2, 4)).reshape(NA_WIN_R, NA_HEADS, GRID_W, NA_WIN_R * GRID_W)


def _layer(x, mod, p, *, batch, seq, latent, layer, caches, tables, final_g, tm):
    g = p["norm_g"]
    x = _ffn_call(x, mod, g[0], p["wg"][0], p["wu"][0], p["wd"][0], final_g, mi=0, final=False, tm=tm)
    proj = _inproj_call(x, mod, g[1], p["w_main"], tm=tm)

    if latent:
        cache_ak, cache_av, cache_nk, cache_nv, state_lru, state_delta = caches
        h0 = state_lru[:, layer]
        s0 = state_delta
        s0_map = lambda b, h: (b, layer, 0, h, 0, 0)
    else:
        h0 = jnp.zeros((batch, 2, BRANCH_W), F32)
        s0 = jnp.zeros((1, 1, 2, 1, DN_DK, DN_DK), F32)
        s0_map = lambda b, h: (0, 0, 0, 0, 0, 0)

    o_a, lru_fin = _lru_call(proj, h0, p["lru_cw"], p["lru_cb"], p["lru_wg"], p["lru_bg"], p["lru_lam"],
                             batch=batch, seq=seq)

    cos, sin = tables["rope"] if latent else (None, None)
    qn, kn = _prep_call(proj, p["gq"], p["gk"], cos, sin, batch=batch, seq=seq, rope=latent)
    tq = min(seq, 256)
    kv_new = (kn, pl.BlockSpec((seq, LANES), lambda b, i: (b, 0)),
              proj, pl.BlockSpec((seq, LANES), lambda b, i: (b, COL_BV128)))
    if latent:
        gqa_cache = pl.BlockSpec((None, None, PAST_LEN, LANES), lambda b, i: (b, layer, 0, 0))
        sources = [kv_new, (cache_ak, gqa_cache, cache_av, gqa_cache)]
    else:
        sources = [kv_new]
    o_b = _attn_call(qn, 0, sources, batch=batch, seq=seq, tq=tq, nkb=1, qscale=1.0)

    if latent:
        o_c = _na_call(proj, cache_nk, cache_nv, _na_bias_table(p["rpb"]), layer, batch=batch, seq=seq)
    else:
        src = (proj, pl.BlockSpec((seq, BRANCH_W), lambda b, i: (b, COL_CK)),
               proj, pl.BlockSpec((seq, BRANCH_W), lambda b, i: (b, COL_CV)))
        o_c = _attn_call(proj, COL_CQ, [src], batch=batch, seq=seq, tq=tq, nkb=4, qscale=HEAD_DIM ** -0.5)

    o_d, dn_fin = _dn_call(proj, p["dn_cw"], p["dn_alog"], p["dn_dtb"], p["dn_ng"], s0, s0_map,
                           batch=batch, seq=seq)

    x = _merge_call(x, mod, g[1], (o_a, o_b, o_c, o_d), p["w_gate"], p["wb"], p["w_out"], tm=tm)
    x = _ffn_call(x, mod, g[2], p["wg"][1], p["wu"][1], p["wd"][1], final_g, mi=6,
                  final=(layer == DEPTH - 1), tm=tm)

    new_ctx = None
    if not latent:
        new_ctx = (kn.reshape(batch, seq, GQA_KV, HEAD_DIM),
                   proj[:, COL_BV128 * LANES:(COL_BV128 + 1) * LANES].reshape(batch, seq, GQA_KV, HEAD_DIM),
                   proj[:, COL_CK * BRANCH_W:(COL_CK + 1) * BRANCH_W].reshape(batch, seq, NA_HEADS, HEAD_DIM),
                   proj[:, COL_CV * BRANCH_W:(COL_CV + 1) * BRANCH_W].reshape(batch, seq, NA_HEADS, HEAD_DIM),
                   lru_fin, dn_fin)
    return x, new_ctx


def kernel(x_prompt, x_sample, c, cache_attn_k, cache_attn_v, cache_na_k, cache_na_v, state_lru, state_delta, c_ctx, w_mod, b_mod, norm_g, w_ffn_gate, w_ffn_up, w_ffn_down, w_in, lru_conv_w, lru_conv_b, lru_w_r, lru_b_r, lru_w_i, lru_b_i, lru_lambda, gqa_q_norm, gqa_k_norm, na_rpb, dn_conv_w, dn_a_log, dn_dt_bias, dn_norm_g, w_branch, w_out, final_norm_g):
    batch_c, seq_c, _ = x_prompt.shape
    batch_l, seq_l, _ = x_sample.shape
    assert batch_l + 1 <= SUBLANES

    cs = jnp.concatenate([c_ctx[None, :], c, jnp.zeros((SUBLANES - 1 - batch_l, D_MODEL), F32)], axis=0)
    mod_all = _mod_call(cs, w_mod, b_mod)

    caches = (cache_attn_k.reshape(batch_l, DEPTH, PAST_LEN, GQA_KV * HEAD_DIM),
              cache_attn_v.reshape(batch_l, DEPTH, PAST_LEN, GQA_KV * HEAD_DIM),
              cache_na_k.reshape(batch_l, DEPTH, PAST_LEN, BRANCH_W),
              cache_na_v.reshape(batch_l, DEPTH, PAST_LEN, BRANCH_W),
              state_lru, state_delta)
    tables = {"rope": _rope_tables(seq_l)}
    final_g = final_norm_g[None, :]

    xc = x_prompt.reshape(1, batch_c * seq_c, D_MODEL)
    xl = x_sample
    ctx_out = []
    for l in range(DEPTH):
        p = _layer_params(l, w_ffn_gate, w_ffn_up, w_ffn_down, w_in, lru_conv_w, lru_conv_b, lru_w_r,
                          lru_b_r, lru_w_i, lru_b_i, lru_lambda, gqa_q_norm, gqa_k_norm, na_rpb, dn_conv_w,
                          dn_a_log, dn_dt_bias, dn_norm_g, w_branch, w_out, norm_g)
        mod_c = mod_all[l, 0:1].reshape(1, N_MOD, D_MODEL)
        mod_l = mod_all[l, 1:1 + batch_l].reshape(batch_l, N_MOD, D_MODEL)
        xc, new_ctx = _layer(xc, mod_c, p, batch=batch_c, seq=seq_c, latent=False, layer=l, caches=None,
                             tables=tables, final_g=final_g, tm=256)
        ctx_out.append(new_ctx)
        xl, _ = _layer(xl, mod_l, p, batch=batch_l, seq=seq_l, latent=True, layer=l, caches=caches,
                       tables=tables, final_g=final_g, tm=256)

    stacked = [jnp.stack([ctx_out[l][i] for l in range(DEPTH)], axis=1) for i in range(6)]
    return (xc.reshape(batch_c, seq_c, D_MODEL), xl, *stacked)
```

```python
import functools

import numpy as np
import jax
import jax.numpy as jnp
from jax import lax
from jax.experimental import pallas as pl
from jax.experimental.pallas import tpu as pltpu

F32 = jnp.float32
BF16 = jnp.bfloat16

D_MODEL = 1024
DEPTH = 4
GRID_W = 64
N_BRANCH = 4
BRANCH_W = 512
N_MOD = 9
D_FF = 2816
EPS = 1e-6
CONV_W = 4
LRU_BLOCKS = 8
LRU_BW = 64
LRU_C = 8.0
HEAD_DIM = 64
GQA_HEADS = 8
GQA_KV = 2
ROPE_BASE = 10000.0
NA_HEADS = 8
NA_WIN_R = 8
NA_WIN_C = 16
DN_DK = 128
DN_HEADS = 4
LRU_PITCH_PAD = 8
DN_TILE = 128
DN_GROUP = 4
PAST_LEN = 512
IN_WIDTHS = (512, 512, 512, 128, 128, 512, 512, 512, 512, 512, 512, 512, 8, 8, 4096)

LANES = 128
SUBLANES = 8
PROJ_W = 5632
COL_BQ, COL_CQ, COL_CK, COL_CV, COL_DQ, COL_DZ = 2, 3, 4, 5, 6, 9
COL_BK128, COL_BV128, COL_SC128 = 40, 41, 42
VMEM_LIMIT = 56 * 1024 * 1024
NEG_BIG = -1e30
GQA_PERM = (0, 4, 1, 5, 2, 6, 3, 7)


def _params(n):
    return pltpu.CompilerParams(dimension_semantics=("arbitrary",) * n, vmem_limit_bytes=VMEM_LIMIT)


def _const_spec(shape):
    nd = len(shape)
    return pl.BlockSpec(shape, lambda *_: (0,) * nd, pipeline_mode=pl.Buffered(1))


def _dot(a, b):
    return jnp.dot(a, b, preferred_element_type=F32)


def _dot_nt(a, b):
    return lax.dot_general(a, b, (((1,), (1,)), ((), ())), preferred_element_type=F32)


def _dot_tn(a, b):
    return lax.dot_general(a, b, (((0,), (0,)), ((), ())), preferred_element_type=F32)


def _split(x):
    hi = x.astype(BF16)
    lo = (x - hi.astype(F32)).astype(BF16)
    return hi, lo


def _dot3(a, b):
    ah, al = _split(a)
    bh, bl = _split(b)
    return _dot(ah, bh) + (_dot(al, bh) + _dot(ah, bl))


def _sigmoid(x):
    return jax.nn.sigmoid(x)


def _modnorm(x, g, shift, scale):
    ms = jnp.mean(x * x, axis=-1, keepdims=True)
    return (x * lax.rsqrt(ms + EPS) * g) * (1.0 + scale) + shift


def _mod_kernel(c_ref, w_ref, b_ref, o_ref):
    c = c_ref[...]
    o_ref[0] = _dot3(c * _sigmoid(c), w_ref[0]) + b_ref[0]


def _mod_call(cs, w_mod, b_mod):
    tn = 1024
    n = N_MOD * D_MODEL
    return pl.pallas_call(
        _mod_kernel,
        grid=(DEPTH, n // tn),
        in_specs=[pl.BlockSpec((SUBLANES, D_MODEL), lambda l, j: (0, 0)),
                  pl.BlockSpec((1, D_MODEL, tn), lambda l, j: (l, 0, j)),
                  pl.BlockSpec((1, 1, tn), lambda l, j: (l, 0, j))],
        out_specs=pl.BlockSpec((1, SUBLANES, tn), lambda l, j: (l, 0, j)),
        out_shape=jax.ShapeDtypeStruct((DEPTH, SUBLANES, n), F32),
        compiler_params=_params(2),
        name="mod",
    )(cs, w_mod, b_mod.reshape(DEPTH, 1, n))


def _ffn_kernel(x_ref, mod_ref, g_ref, wg_ref, wu_ref, wd_ref, gf_ref, o_ref, *, mi, final):
    x = x_ref[0]
    mod = mod_ref[0]
    h = _modnorm(x, g_ref[...], mod[mi:mi + 1], mod[mi + 1:mi + 2]).astype(BF16)
    gt = _dot(h, wg_ref[...])
    up = _dot(h, wu_ref[...])
    a = (gt * _sigmoid(gt) * up).astype(BF16)
    y = x + 0.5 * mod[mi + 2:mi + 3] * _dot(a, wd_ref[...])
    if final:
        ms = jnp.mean(y * y, axis=-1, keepdims=True)
        y = y * lax.rsqrt(ms + EPS) * gf_ref[...]
    o_ref[0] = y


def _ffn_call(x, mod, g, wg, wu, wd, gf, *, mi, final, tm):
    nb, rows, _ = x.shape
    return pl.pallas_call(
        functools.partial(_ffn_kernel, mi=mi, final=final),
        grid=(nb, rows // tm),
        in_specs=[pl.BlockSpec((1, tm, D_MODEL), lambda b, i: (b, i, 0)),
                  pl.BlockSpec((1, N_MOD, D_MODEL), lambda b, i: (b, 0, 0)),
                  _const_spec((1, D_MODEL)),
                  _const_spec((D_MODEL, D_FF)),
                  _const_spec((D_MODEL, D_FF)),
                  _const_spec((D_FF, D_MODEL)),
                  _const_spec((1, D_MODEL))],
        out_specs=pl.BlockSpec((1, tm, D_MODEL), lambda b, i: (b, i, 0)),
        out_shape=jax.ShapeDtypeStruct(x.shape, F32),
        compiler_params=_params(2),
        name="ffn",
    )(x, mod, g, wg, wu, wd, gf)


def _inproj_kernel(x_ref, mod_ref, g_ref, w_ref, o_ref):
    mod = mod_ref[0]
    h = _modnorm(x_ref[0], g_ref[...], mod[3:4], mod[4:5]).astype(BF16)
    o_ref[...] = _dot(h, w_ref[...])


def _inproj_call(x, mod, g, w, *, tm):
    nb, rows, _ = x.shape
    nt = rows // tm
    return pl.pallas_call(
        _inproj_kernel,
        grid=(nb, nt),
        in_specs=[pl.BlockSpec((1, tm, D_MODEL), lambda b, i: (b, i, 0)),
                  pl.BlockSpec((1, N_MOD, D_MODEL), lambda b, i: (b, 0, 0)),
                  _const_spec((1, D_MODEL)),
                  _const_spec((D_MODEL, PROJ_W))],
        out_specs=pl.BlockSpec((tm, PROJ_W), lambda b, i: (b * nt + i, 0)),
        out_shape=jax.ShapeDtypeStruct((nb * rows, PROJ_W), F32),
        compiler_params=_params(2),
        name="inproj",
    )(x, mod, g, w)


def _merge_kernel(x_ref, mod_ref, g_ref, oa_ref, ob_ref, oc_ref, od_ref, wgate_ref, wb_ref, wout_ref, o_ref):
    x = x_ref[0]
    mod = mod_ref[0]
    h = _modnorm(x, g_ref[...], mod[3:4], mod[4:5]).astype(BF16)
    acc = None
    for n, ref in enumerate((oa_ref, ob_ref, oc_ref, od_ref)):
        gate = _sigmoid(_dot(h, wgate_ref[:, n * D_MODEL:(n + 1) * D_MODEL]))
        term = gate * _dot(ref[...].astype(BF16), wb_ref[n])
        acc = term if acc is None else acc + term
    o_ref[0] = x + mod[5:6] * _dot(acc.astype(BF16), wout_ref[...])


def _merge_call(x, mod, g, outs, wgate, wb, wout, *, tm):
    nb, rows, _ = x.shape
    nt = rows // tm
    ospec = pl.BlockSpec((tm, BRANCH_W), lambda b, i: (b * nt + i, 0))
    return pl.pallas_call(
        _merge_kernel,
        grid=(nb, nt),
        in_specs=[pl.BlockSpec((1, tm, D_MODEL), lambda b, i: (b, i, 0)),
                  pl.BlockSpec((1, N_MOD, D_MODEL), lambda b, i: (b, 0, 0)),
                  _const_spec((1, D_MODEL)),
                  ospec, ospec, ospec, ospec,
                  _const_spec((D_MODEL, N_BRANCH * D_MODEL)),
                  _const_spec((N_BRANCH, BRANCH_W, D_MODEL)),
                  _const_spec((D_MODEL, D_MODEL))],
        out_specs=pl.BlockSpec((1, tm, D_MODEL), lambda b, i: (b, i, 0)),
        out_shape=jax.ShapeDtypeStruct(x.shape, F32),
        compiler_params=_params(2),
        name="merge",
    )(x, mod, g, *outs, wgate, wb, wout)


def _log_sigmoid(x):
    return jnp.minimum(x, 0.0) - jnp.log1p(jnp.exp(-jnp.abs(x)))


def _gelu_tanh(x):
    return x * (0.5 * (1.0 + jnp.tanh(0.7978845608028654 * (x + 0.044715 * (x * x * x)))))


def _lru_kernel(ax_ref, ay_ref, h0_ref, cw_ref, cb_ref, wg_ref, bg_ref, lam_ref, o_ref, fin_ref,
                xp_ref, a_ref, u_ref, *, seq):
    lc = seq // SUBLANES
    pitch = lc + LRU_PITCH_PAD
    zero8 = jnp.zeros((SUBLANES, LANES), F32)
    xp_ref[0:SUBLANES, :] = zero8
    xp_ref[SUBLANES + seq:2 * SUBLANES + seq, :] = zero8
    xp_ref[SUBLANES:SUBLANES + seq, :] = ax_ref[...]
    cw = cw_ref[...]
    xa = cb_ref[...] + cw[0:1] * xp_ref[pl.ds(SUBLANES - 2, seq), :]
    for k in range(1, CONV_W):
        xa = xa + cw[k:k + 1] * xp_ref[pl.ds(SUBLANES - 2 + k, seq), :]
    gates = _dot(xa.astype(BF16), wg_ref[0]) + bg_ref[0]
    lam = lam_ref[...]
    for dr in range(2):
        r = _sigmoid(gates[:, (2 * dr) * LANES:(2 * dr + 1) * LANES])
        i = _sigmoid(gates[:, (2 * dr + 1) * LANES:(2 * dr + 2) * LANES])
        log_a = (LRU_C * r) * _log_sigmoid(lam[dr:dr + 1])
        a = jnp.exp(log_a)
        u = jnp.sqrt(-jnp.tanh(log_a) * (a * a + 1.0)) * (i * xa)
        for k in range(SUBLANES):
            a_ref[dr, pl.ds(k * pitch, lc), :] = a[k * lc:(k + 1) * lc]
            u_ref[dr, pl.ds(k * pitch, lc), :] = u[k * lc:(k + 1) * lc]

    sub = lax.broadcasted_iota(jnp.int32, (SUBLANES, LANES), 0)
    h0 = h0_ref[0]
    hf0 = jnp.where(sub == 0, h0[0:1], 0.0)
    hb0 = jnp.where(sub == SUBLANES - 1, h0[1:2], 0.0)
    ones = jnp.ones((SUBLANES, LANES), F32)

    def body(s, carry):
        hf, pf, hb, pb = carry
        rows_f = pl.ds(s, SUBLANES, stride=pitch)
        rows_b = pl.ds(lc - 1 - s, SUBLANES, stride=pitch)
        af = a_ref[0, rows_f, :]
        hf = af * hf + u_ref[0, rows_f, :]
        pf = af * pf
        u_ref[0, rows_f, :] = hf
        a_ref[0, rows_f, :] = pf
        ab = a_ref[1, rows_b, :]
        hb = ab * hb + u_ref[1, rows_b, :]
        pb = ab * pb
        u_ref[1, rows_b, :] = hb
        a_ref[1, rows_b, :] = pb
        return hf, pf, hb, pb

    hf, pf, hb, pb = lax.fori_loop(0, lc, body, (hf0, ones, hb0, ones))

    cf = [jnp.zeros((1, LANES), F32)]
    for k in range(1, SUBLANES):
        cf.append(hf[k - 1:k] + pf[k - 1:k] * cf[k - 1])
    fin_ref[0, 0:1, :] = hf[SUBLANES - 1:SUBLANES] + pf[SUBLANES - 1:SUBLANES] * cf[SUBLANES - 1]
    cb = [None] * SUBLANES
    cb[SUBLANES - 1] = jnp.zeros((1, LANES), F32)
    for k in range(SUBLANES - 2, -1, -1):
        cb[k] = hb[k + 1:k + 2] + pb[k + 1:k + 2] * cb[k + 1]
    fin_ref[0, 1:2, :] = hb[0:1] + pb[0:1] * cb[0]

    for k in range(SUBLANES):
        rows = pl.ds(k * lc, lc)
        held = pl.ds(k * pitch, lc)
        h = (u_ref[0, held, :] + a_ref[0, held, :] * cf[k]) + (u_ref[1, held, :] + a_ref[1, held, :] * cb[k])
        o_ref[rows, :] = h * _gelu_tanh(ay_ref[rows, :])


def _lru_call(proj, h0, cw, cb, wg, bg, lam, *, batch, seq):
    ncol = BRANCH_W // LANES
    return pl.pallas_call(
        functools.partial(_lru_kernel, seq=seq),
        grid=(batch, ncol),
        in_specs=[pl.BlockSpec((seq, LANES), lambda b, c: (b, c)),
                  pl.BlockSpec((seq, LANES), lambda b, c: (b, ncol + c)),
                  pl.BlockSpec((1, 2, LANES), lambda b, c: (b, 0, c)),
                  pl.BlockSpec((CONV_W, LANES), lambda b, c: (0, c)),
                  pl.BlockSpec((1, LANES), lambda b, c: (0, c)),
                  pl.BlockSpec((1, LANES, 4 * LANES), lambda b, c: (c, 0, 0)),
                  pl.BlockSpec((1, 1, 4 * LANES), lambda b, c: (c, 0, 0)),
                  pl.BlockSpec((2, LANES), lambda b, c: (0, c))],
        out_specs=[pl.BlockSpec((seq, LANES), lambda b, c: (b, c)),
                   pl.BlockSpec((1, 2, LANES), lambda b, c: (b, 0, c))],
        out_shape=[jax.ShapeDtypeStruct((batch * seq, BRANCH_W), F32),
                   jax.ShapeDtypeStruct((batch, 2, BRANCH_W), F32)],
        scratch_shapes=[pltpu.VMEM((seq + 2 * SUBLANES, LANES), F32),
                        pltpu.VMEM((2, seq + SUBLANES * LRU_PITCH_PAD, LANES), F32),
                        pltpu.VMEM((2, seq + SUBLANES * LRU_PITCH_PAD, LANES), F32)],
        compiler_params=_params(2),
        name="lru",
    )(proj, proj, h0, cw, cb, wg, bg, lam)


def _prep_kernel(*refs, rope):
    if rope:
        q_ref, kv_ref, gq_ref, gk_ref, cos_ref, sin_ref, qn_ref, kn_ref = refs
    else:
        q_ref, kv_ref, gq_ref, gk_ref, qn_ref, kn_ref = refs
    rows = q_ref.shape[0]
    lane = lax.broadcasted_iota(jnp.int32, (rows, LANES), 1)
    lo = lane < HEAD_DIM
    first16 = (lane & 16) == 0

    def head_norm(x, g):
        sq = x * x
        s_lo = jnp.sum(jnp.where(lo, sq, 0.0), axis=-1, keepdims=True)
        s_hi = jnp.sum(jnp.where(lo, 0.0, sq), axis=-1, keepdims=True)
        ms = jnp.where(lo, s_lo, s_hi) * (1.0 / HEAD_DIM)
        return x * lax.rsqrt(ms + EPS) * g

    def rotate(y):
        if not rope:
            return y
        partner = jnp.where(first16, pltpu.roll(y, LANES - 16, 1), pltpu.roll(y, 16, 1))
        return y * cos_ref[...] + partner * sin_ref[...]

    for p in range(BRANCH_W // LANES):
        cols = slice(p * LANES, (p + 1) * LANES)
        qn_ref[:, cols] = rotate(head_norm(q_ref[:, cols], gq_ref[...])) * (HEAD_DIM ** -0.5)
    kn_ref[...] = rotate(head_norm(kv_ref[:, 0:LANES], gk_ref[...]))


def _prep_call(proj, gq, gk, cos, sin, *, batch, seq, rope):
    tab = pl.BlockSpec((seq, LANES), lambda b: (0, 0))
    vec = pl.BlockSpec((1, LANES), lambda b: (0, 0))
    in_specs = [pl.BlockSpec((seq, BRANCH_W), lambda b: (b, COL_BQ)),
                pl.BlockSpec((seq, BRANCH_W), lambda b: (b, COL_BK128 // 4)),
                vec, vec]
    args = [proj, proj, gq, gk]
    if rope:
        in_specs += [tab, tab]
        args += [cos, sin]
    return pl.pallas_call(
        functools.partial(_prep_kernel, rope=rope),
        grid=(batch,),
        in_specs=in_specs,
        out_specs=[pl.BlockSpec((seq, BRANCH_W), lambda b: (b, 0)),
                   pl.BlockSpec((seq, LANES), lambda b: (b, 0))],
        out_shape=[jax.ShapeDtypeStruct((batch * seq, BRANCH_W), F32),
                   jax.ShapeDtypeStruct((batch * seq, LANES), F32)],
        compiler_params=_params(1),
        name="attn_prep",
    )(*args)


def _softmax_pv(scores, values):
    m = None
    for s in scores:
        sm = jnp.max(s, axis=-1, keepdims=True)
        m = sm if m is None else jnp.maximum(m, sm)
    den = None
    out = None
    for s, v in zip(scores, values):
        p = jnp.exp(s - m)
        ps = jnp.sum(p, axis=-1, keepdims=True)
        den = ps if den is None else den + ps
        o = _dot(p.astype(BF16), v)
        out = o if out is None else out + o
    return out / den


def _attn_kernel(*refs, nsrc, nkb, qscale):
    q_ref = refs[0]
    src = refs[1:1 + 2 * nsrc]
    o_ref = refs[1 + 2 * nsrc]
    tq = q_ref.shape[0]
    lo = lax.broadcasted_iota(jnp.int32, (tq, LANES), 1) < HEAD_DIM

    loaded = {}

    def kv(p):
        col = p if nkb != 1 else 0
        if col not in loaded:
            cols = slice(col * LANES, (col + 1) * LANES)
            loaded[col] = ([src[2 * i][:, cols].astype(BF16) for i in range(nsrc)],
                           [src[2 * i + 1][:, cols].astype(BF16) for i in range(nsrc)])
        return loaded[col]

    def scores(unit):
        p, half = unit
        qb = q_ref[:, p * LANES:(p + 1) * LANES]
        if qscale != 1.0:
            qb = qb * qscale
        qm = jnp.where(lo if half == 0 else jnp.logical_not(lo), qb, 0.0).astype(BF16)
        return [_dot_nt(qm, k) for k in kv(p)[0]]

    units = [(p, half) for p in range(BRANCH_W // LANES) for half in range(2)]
    pending = scores(units[0])
    first_half = None
    for n, (p, half) in enumerate(units):
        current = pending
        if n + 1 < len(units):
            pending = scores(units[n + 1])
        out = _softmax_pv(current, kv(p)[1])
        if half == 0:
            first_half = out
        else:
            o_ref[:, p * LANES:(p + 1) * LANES] = jnp.where(lo, first_half, out)


def _attn_call(q_arr, q_col, sources, *, batch, seq, tq, nkb, qscale):
    nq = seq // tq
    in_specs = [pl.BlockSpec((tq, BRANCH_W), lambda b, i: (b * nq + i, q_col))]
    args = [q_arr]
    for k_arr, k_spec, v_arr, v_spec in sources:
        in_specs += [k_spec, v_spec]
        args += [k_arr, v_arr]
    return pl.pallas_call(
        functools.partial(_attn_kernel, nsrc=len(sources), nkb=nkb, qscale=qscale),
        grid=(batch, nq),
        in_specs=in_specs,
        out_specs=pl.BlockSpec((tq, BRANCH_W), lambda b, i: (b * nq + i, 0)),
        out_shape=jax.ShapeDtypeStruct((batch * seq, BRANCH_W), F32),
        compiler_params=_params(2),
        name="attn",
    )(*args)


def _na_row_start(r, rows):
    return jnp.clip(r - NA_WIN_R // 2, 0, rows - NA_WIN_R)


def _na_kernel(q_ref, k_ref, v_ref, ck_ref, cv_ref, bias_ref, o_ref, *, rows):
    r = pl.program_id(1)
    rs = _na_row_start(r, rows)
    win = pl.ds(pl.multiple_of(rs * GRID_W, GRID_W), NA_WIN_R * GRID_W)
    first_off = rs - r + NA_WIN_R - 1
    lo = lax.broadcasted_iota(jnp.int32, (GRID_W, LANES), 1) < HEAD_DIM
    ncol = BRANCH_W // LANES
    scores = []
    for p in range(ncol):
        cols = slice(p * LANES, (p + 1) * LANES)
        qb = q_ref[:, cols] * (HEAD_DIM ** -0.5)
        kw = k_ref[win, cols].astype(BF16)
        kc = ck_ref[:, cols].astype(BF16)
        for half in range(2):
            qm = jnp.where(lo if half == 0 else jnp.logical_not(lo), qb, 0.0).astype(BF16)
            scores.append((_dot_nt(qm, kw), _dot_nt(qm, kc)))
    probs = []
    for h, (s_loc, s_ctx) in enumerate(scores):
        bias = jnp.concatenate([bias_ref[h, first_off + 2 * t] for t in range(NA_WIN_R // 2)], axis=-1)
        s_loc = s_loc + bias
        m = jnp.maximum(jnp.max(s_loc, axis=-1, keepdims=True), jnp.max(s_ctx, axis=-1, keepdims=True))
        p_loc = jnp.exp(s_loc - m)
        p_ctx = jnp.exp(s_ctx - m)
        den = jnp.sum(p_loc, axis=-1, keepdims=True) + jnp.sum(p_ctx, axis=-1, keepdims=True)
        probs.append((p_loc.astype(BF16), p_ctx.astype(BF16), den))
    for p in range(ncol):
        cols = slice(p * LANES, (p + 1) * LANES)
        vw = v_ref[win, cols].astype(BF16)
        vc = cv_ref[:, cols].astype(BF16)
        halves = []
        for half in range(2):
            p_loc, p_ctx, den = probs[2 * p + half]
            halves.append((_dot(p_loc, vw) + _dot(p_ctx, vc)) / den)
        o_ref[:, cols] = jnp.where(lo, halves[0], halves[1])


def _na_call(proj, cache_k, cache_v, bias, layer, *, batch, seq):
    rows = seq // GRID_W
    cache_spec = pl.BlockSpec((None, None, PAST_LEN, BRANCH_W), lambda b, r: (b, layer, 0, 0))
    return pl.pallas_call(
        functools.partial(_na_kernel, rows=rows),
        grid=(batch, rows),
        in_specs=[pl.BlockSpec((GRID_W, BRANCH_W), lambda b, r: (b * rows + r, COL_CQ)),
                  pl.BlockSpec((seq, BRANCH_W), lambda b, r: (b, COL_CK)),
                  pl.BlockSpec((seq, BRANCH_W), lambda b, r: (b, COL_CV)),
                  cache_spec, cache_spec,
                  _const_spec(bias.shape)],
        out_specs=pl.BlockSpec((GRID_W, BRANCH_W), lambda b, r: (b * rows + r, 0)),
        out_shape=jax.ShapeDtypeStruct((batch * seq, BRANCH_W), F32),
        compiler_params=_params(2),
        name="na",
    )(proj, proj, proj, cache_k, cache_v, bias)


def _split3(x):
    hi = x.astype(BF16)
    r = x - hi.astype(F32)
    mid = r.astype(BF16)
    lo = (r - mid.astype(F32)).astype(BF16)
    return hi, mid, lo


def _dn_kernel(q_ref, k_ref, v_ref, z_ref, sl_ref, cw_ref, alog_ref, dtb_ref, ng_ref, s0_ref,
               o_ref, sfin_ref,
               xp_ref, qn_ref, kn_ref, vn_ref, col_ref, u_ref, wq_ref, kd_ref, in_ref, gl_ref, of_ref,
               *, seq):
    c_len = DN_TILE
    n_chunks = seq // c_len
    head = pl.program_id(1)

    zero8 = jnp.zeros((SUBLANES, LANES), F32)
    for j, (src, dst) in enumerate(((q_ref, qn_ref), (k_ref, kn_ref), (v_ref, vn_ref))):
        xp_ref[0:SUBLANES, :] = zero8
        xp_ref[SUBLANES + seq:2 * SUBLANES + seq, :] = zero8
        xp_ref[SUBLANES:SUBLANES + seq, :] = src[...]
        cw = cw_ref[j]
        y = cw[0:1] * xp_ref[pl.ds(SUBLANES - 2, seq), :]
        for t in range(1, CONV_W):
            y = y + cw[t:t + 1] * xp_ref[pl.ds(SUBLANES - 2 + t, seq), :]
        y = y * _sigmoid(y)
        if j < 2:
            y = y * lax.rsqrt(jnp.sum(y * y, axis=-1, keepdims=True) + EPS)
        if j == 0:
            y = y * (DN_DK ** -0.5)
        dst[...] = y

    sl = sl_ref[...]
    lane = lax.broadcasted_iota(jnp.int32, (seq, LANES), 1)
    beta_all = _sigmoid(sl)
    xs = sl + dtb_ref[...]
    softplus = jnp.maximum(xs, 0.0) + jnp.log1p(jnp.exp(-jnp.abs(xs)))
    g_all = -jnp.exp(alog_ref[...]) * softplus
    cols = jnp.zeros((seq, LANES), F32)
    for dr in range(2):
        beta = jnp.sum(jnp.where(lane == dr * DN_HEADS + head, beta_all, 0.0), axis=-1, keepdims=True)
        g = jnp.sum(jnp.where(lane == 2 * DN_HEADS + dr * DN_HEADS + head, g_all, 0.0), axis=-1, keepdims=True)
        cols = jnp.where(lane == dr, beta, cols)
        cols = jnp.where(lane == 2 + dr, g, cols)
    col_ref[...] = cols

    ii = lax.broadcasted_iota(jnp.int32, (c_len, c_len), 0)
    jj = lax.broadcasted_iota(jnp.int32, (c_len, c_len), 1)
    eye = jnp.where(ii == jj, 1.0, 0.0).astype(F32)
    causal = (ii >= jj, ii <= jj)
    strict = (ii > jj, ii < jj)
    tri16 = tuple(jnp.where(m, 1.0, 0.0).astype(BF16) for m in causal)
    level_masks = []
    for dr in range(2):
        hi_idx, lo_idx = (ii, jj) if dr == 0 else (jj, ii)
        masks = []
        lvl = 0
        while (1 << lvl) < c_len:
            masks.append(((hi_idx >> (lvl + 1)) == (lo_idx >> (lvl + 1)))
                         & ((hi_idx >> lvl) == (lo_idx >> lvl) + 1))
            lvl += 1
        level_masks.append(masks)

    group = min(DN_GROUP, n_chunks)
    chains = [(t, dr) for t in range(group) for dr in range(2)]

    def group_body(i, carry):
        cs = [i * group + t for t in range(group)]
        rows = [pl.ds(pl.multiple_of(c * c_len, c_len), c_len) for c in cs]
        q = [qn_ref[r, :] for r in rows]
        k = [kn_ref[r, :] for r in rows]
        v = [vn_ref[r, :] for r in rows]
        blk = [col_ref[r, :] for r in rows]
        q16 = [x.astype(BF16) for x in q]
        k16 = [x.astype(BF16) for x in k]
        qk = [_dot_nt(a, b) for a, b in zip(q16, k16)]
        kk = [_dot_nt(b, b) for b in k16]
        parts = [_split3(x) for x in blk]
        gc_col = [sum(_dot(tri16[dr], p) for p in parts[t])[:, 2 + dr:3 + dr] for t, dr in chains]
        gc_row = [sum(_dot_tn(p, tri16[1 - dr]) for p in parts[t])[2 + dr:3 + dr, :] for t, dr in chains]
        beta = [blk[t][:, dr:dr + 1] for t, dr in chains]
        decay = [jnp.where(causal[dr], jnp.exp(jnp.where(causal[dr], gc_col[n] - gc_row[n], 0.0)), 0.0)
                 for n, (t, dr) in enumerate(chains)]
        lm = [jnp.where(strict[dr], (beta[n] * kk[t]) * decay[n], 0.0) for n, (t, dr) in enumerate(chains)]
        lm16 = [x.astype(BF16) for x in lm]
        zero16 = jnp.zeros((c_len, c_len), BF16)
        xs = [eye - jnp.where(level_masks[dr][0], lm[n], 0.0) for n, (t, dr) in enumerate(chains)]
        for lvl in range(1, len(level_masks[0])):
            ys = [_dot(jnp.where(level_masks[dr][lvl], lm16[n], zero16), xs[n].astype(BF16))
                  for n, (t, dr) in enumerate(chains)]
            xs = [xs[n] - _dot(xs[n].astype(BF16), ys[n].astype(BF16)) for n in range(len(chains))]
        tmat = [x.astype(BF16) for x in xs]
        eg = [jnp.exp(x) for x in gc_col]
        g_last = [gc_col[n][c_len - 1:c_len] if dr == 0 else gc_col[n][0:1] for n, (t, dr) in enumerate(chains)]
        us = [_dot(tmat[n], (v[t] * beta[n]).astype(BF16)) for n, (t, dr) in enumerate(chains)]
        ws = [_dot(tmat[n], (k[t] * (beta[n] * eg[n])).astype(BF16)) for n, (t, dr) in enumerate(chains)]
        for n, (t, dr) in enumerate(chains):
            c = cs[t]
            u_ref[dr, rows[t], :] = us[n]
            wq_ref[dr, pl.ds(pl.multiple_of(2 * c * c_len, c_len), c_len), :] = ws[n].astype(BF16)
            wq_ref[dr, pl.ds(pl.multiple_of(2 * c * c_len + c_len, c_len), c_len), :] = (q[t] * eg[n]).astype(BF16)
            kd_ref[dr, rows[t], :] = (k[t] * jnp.exp(g_last[n] - gc_col[n])).astype(BF16)
            in_ref[dr, rows[t], :] = (qk[t] * decay[n]).astype(BF16)
            gl_ref[dr, pl.ds(c, 1), :] = jnp.broadcast_to(jnp.exp(g_last[n]), (1, LANES))
        return carry

    lax.fori_loop(0, n_chunks // group, group_body, 0)

    def step(i, states):
        cs = [i, n_chunks - 1 - i]
        rows = [pl.ds(pl.multiple_of(c * c_len, c_len), c_len) for c in cs]
        ws_qs = [_dot(wq_ref[dr, pl.ds(pl.multiple_of(2 * cs[dr] * c_len, 2 * c_len), 2 * c_len), :],
                      states[dr].astype(BF16)) for dr in range(2)]
        v16 = [(u_ref[dr, rows[dr], :] - ws_qs[dr][0:c_len]).astype(BF16) for dr in range(2)]
        intra = [_dot(in_ref[dr, rows[dr], :], v16[dr]) for dr in range(2)]
        upd = [_dot_tn(kd_ref[dr, rows[dr], :], v16[dr]) for dr in range(2)]
        for dr in range(2):
            of_ref[dr, rows[dr], :] = ws_qs[dr][c_len:2 * c_len] + intra[dr]
        return tuple(states[dr] * gl_ref[dr, pl.ds(cs[dr], 1), :] + upd[dr] for dr in range(2))

    s_fin = lax.fori_loop(0, n_chunks, step, (s0_ref[0], s0_ref[1]))
    sfin_ref[0] = s_fin[0]
    sfin_ref[1] = s_fin[1]

    o = of_ref[0] + of_ref[1]
    y = o * lax.rsqrt(jnp.mean(o * o, axis=-1, keepdims=True) + EPS) * ng_ref[...]
    z = z_ref[...]
    o_ref[...] = y * (z * _sigmoid(z))


def _dn_call(proj, cw, alog, dtb, ng, s0, s0_map, *, batch, seq):
    nh = DN_HEADS
    n_chunks = seq // DN_TILE
    vec = pl.BlockSpec((1, LANES), lambda b, h: (0, 0))
    nsub = max(n_chunks, SUBLANES)
    return pl.pallas_call(
        functools.partial(_dn_kernel, seq=seq),
        grid=(batch, nh),
        in_specs=[pl.BlockSpec((seq, LANES), lambda b, h: (b, COL_DQ * 4 + h)),
                  pl.BlockSpec((seq, LANES), lambda b, h: (b, COL_DQ * 4 + nh + h)),
                  pl.BlockSpec((seq, LANES), lambda b, h: (b, COL_DQ * 4 + 2 * nh + h)),
                  pl.BlockSpec((seq, LANES), lambda b, h: (b, COL_DZ * 4 + h)),
                  pl.BlockSpec((seq, LANES), lambda b, h: (b, COL_SC128)),
                  pl.BlockSpec((3, CONV_W, LANES), lambda b, h: (0, 0, h)),
                  vec, vec, vec,
                  pl.BlockSpec((None, None, 2, None, DN_DK, DN_DK), s0_map)],
        out_specs=[pl.BlockSpec((seq, LANES), lambda b, h: (b, h)),
                   pl.BlockSpec((None, 2, None, DN_DK, DN_DK), lambda b, h: (b, 0, h, 0, 0))],
        out_shape=[jax.ShapeDtypeStruct((batch * seq, BRANCH_W), F32),
                   jax.ShapeDtypeStruct((batch, 2, nh, DN_DK, DN_DK), F32)],
        scratch_shapes=[pltpu.VMEM((seq + 2 * SUBLANES, LANES), F32),
                        pltpu.VMEM((seq, LANES), F32),
                        pltpu.VMEM((seq, LANES), F32),
                        pltpu.VMEM((seq, LANES), F32),
                        pltpu.VMEM((seq, LANES), F32),
                        pltpu.VMEM((2, seq, LANES), F32),
                        pltpu.VMEM((2, 2 * seq, LANES), BF16),
                        pltpu.VMEM((2, seq, LANES), BF16),
                        pltpu.VMEM((2, seq, DN_TILE), BF16),
                        pltpu.VMEM((2, nsub, LANES), F32),
                        pltpu.VMEM((2, seq, LANES), F32)],
        compiler_params=_params(2),
        name="deltanet",
    )(proj, proj, proj, proj, proj, cw, alog, dtb, ng, s0)


def _layer_params(l, w_ffn_gate, w_ffn_up, w_ffn_down, w_in, lru_conv_w, lru_conv_b, lru_w_r, lru_b_r,
                  lru_w_i, lru_b_i, lru_lambda, gqa_q_norm, gqa_k_norm, na_rpb, dn_conv_w, dn_a_log,
                  dn_dt_bias, dn_norm_g, w_branch, w_out, norm_g):
    offs = np.cumsum((0,) + IN_WIDTHS)
    w = w_in[l]
    seg = [w[:, offs[i]:offs[i + 1]] for i in range(len(IN_WIDTHS))]
    (a_x, a_y, b_q, b_k, b_v, c_q, c_k, c_v, d_q, d_k, d_v, d_z, d_b, d_a, g_lin) = seg
    perm = np.asarray(GQA_PERM)
    b_q = b_q.reshape(D_MODEL, GQA_HEADS, HEAD_DIM)[:, perm].reshape(D_MODEL, BRANCH_W)
    pad = jnp.zeros((D_MODEL, PROJ_W - 5120 - 2 * LANES - 16), F32)
    w_main = jnp.concatenate([a_x, a_y, b_q, c_q, c_k, c_v, d_q, d_k, d_v, d_z, b_k, b_v, d_b, d_a, pad],
                             axis=1).astype(BF16)

    def block_diag(wb):
        wb = wb.reshape(2, LRU_BLOCKS // 2, 2, LRU_BW, LRU_BW)
        z = jnp.zeros_like(wb[:, :, 0])
        return jnp.concatenate([jnp.concatenate([wb[:, :, 0], z], axis=-1),
                                jnp.concatenate([z, wb[:, :, 1]], axis=-1)], axis=-2)

    wr, wi = block_diag(lru_w_r[l]), block_diag(lru_w_i[l])
    lru_wg = jnp.concatenate([wr[0], wi[0], wr[1], wi[1]], axis=-1).astype(BF16)
    ncol = BRANCH_W // LANES

    def col_blocks(v):
        return v.reshape(ncol, LANES)

    lru_bg = jnp.concatenate([col_blocks(lru_b_r[l, 0]), col_blocks(lru_b_i[l, 0]),
                              col_blocks(lru_b_r[l, 1]), col_blocks(lru_b_i[l, 1])], axis=-1)[:, None, :]

    wb = w_branch[l]
    wb_b = wb[1].reshape(GQA_HEADS, HEAD_DIM, D_MODEL)[perm].reshape(BRANCH_W, D_MODEL)
    wb = jnp.stack([wb[0], wb_b, wb[2], wb[3]], axis=0).astype(BF16)

    lane_pad = jnp.zeros((LANES - 4 * DN_HEADS,), F32)
    alog = jnp.concatenate([jnp.zeros((2 * DN_HEADS,), F32), dn_a_log[l].reshape(-1), lane_pad])[None, :]
    dtb = jnp.concatenate([jnp.zeros((2 * DN_HEADS,), F32), dn_dt_bias[l].reshape(-1), lane_pad])[None, :]

    return dict(
        norm_g=norm_g[l][:, None, :],
        wg=w_ffn_gate[l].astype(BF16), wu=w_ffn_up[l].astype(BF16), wd=w_ffn_down[l].astype(BF16),
        w_main=w_main, w_gate=g_lin.astype(BF16),
        lru_cw=lru_conv_w[l], lru_cb=lru_conv_b[l][None, :], lru_wg=lru_wg, lru_bg=lru_bg,
        lru_lam=lru_lambda[l],
        gq=jnp.tile(gqa_q_norm[l], 2)[None, :], gk=jnp.tile(gqa_k_norm[l], 2)[None, :],
        rpb=na_rpb[l],
        dn_cw=dn_conv_w[l].reshape(CONV_W, 3, BRANCH_W).transpose(1, 0, 2),
        dn_alog=alog, dn_dtb=dtb, dn_ng=dn_norm_g[l][None, :],
        wb=wb, w_out=w_out[l].astype(BF16),
    )


def _rope_tables(seq):
    pos = jnp.arange(seq)
    half = HEAD_DIM // 2
    inv = jnp.power(ROPE_BASE, -jnp.arange(0, half, 2, dtype=F32) / half)
    ang_r = (pos // GRID_W).astype(F32)[:, None] * inv[None, :]
    ang_c = (pos % GRID_W).astype(F32)[:, None] * inv[None, :]
    cos = jnp.concatenate([jnp.cos(ang_r)] * 2 + [jnp.cos(ang_c)] * 2, axis=-1)
    sin = jnp.concatenate([-jnp.sin(ang_r), jnp.sin(ang_r), -jnp.sin(ang_c), jnp.sin(ang_c)], axis=-1)
    return jnp.tile(cos, (1, 2)), jnp.tile(sin, (1, 2))


def _na_bias_table(rpb):
    qc = np.arange(GRID_W)
    cs = np.clip(qc - NA_WIN_C // 2, 0, GRID_W - NA_WIN_C)
    kc = np.arange(GRID_W)
    inwin = (kc[None, :] >= cs[:, None]) & (kc[None, :] < cs[:, None] + NA_WIN_C)
    coff = np.clip(kc[None, :] - qc[:, None] + NA_WIN_C - 1, 0, 2 * NA_WIN_C - 2)
    t = jnp.where(inwin[None, None], rpb.astype(F32)[:, :, coff], NEG_BIG)
    return jnp.concatenate([t[:, :-1], t[:, 1:]], axis=-1)


def _layer(x, mod, p, *, batch, seq, latent, layer, caches, tables, final_g, tm):
    g = p["norm_g"]
    x = _ffn_call(x, mod, g[0], p["wg"][0], p["wu"][0], p["wd"][0], final_g, mi=0, final=False, tm=tm)
    proj = _inproj_call(x, mod, g[1], p["w_main"], tm=tm)

    if latent:
        cache_ak, cache_av, cache_nk, cache_nv, state_lru, state_delta = caches
        h0 = state_lru[:, layer]
        s0 = state_delta
        s0_map = lambda b, h: (b, layer, 0, h, 0, 0)
    else:
        h0 = jnp.zeros((batch, 2, BRANCH_W), F32)
        s0 = jnp.zeros((1, 1, 2, 1, DN_DK, DN_DK), F32)
        s0_map = lambda b, h: (0, 0, 0, 0, 0, 0)

    o_a, lru_fin = _lru_call(proj, h0, p["lru_cw"], p["lru_cb"], p["lru_wg"], p["lru_bg"], p["lru_lam"],
                             batch=batch, seq=seq)

    cos, sin = tables["rope"] if latent else (None, None)
    qn, kn = _prep_call(proj, p["gq"], p["gk"], cos, sin, batch=batch, seq=seq, rope=latent)
    tq = min(seq, 256)
    kv_new = (kn, pl.BlockSpec((seq, LANES), lambda b, i: (b, 0)),
              proj, pl.BlockSpec((seq, LANES), lambda b, i: (b, COL_BV128)))
    if latent:
        gqa_cache = pl.BlockSpec((None, None, PAST_LEN, LANES), lambda b, i: (b, layer, 0, 0))
        sources = [kv_new, (cache_ak, gqa_cache, cache_av, gqa_cache)]
    else:
        sources = [kv_new]
    o_b = _attn_call(qn, 0, sources, batch=batch, seq=seq, tq=tq, nkb=1, qscale=1.0)

    if latent:
        o_c = _na_call(proj, cache_nk, cache_nv, _na_bias_table(p["rpb"]), layer, batch=batch, seq=seq)
    else:
        src = (proj, pl.BlockSpec((seq, BRANCH_W), lambda b, i: (b, COL_CK)),
               proj, pl.BlockSpec((seq, BRANCH_W), lambda b, i: (b, COL_CV)))
        o_c = _attn_call(proj, COL_CQ, [src], batch=batch, seq=seq, tq=tq, nkb=4, qscale=HEAD_DIM ** -0.5)

    o_d, dn_fin = _dn_call(proj, p["dn_cw"], p["dn_alog"], p["dn_dtb"], p["dn_ng"], s0, s0_map,
                           batch=batch, seq=seq)

    x = _merge_call(x, mod, g[1], (o_a, o_b, o_c, o_d), p["w_gate"], p["wb"], p["w_out"], tm=tm)
    x = _ffn_call(x, mod, g[2], p["wg"][1], p["wu"][1], p["wd"][1], final_g, mi=6,
                  final=(layer == DEPTH - 1), tm=tm)

    new_ctx = None
    if not latent:
        new_ctx = (kn.reshape(batch, seq, GQA_KV, HEAD_DIM),
                   proj[:, COL_BV128 * LANES:(COL_BV128 + 1) * LANES].reshape(batch, seq, GQA_KV, HEAD_DIM),
                   proj[:, COL_CK * BRANCH_W:(COL_CK + 1) * BRANCH_W].reshape(batch, seq, NA_HEADS, HEAD_DIM),
                   proj[:, COL_CV * BRANCH_W:(COL_CV + 1) * BRANCH_W].reshape(batch, seq, NA_HEADS, HEAD_DIM),
                   lru_fin, dn_fin)
    return x, new_ctx


def kernel(x_prompt, x_sample, c, cache_attn_k, cache_attn_v, cache_na_k, cache_na_v, state_lru, state_delta, c_ctx, w_mod, b_mod, norm_g, w_ffn_gate, w_ffn_up, w_ffn_down, w_in, lru_conv_w, lru_conv_b, lru_w_r, lru_b_r, lru_w_i, lru_b_i, lru_lambda, gqa_q_norm, gqa_k_norm, na_rpb, dn_conv_w, dn_a_log, dn_dt_bias, dn_norm_g, w_branch, w_out, final_norm_g):
    batch_c, seq_c, _ = x_prompt.shape
    batch_l, seq_l, _ = x_sample.shape
    assert batch_l + 1 <= SUBLANES

    cs = jnp.concatenate([c_ctx[None, :], c, jnp.zeros((SUBLANES - 1 - batch_l, D_MODEL), F32)], axis=0)
    mod_all = _mod_call(cs, w_mod, b_mod)

    caches = (cache_attn_k.reshape(batch_l, DEPTH, PAST_LEN, GQA_KV * HEAD_DIM),
              cache_attn_v.reshape(batch_l, DEPTH, PAST_LEN, GQA_KV * HEAD_DIM),
              cache_na_k.reshape(batch_l, DEPTH, PAST_LEN, BRANCH_W),
              cache_na_v.reshape(batch_l, DEPTH, PAST_LEN, BRANCH_W),
              state_lru, state_delta)
    tables = {"rope": _rope_tables(seq_l)}
    final_g = final_norm_g[None, :]

    xc = x_prompt.reshape(1, batch_c * seq_c, D_MODEL)
    xl = x_sample
    ctx_out = []
    for l in range(DEPTH):
        p = _layer_params(l, w_ffn_gate, w_ffn_up, w_ffn_down, w_in, lru_conv_w, lru_conv_b, lru_w_r,
                          lru_b_r, lru_w_i, lru_b_i, lru_lambda, gqa_q_norm, gqa_k_norm, na_rpb, dn_conv_w,
                          dn_a_log, dn_dt_bias, dn_norm_g, w_branch, w_out, norm_g)
        mod_c = mod_all[l, 0:1].reshape(1, N_MOD, D_MODEL)
        mod_l = mod_all[l, 1:1 + batch_l].reshape(batch_l, N_MOD, D_MODEL)
        xc, new_ctx = _layer(xc, mod_c, p, batch=batch_c, seq=seq_c, latent=False, layer=l, caches=None,
                             tables=tables, final_g=final_g, tm=256)
        ctx_out.append(new_ctx)
        xl, _ = _layer(xl, mod_l, p, batch=batch_l, seq=seq_l, latent=True, layer=l, caches=caches,
                       tables=tables, final_g=final_g, tm=256)

    stacked = [jnp.stack([ctx_out[l][i] for l in range(DEPTH)], axis=1) for i in range(6)]
    return (xc.reshape(batch_c, seq_c, D_MODEL), xl, *stacked)
```

```python
import functools

import numpy as np
import jax
import jax.numpy as jnp
from jax import lax
from jax.experimental import pallas as pl
from jax.experimental.pallas import tpu as pltpu

F32 = jnp.float32
BF16 = jnp.bfloat16

D_MODEL = 1024
DEPTH = 4
GRID_W = 64
N_BRANCH = 4
BRANCH_W = 512
N_MOD = 9
D_FF = 2816
EPS = 1e-6
CONV_W = 4
LRU_BLOCKS = 8
LRU_BW = 64
LRU_C = 8.0
HEAD_DIM = 64
GQA_HEADS = 8
GQA_KV = 2
ROPE_BASE = 10000.0
NA_HEADS = 8
NA_WIN_R = 8
NA_WIN_C = 16
DN_DK = 128
DN_HEADS = 4
LRU_PITCH_PAD = 8
DN_TILE = 128
DN_CHAINS = 8
PAST_LEN = 512
IN_WIDTHS = (512, 512, 512, 128, 128, 512, 512, 512, 512, 512, 512, 512, 8, 8, 4096)

LANES = 128
SUBLANES = 8
PROJ_W = 5632
COL_BQ, COL_CQ, COL_CK, COL_CV, COL_DQ, COL_DZ = 2, 3, 4, 5, 6, 9
COL_BK128, COL_BV128, COL_SC128 = 40, 41, 42
VMEM_LIMIT = 56 * 1024 * 1024
NEG_BIG = -1e30
GQA_PERM = (0, 4, 1, 5, 2, 6, 3, 7)


def _params(n):
    return pltpu.CompilerParams(dimension_semantics=("arbitrary",) * n, vmem_limit_bytes=VMEM_LIMIT)


def _const_spec(shape):
    nd = len(shape)
    return pl.BlockSpec(shape, lambda *_: (0,) * nd, pipeline_mode=pl.Buffered(1))


def _layer_spec(arr, layer, *sub):
    tail = arr.shape[1 + len(sub):]
    index = (layer,) + tuple(sub) + (0,) * len(tail)
    return pl.BlockSpec((None,) * (1 + len(sub)) + tuple(tail), lambda *_: index, pipeline_mode=pl.Buffered(1))


def _mod_spec(layer, row0):
    return pl.BlockSpec((None, None, N_MOD, D_MODEL), lambda b, i: (layer, row0 + b, 0, 0))


def _dot(a, b):
    return jnp.dot(a, b, preferred_element_type=F32)


def _dot_nt(a, b):
    return lax.dot_general(a, b, (((1,), (1,)), ((), ())), preferred_element_type=F32)


def _dot_tn(a, b):
    return lax.dot_general(a, b, (((0,), (0,)), ((), ())), preferred_element_type=F32)


def _split(x):
    hi = x.astype(BF16)
    lo = (x - hi.astype(F32)).astype(BF16)
    return hi, lo


def _dot3(a, b):
    ah, al = _split(a)
    bh, bl = _split(b)
    return _dot(ah, bh) + (_dot(al, bh) + _dot(ah, bl))


def _sigmoid(x):
    return 0.5 * jnp.tanh(0.5 * x) + 0.5


def _modnorm(x, g, shift, scale):
    ms = jnp.mean(x * x, axis=-1, keepdims=True)
    return (x * lax.rsqrt(ms + EPS) * g) * (1.0 + scale) + shift


def _mod_kernel(c_ref, w_ref, b_ref, o_ref):
    c = c_ref[...]
    o_ref[0] = _dot3(c * _sigmoid(c), w_ref[0]) + b_ref[0]


def _mod_call(cs, w_mod, b_mod):
    tn = 1024
    n = N_MOD * D_MODEL
    return pl.pallas_call(
        _mod_kernel,
        grid=(DEPTH, n // tn),
        in_specs=[pl.BlockSpec((SUBLANES, D_MODEL), lambda l, j: (0, 0)),
                  pl.BlockSpec((1, D_MODEL, tn), lambda l, j: (l, 0, j)),
                  pl.BlockSpec((1, 1, tn), lambda l, j: (l, 0, j))],
        out_specs=pl.BlockSpec((1, SUBLANES, tn), lambda l, j: (l, 0, j)),
        out_shape=jax.ShapeDtypeStruct((DEPTH, SUBLANES, n), F32),
        compiler_params=_params(2),
        name="mod",
    )(cs, w_mod, b_mod.reshape(DEPTH, 1, n))


def _ffn_kernel(x_ref, mod_ref, g_ref, wg_ref, wu_ref, wd_ref, gf_ref, o_ref, *, mi, final):
    x = x_ref[0]
    mod = mod_ref[...]
    h = _modnorm(x, g_ref[...], mod[mi:mi + 1], mod[mi + 1:mi + 2]).astype(BF16)
    gt = _dot(h, wg_ref[...])
    up = _dot(h, wu_ref[...])
    a = (gt * _sigmoid(gt) * up).astype(BF16)
    y = x + 0.5 * mod[mi + 2:mi + 3] * _dot(a, wd_ref[...])
    if final:
        ms = jnp.mean(y * y, axis=-1, keepdims=True)
        y = y * lax.rsqrt(ms + EPS) * gf_ref[...]
    o_ref[0] = y


def _ffn_call(x, mod_all, params, gf, *, layer, which, row0, final, tm):
    nb, rows, _ = x.shape
    return pl.pallas_call(
        functools.partial(_ffn_kernel, mi=6 * which, final=final),
        grid=(nb, rows // tm),
        in_specs=[pl.BlockSpec((1, tm, D_MODEL), lambda b, i: (b, i, 0)),
                  _mod_spec(layer, row0),
                  _layer_spec(params["norm_g"], layer, 2 * which),
                  _layer_spec(params["wg"], layer, which),
                  _layer_spec(params["wu"], layer, which),
                  _layer_spec(params["wd"], layer, which),
                  _const_spec((1, D_MODEL))],
        out_specs=pl.BlockSpec((1, tm, D_MODEL), lambda b, i: (b, i, 0)),
        out_shape=jax.ShapeDtypeStruct(x.shape, F32),
        compiler_params=_params(2),
        name="ffn",
    )(x, mod_all, params["norm_g"], params["wg"], params["wu"], params["wd"], gf)


def _inproj_kernel(x_ref, mod_ref, g_ref, w_ref, o_ref):
    mod = mod_ref[...]
    h = _modnorm(x_ref[0], g_ref[...], mod[3:4], mod[4:5]).astype(BF16)
    o_ref[...] = _dot(h, w_ref[...])


def _inproj_call(x, mod_all, params, *, layer, row0, tm):
    nb, rows, _ = x.shape
    nt = rows // tm
    return pl.pallas_call(
        _inproj_kernel,
        grid=(nb, nt),
        in_specs=[pl.BlockSpec((1, tm, D_MODEL), lambda b, i: (b, i, 0)),
                  _mod_spec(layer, row0),
                  _layer_spec(params["norm_g"], layer, 1),
                  _layer_spec(params["w_main"], layer)],
        out_specs=pl.BlockSpec((tm, PROJ_W), lambda b, i: (b * nt + i, 0)),
        out_shape=jax.ShapeDtypeStruct((nb * rows, PROJ_W), F32),
        compiler_params=_params(2),
        name="inproj",
    )(x, mod_all, params["norm_g"], params["w_main"])


def _merge_kernel(x_ref, mod_ref, g_ref, oa_ref, ob_ref, oc_ref, od_ref, wgate_ref, wb_ref, wout_ref, o_ref):
    x = x_ref[0]
    mod = mod_ref[...]
    h = _modnorm(x, g_ref[...], mod[3:4], mod[4:5]).astype(BF16)
    acc = None
    for n, ref in enumerate((oa_ref, ob_ref, oc_ref, od_ref)):
        gate = _sigmoid(_dot(h, wgate_ref[:, n * D_MODEL:(n + 1) * D_MODEL]))
        term = gate * _dot(ref[...].astype(BF16), wb_ref[n])
        acc = term if acc is None else acc + term
    o_ref[0] = x + mod[5:6] * _dot(acc.astype(BF16), wout_ref[...])


def _merge_call(x, mod_all, params, outs, *, layer, row0, tm):
    nb, rows, _ = x.shape
    nt = rows // tm
    ospec = pl.BlockSpec((tm, BRANCH_W), lambda b, i: (b * nt + i, 0))
    return pl.pallas_call(
        _merge_kernel,
        grid=(nb, nt),
        in_specs=[pl.BlockSpec((1, tm, D_MODEL), lambda b, i: (b, i, 0)),
                  _mod_spec(layer, row0),
                  _layer_spec(params["norm_g"], layer, 1),
                  ospec, ospec, ospec, ospec,
                  _layer_spec(params["w_gate"], layer),
                  _layer_spec(params["wb"], layer),
                  _layer_spec(params["w_out"], layer)],
        out_specs=pl.BlockSpec((1, tm, D_MODEL), lambda b, i: (b, i, 0)),
        out_shape=jax.ShapeDtypeStruct(x.shape, F32),
        compiler_params=_params(2),
        name="merge",
    )(x, mod_all, params["norm_g"], *outs, params["w_gate"], params["wb"], params["w_out"])


def _log_sigmoid(x):
    return jnp.minimum(x, 0.0) - jnp.log1p(jnp.exp(-jnp.abs(x)))


def _gelu_tanh(x):
    return x * (0.5 * (1.0 + jnp.tanh(0.7978845608028654 * (x + 0.044715 * (x * x * x)))))


def _lru_kernel(ax_ref, ay_ref, h0_ref, cw_ref, cb_ref, wg_ref, bg_ref, lam_ref, o_ref, fin_ref,
                xp_ref, a_ref, u_ref, *, seq):
    lc = seq // SUBLANES
    pitch = lc + LRU_PITCH_PAD
    zero8 = jnp.zeros((SUBLANES, LANES), F32)
    xp_ref[0:SUBLANES, :] = zero8
    xp_ref[SUBLANES + seq:2 * SUBLANES + seq, :] = zero8
    xp_ref[SUBLANES:SUBLANES + seq, :] = ax_ref[...]
    cw = cw_ref[...]
    xa = cb_ref[...] + cw[0:1] * xp_ref[pl.ds(SUBLANES - 2, seq), :]
    for k in range(1, CONV_W):
        xa = xa + cw[k:k + 1] * xp_ref[pl.ds(SUBLANES - 2 + k, seq), :]
    gates = _dot(xa.astype(BF16), wg_ref[...]) + bg_ref[...]
    lam = lam_ref[...]
    for dr in range(2):
        r = _sigmoid(gates[:, (2 * dr) * LANES:(2 * dr + 1) * LANES])
        i = _sigmoid(gates[:, (2 * dr + 1) * LANES:(2 * dr + 2) * LANES])
        log_a = (LRU_C * r) * _log_sigmoid(lam[dr:dr + 1])
        a = jnp.exp(log_a)
        one_m_a2 = -jnp.tanh(log_a) * (a * a + 1.0)
        root = jnp.where(one_m_a2 > 0.0, one_m_a2 * lax.rsqrt(one_m_a2), 0.0)
        u = root * (i * xa)
        for k in range(SUBLANES):
            a_ref[dr, pl.ds(k * pitch, lc), :] = a[k * lc:(k + 1) * lc]
            u_ref[dr, pl.ds(k * pitch, lc), :] = u[k * lc:(k + 1) * lc]

    sub = lax.broadcasted_iota(jnp.int32, (SUBLANES, LANES), 0)
    h0 = h0_ref[...]
    hf0 = jnp.where(sub == 0, h0[0:1], 0.0)
    hb0 = jnp.where(sub == SUBLANES - 1, h0[1:2], 0.0)
    ones = jnp.ones((SUBLANES, LANES), F32)

    def body(s, carry):
        hf, pf, hb, pb = carry
        rows_f = pl.ds(s, SUBLANES, stride=pitch)
        rows_b = pl.ds(lc - 1 - s, SUBLANES, stride=pitch)
        af = a_ref[0, rows_f, :]
        hf = af * hf + u_ref[0, rows_f, :]
        pf = af * pf
        u_ref[0, rows_f, :] = hf
        a_ref[0, rows_f, :] = pf
        ab = a_ref[1, rows_b, :]
        hb = ab * hb + u_ref[1, rows_b, :]
        pb = ab * pb
        u_ref[1, rows_b, :] = hb
        a_ref[1, rows_b, :] = pb
        return hf, pf, hb, pb

    hf, pf, hb, pb = lax.fori_loop(0, lc, body, (hf0, ones, hb0, ones))

    cf = [jnp.zeros((1, LANES), F32)]
    for k in range(1, SUBLANES):
        cf.append(hf[k - 1:k] + pf[k - 1:k] * cf[k - 1])
    fin_ref[0:1, :] = hf[SUBLANES - 1:SUBLANES] + pf[SUBLANES - 1:SUBLANES] * cf[SUBLANES - 1]
    cb = [None] * SUBLANES
    cb[SUBLANES - 1] = jnp.zeros((1, LANES), F32)
    for k in range(SUBLANES - 2, -1, -1):
        cb[k] = hb[k + 1:k + 2] + pb[k + 1:k + 2] * cb[k + 1]
    fin_ref[1:2, :] = hb[0:1] + pb[0:1] * cb[0]

    for k in range(SUBLANES):
        rows = pl.ds(k * lc, lc)
        held = pl.ds(k * pitch, lc)
        h = (u_ref[0, held, :] + a_ref[0, held, :] * cf[k]) + (u_ref[1, held, :] + a_ref[1, held, :] * cb[k])
        o_ref[rows, :] = h * _gelu_tanh(ay_ref[rows, :])


def _lru_call(proj, h0, h0_map, params, *, layer, batch, seq):
    ncol = BRANCH_W // LANES
    pad_rows = SUBLANES * LRU_PITCH_PAD
    return pl.pallas_call(
        functools.partial(_lru_kernel, seq=seq),
        grid=(batch, ncol),
        in_specs=[pl.BlockSpec((seq, LANES), lambda b, c: (b, c)),
                  pl.BlockSpec((seq, LANES), lambda b, c: (b, ncol + c)),
                  pl.BlockSpec((None, None, 2, LANES), h0_map),
                  pl.BlockSpec((None, CONV_W, LANES), lambda b, c: (layer, 0, c)),
                  pl.BlockSpec((None, 1, LANES), lambda b, c: (layer, 0, c)),
                  pl.BlockSpec((None, None, LANES, 4 * LANES), lambda b, c: (layer, c, 0, 0)),
                  pl.BlockSpec((None, None, 1, 4 * LANES), lambda b, c: (layer, c, 0, 0)),
                  pl.BlockSpec((None, 2, LANES), lambda b, c: (layer, 0, c))],
        out_specs=[pl.BlockSpec((seq, LANES), lambda b, c: (b, c)),
                   pl.BlockSpec((None, 2, LANES), lambda b, c: (b, 0, c))],
        out_shape=[jax.ShapeDtypeStruct((batch * seq, BRANCH_W), F32),
                   jax.ShapeDtypeStruct((batch, 2, BRANCH_W), F32)],
        scratch_shapes=[pltpu.VMEM((seq + 2 * SUBLANES, LANES), F32),
                        pltpu.VMEM((2, seq + pad_rows, LANES), F32),
                        pltpu.VMEM((2, seq + pad_rows, LANES), F32)],
        compiler_params=_params(2),
        name="lru",
    )(proj, proj, h0, params["lru_cw"], params["lru_cb"], params["lru_wg"], params["lru_bg"], params["lru_lam"])


def _prep_kernel(*refs, rope):
    if rope:
        q_ref, kv_ref, gq_ref, gk_ref, cos_ref, sin_ref, qn_ref, kn_ref = refs
    else:
        q_ref, kv_ref, gq_ref, gk_ref, qn_ref, kn_ref = refs
    rows = q_ref.shape[0]
    lane = lax.broadcasted_iota(jnp.int32, (rows, LANES), 1)
    lo = lane < HEAD_DIM
    first16 = (lane & 16) == 0

    def head_norm(x, g):
        sq = x * x
        s_lo = jnp.sum(jnp.where(lo, sq, 0.0), axis=-1, keepdims=True)
        s_hi = jnp.sum(jnp.where(lo, 0.0, sq), axis=-1, keepdims=True)
        ms = jnp.where(lo, s_lo, s_hi) * (1.0 / HEAD_DIM)
        return x * lax.rsqrt(ms + EPS) * g

    def rotate(y):
        if not rope:
            return y
        partner = jnp.where(first16, pltpu.roll(y, LANES - 16, 1), pltpu.roll(y, 16, 1))
        return y * cos_ref[...] + partner * sin_ref[...]

    for p in range(BRANCH_W // LANES):
        cols = slice(p * LANES, (p + 1) * LANES)
        qn_ref[:, cols] = rotate(head_norm(q_ref[:, cols], gq_ref[...])) * (HEAD_DIM ** -0.5)
    kn_ref[...] = rotate(head_norm(kv_ref[:, 0:LANES], gk_ref[...]))


def _prep_call(proj, params, cos, sin, *, layer, batch, seq, rope):
    tab = pl.BlockSpec((seq, LANES), lambda b: (0, 0))
    in_specs = [pl.BlockSpec((seq, BRANCH_W), lambda b: (b, COL_BQ)),
                pl.BlockSpec((seq, BRANCH_W), lambda b: (b, COL_BK128 // 4)),
                _layer_spec(params["gq"], layer), _layer_spec(params["gk"], layer)]
    args = [proj, proj, params["gq"], params["gk"]]
    if rope:
        in_specs += [tab, tab]
        args += [cos, sin]
    return pl.pallas_call(
        functools.partial(_prep_kernel, rope=rope),
        grid=(batch,),
        in_specs=in_specs,
        out_specs=[pl.BlockSpec((seq, BRANCH_W), lambda b: (b, 0)),
                   pl.BlockSpec((seq, LANES), lambda b: (b, 0))],
        out_shape=[jax.ShapeDtypeStruct((batch * seq, BRANCH_W), F32),
                   jax.ShapeDtypeStruct((batch * seq, LANES), F32)],
        compiler_params=_params(1),
        name="attn_prep",
    )(*args)


def _softmax_pv(scores, values):
    m = None
    for s in scores:
        sm = jnp.max(s, axis=-1, keepdims=True)
        m = sm if m is None else jnp.maximum(m, sm)
    den = None
    out = None
    for s, v in zip(scores, values):
        p = jnp.exp(s - m)
        ps = jnp.sum(p, axis=-1, keepdims=True)
        den = ps if den is None else den + ps
        o = _dot(p.astype(BF16), v)
        out = o if out is None else out + o
    return out / den


def _attn_kernel(*refs, nsrc, nkb, qscale):
    q_ref = refs[0]
    src = refs[1:1 + 2 * nsrc]
    o_ref = refs[1 + 2 * nsrc]
    tq = q_ref.shape[0]
    lo = lax.broadcasted_iota(jnp.int32, (tq, LANES), 1) < HEAD_DIM

    loaded = {}

    def kv(p):
        col = p if nkb != 1 else 0
        if col not in loaded:
            cols = slice(col * LANES, (col + 1) * LANES)
            loaded[col] = ([src[2 * i][:, cols].astype(BF16) for i in range(nsrc)],
                           [src[2 * i + 1][:, cols].astype(BF16) for i in range(nsrc)])
        return loaded[col]

    def scores(unit):
        p, half = unit
        qb = q_ref[:, p * LANES:(p + 1) * LANES]
        if qscale != 1.0:
            qb = qb * qscale
        qm = jnp.where(lo if half == 0 else jnp.logical_not(lo), qb, 0.0).astype(BF16)
        return [_dot_nt(qm, k) for k in kv(p)[0]]

    units = [(p, half) for p in range(BRANCH_W // LANES) for half in range(2)]
    pending = scores(units[0])
    first_half = None
    for n, (p, half) in enumerate(units):
        current = pending
        if n + 1 < len(units):
            pending = scores(units[n + 1])
        out = _softmax_pv(current, kv(p)[1])
        if half == 0:
            first_half = out
        else:
            o_ref[:, p * LANES:(p + 1) * LANES] = jnp.where(lo, first_half, out)


def _attn_call(q_arr, q_col, sources, *, batch, seq, tq, nkb, qscale):
    nq = seq // tq
    in_specs = [pl.BlockSpec((tq, BRANCH_W), lambda b, i: (b * nq + i, q_col))]
    args = [q_arr]
    for k_arr, k_spec, v_arr, v_spec in sources:
        in_specs += [k_spec, v_spec]
        args += [k_arr, v_arr]
    return pl.pallas_call(
        functools.partial(_attn_kernel, nsrc=len(sources), nkb=nkb, qscale=qscale),
        grid=(batch, nq),
        in_specs=in_specs,
        out_specs=pl.BlockSpec((tq, BRANCH_W), lambda b, i: (b * nq + i, 0)),
        out_shape=jax.ShapeDtypeStruct((batch * seq, BRANCH_W), F32),
        compiler_params=_params(2),
        name="attn",
    )(*args)


def _na_row_start(r, rows):
    return jnp.clip(r - NA_WIN_R // 2, 0, rows - NA_WIN_R)


def _na_kernel(q_ref, k_ref, v_ref, ck_ref, cv_ref, bias_ref, o_ref, *, rows):
    r = pl.program_id(1)
    rs = _na_row_start(r, rows)
    win = pl.ds(pl.multiple_of(rs * GRID_W, GRID_W), NA_WIN_R * GRID_W)
    first_off = rs - r + NA_WIN_R - 1
    lo = lax.broadcasted_iota(jnp.int32, (GRID_W, LANES), 1) < HEAD_DIM
    ncol = BRANCH_W // LANES
    scores = []
    for p in range(ncol):
        cols = slice(p * LANES, (p + 1) * LANES)
        qb = q_ref[:, cols] * (HEAD_DIM ** -0.5)
        kw = k_ref[win, cols].astype(BF16)
        kc = ck_ref[:, cols].astype(BF16)
        for half in range(2):
            qm = jnp.where(lo if half == 0 else jnp.logical_not(lo), qb, 0.0).astype(BF16)
            scores.append((_dot_nt(qm, kw), _dot_nt(qm, kc)))
    probs = []
    for h, (s_loc, s_ctx) in enumerate(scores):
        bias = jnp.concatenate([bias_ref[h, first_off + 2 * t] for t in range(NA_WIN_R // 2)], axis=-1)
        s_loc = s_loc + bias
        m = jnp.maximum(jnp.max(s_loc, axis=-1, keepdims=True), jnp.max(s_ctx, axis=-1, keepdims=True))
        p_loc = jnp.exp(s_loc - m)
        p_ctx = jnp.exp(s_ctx - m)
        den = jnp.sum(p_loc, axis=-1, keepdims=True) + jnp.sum(p_ctx, axis=-1, keepdims=True)
        probs.append((p_loc.astype(BF16), p_ctx.astype(BF16), den))
    for p in range(ncol):
        cols = slice(p * LANES, (p + 1) * LANES)
        vw = v_ref[win, cols].astype(BF16)
        vc = cv_ref[:, cols].astype(BF16)
        halves = []
        for half in range(2):
            p_loc, p_ctx, den = probs[2 * p + half]
            halves.append((_dot(p_loc, vw) + _dot(p_ctx, vc)) / den)
        o_ref[:, cols] = jnp.where(lo, halves[0], halves[1])


def _na_call(proj, cache_k, cache_v, params, *, layer, batch, seq):
    rows = seq // GRID_W
    cache_spec = pl.BlockSpec((None, None, PAST_LEN, BRANCH_W), lambda b, r: (b, layer, 0, 0))
    return pl.pallas_call(
        functools.partial(_na_kernel, rows=rows),
        grid=(batch, rows),
        in_specs=[pl.BlockSpec((GRID_W, BRANCH_W), lambda b, r: (b * rows + r, COL_CQ)),
                  pl.BlockSpec((seq, BRANCH_W), lambda b, r: (b, COL_CK)),
                  pl.BlockSpec((seq, BRANCH_W), lambda b, r: (b, COL_CV)),
                  cache_spec, cache_spec,
                  _layer_spec(params["na_bias"], layer)],
        out_specs=pl.BlockSpec((GRID_W, BRANCH_W), lambda b, r: (b * rows + r, 0)),
        out_shape=jax.ShapeDtypeStruct((batch * seq, BRANCH_W), F32),
        compiler_params=_params(2),
        name="na",
    )(proj, proj, proj, cache_k, cache_v, params["na_bias"])


def _split3(x):
    hi = x.astype(BF16)
    r = x - hi.astype(F32)
    mid = r.astype(BF16)
    lo = (r - mid.astype(F32)).astype(BF16)
    return hi, mid, lo


def _dn_kernel(q_ref, k_ref, v_ref, z_ref, sl_ref, cw_ref, alog_ref, dtb_ref, ng_ref, s0_ref,
               o_ref, sfin_ref,
               xp_ref, qn_ref, kn_ref, vn_ref, col_ref, u_ref, wq_ref, kd_ref, in_ref, gl_ref, acc_ref,
               *, seq, nhs):
    c_len = DN_TILE
    n_chunks = seq // c_len

    zero8 = jnp.zeros((SUBLANES, LANES), F32)
    xp_ref[0:SUBLANES, :] = zero8
    xp_ref[SUBLANES + seq:2 * SUBLANES + seq, :] = zero8
    for hh in range(nhs):
        lanes = slice(hh * LANES, (hh + 1) * LANES)
        for j, (src, dst) in enumerate(((q_ref, qn_ref), (k_ref, kn_ref), (v_ref, vn_ref))):
            xp_ref[SUBLANES:SUBLANES + seq, :] = src[:, lanes]
            cw = cw_ref[j][:, lanes]
            y = cw[0:1] * xp_ref[pl.ds(SUBLANES - 2, seq), :]
            for t in range(1, CONV_W):
                y = y + cw[t:t + 1] * xp_ref[pl.ds(SUBLANES - 2 + t, seq), :]
            y = y * _sigmoid(y)
            if j < 2:
                y = y * lax.rsqrt(jnp.sum(y * y, axis=-1, keepdims=True) + EPS)
            if j == 0:
                y = y * (DN_DK ** -0.5)
            dst[hh] = y

    sl = sl_ref[...]
    lane = lax.broadcasted_iota(jnp.int32, (seq, LANES), 1)
    beta_all = _sigmoid(sl)
    xs = sl + dtb_ref[...]
    softplus = jnp.maximum(xs, 0.0) + jnp.log1p(jnp.exp(-jnp.abs(xs)))
    g_all = -jnp.exp(alog_ref[...]) * softplus
    for hh in range(nhs):
        head = pl.program_id(1) * nhs + hh
        cols = jnp.zeros((seq, LANES), F32)
        for dr in range(2):
            beta = jnp.sum(jnp.where(lane == dr * DN_HEADS + head, beta_all, 0.0), axis=-1, keepdims=True)
            g = jnp.sum(jnp.where(lane == 2 * DN_HEADS + dr * DN_HEADS + head, g_all, 0.0),
                        axis=-1, keepdims=True)
            cols = jnp.where(lane == dr, beta, cols)
            cols = jnp.where(lane == 2 + dr, g, cols)
        col_ref[hh] = cols
        acc_ref[hh] = jnp.zeros((seq, LANES), F32)

    ii = lax.broadcasted_iota(jnp.int32, (c_len, c_len), 0)
    jj = lax.broadcasted_iota(jnp.int32, (c_len, c_len), 1)
    eye = jnp.where(ii == jj, 1.0, 0.0).astype(F32)
    causal = (ii >= jj, ii <= jj)
    strict = (ii > jj, ii < jj)
    tri16 = tuple(jnp.where(m, 1.0, 0.0).astype(BF16) for m in causal)
    level_masks = []
    for dr in range(2):
        hi_idx, lo_idx = (ii, jj) if dr == 0 else (jj, ii)
        masks = []
        lvl = 0
        while (1 << lvl) < c_len:
            masks.append(((hi_idx >> (lvl + 1)) == (lo_idx >> (lvl + 1)))
                         & ((hi_idx >> lvl) == (lo_idx >> lvl) + 1))
            lvl += 1
        level_masks.append(masks)

    group = max(1, min(DN_CHAINS // (2 * nhs), n_chunks))
    units = [(hh, t) for hh in range(nhs) for t in range(group)]
    chains = [(m, dr) for m in range(len(units)) for dr in range(2)]
    lane_sq = jj

    def group_body(i, carry):
        cs = [i * group + t for hh, t in units]
        rows = [pl.ds(pl.multiple_of(c * c_len, c_len), c_len) for c in cs]
        q = [qn_ref[hh, rows[m], :] for m, (hh, t) in enumerate(units)]
        k = [kn_ref[hh, rows[m], :] for m, (hh, t) in enumerate(units)]
        v = [vn_ref[hh, rows[m], :] for m, (hh, t) in enumerate(units)]
        blk = [col_ref[hh, rows[m], :] for m, (hh, t) in enumerate(units)]
        q16 = [x.astype(BF16) for x in q]
        k16 = [x.astype(BF16) for x in k]
        qk = [_dot_nt(a, b) for a, b in zip(q16, k16)]
        kk = [_dot_nt(b, b) for b in k16]
        parts = [_split3(x) for x in blk]
        cum = [jnp.where(lane_sq == 2, sum(_dot(tri16[0], p) for p in ps), sum(_dot(tri16[1], p) for p in ps))
               for ps in parts]
        cum_t = [x.T for x in cum]
        gc_col = [cum[m][:, 2 + dr:3 + dr] for m, dr in chains]
        gc_row = [cum_t[m][2 + dr:3 + dr, :] for m, dr in chains]
        beta = [blk[m][:, dr:dr + 1] for m, dr in chains]
        decay = [jnp.where(causal[dr], jnp.exp(jnp.where(causal[dr], gc_col[n] - gc_row[n], 0.0)), 0.0)
                 for n, (m, dr) in enumerate(chains)]
        lm = [jnp.where(strict[dr], (beta[n] * kk[m]) * decay[n], 0.0) for n, (m, dr) in enumerate(chains)]
        lm16 = [x.astype(BF16) for x in lm]
        zero16 = jnp.zeros((c_len, c_len), BF16)
        xs = [eye - jnp.where(level_masks[dr][0], lm[n], 0.0) for n, (m, dr) in enumerate(chains)]
        for lvl in range(1, len(level_masks[0])):
            ys = [_dot(jnp.where(level_masks[dr][lvl], lm16[n], zero16), xs[n].astype(BF16))
                  for n, (m, dr) in enumerate(chains)]
            xs = [xs[n] - _dot(xs[n].astype(BF16), ys[n].astype(BF16)) for n in range(len(chains))]
        tmat = [x.astype(BF16) for x in xs]
        eg = [jnp.exp(x) for x in gc_col]
        g_last = [gc_col[n][c_len - 1:c_len] if dr == 0 else gc_col[n][0:1] for n, (m, dr) in enumerate(chains)]
        us = [_dot(tmat[n], (v[m] * beta[n]).astype(BF16)) for n, (m, dr) in enumerate(chains)]
        ws = [_dot(tmat[n], (k[m] * (beta[n] * eg[n])).astype(BF16)) for n, (m, dr) in enumerate(chains)]
        for n, (m, dr) in enumerate(chains):
            hh, c = units[m][0], cs[m]
            u_ref[dr, hh, rows[m], :] = us[n]
            wq_ref[dr, hh, pl.ds(pl.multiple_of(2 * c * c_len, c_len), c_len), :] = ws[n].astype(BF16)
            wq_ref[dr, hh, pl.ds(pl.multiple_of(2 * c * c_len + c_len, c_len), c_len), :] = (
                q[m] * eg[n]).astype(BF16)
            kd_ref[dr, hh, rows[m], :] = (k[m] * jnp.exp(g_last[n] - gc_col[n])).astype(BF16)
            in_ref[dr, hh, rows[m], :] = (qk[m] * decay[n]).astype(BF16)
            gl_ref[dr, hh, pl.ds(c, 1), :] = jnp.broadcast_to(jnp.exp(g_last[n]), (1, LANES))
        return carry

    lax.fori_loop(0, n_chunks // group, group_body, 0)

    seq_chains = [(dr, hh) for dr in range(2) for hh in range(nhs)]

    def step(i, states):
        cs = [i, n_chunks - 1 - i]
        rows = [pl.ds(pl.multiple_of(c * c_len, c_len), c_len) for c in cs]
        ws_qs = [_dot(wq_ref[dr, hh, pl.ds(pl.multiple_of(2 * cs[dr] * c_len, 2 * c_len), 2 * c_len), :],
                      states[n].astype(BF16)) for n, (dr, hh) in enumerate(seq_chains)]
        v16 = [(u_ref[dr, hh, rows[dr], :] - ws_qs[n][0:c_len]).astype(BF16)
               for n, (dr, hh) in enumerate(seq_chains)]
        intra = [_dot(in_ref[dr, hh, rows[dr], :], v16[n]) for n, (dr, hh) in enumerate(seq_chains)]
        upd = [_dot_tn(kd_ref[dr, hh, rows[dr], :], v16[n]) for n, (dr, hh) in enumerate(seq_chains)]
        for n, (dr, hh) in enumerate(seq_chains):
            acc_ref[hh, rows[dr], :] += ws_qs[n][c_len:2 * c_len] + intra[n]
        return tuple(states[n] * gl_ref[dr, hh, pl.ds(cs[dr], 1), :] + upd[n]
                     for n, (dr, hh) in enumerate(seq_chains))

    s_fin = lax.fori_loop(0, n_chunks, step, tuple(s0_ref[dr, hh] for dr, hh in seq_chains))
    for n, (dr, hh) in enumerate(seq_chains):
        sfin_ref[dr, hh] = s_fin[n]

    for hh in range(nhs):
        lanes = slice(hh * LANES, (hh + 1) * LANES)
        o = acc_ref[hh]
        y = o * lax.rsqrt(jnp.mean(o * o, axis=-1, keepdims=True) + EPS) * ng_ref[...]
        z = z_ref[:, lanes]
        o_ref[:, lanes] = y * (z * _sigmoid(z))


def _dn_call(proj, params, s0, s0_map, *, layer, batch, seq, nhs):
    nh = DN_HEADS
    nblk = nh // nhs
    width = nhs * LANES
    n_chunks = seq // DN_TILE
    vec = pl.BlockSpec((None, 1, LANES), lambda b, h: (layer, 0, 0))
    nsub = max(n_chunks, SUBLANES)
    return pl.pallas_call(
        functools.partial(_dn_kernel, seq=seq, nhs=nhs),
        grid=(batch, nblk),
        in_specs=[pl.BlockSpec((seq, width), lambda b, h: (b, COL_DQ * nblk + h)),
                  pl.BlockSpec((seq, width), lambda b, h: (b, (COL_DQ + 1) * nblk + h)),
                  pl.BlockSpec((seq, width), lambda b, h: (b, (COL_DQ + 2) * nblk + h)),
                  pl.BlockSpec((seq, width), lambda b, h: (b, COL_DZ * nblk + h)),
                  pl.BlockSpec((seq, LANES), lambda b, h: (b, COL_SC128)),
                  pl.BlockSpec((None, 3, CONV_W, width), lambda b, h: (layer, 0, 0, h)),
                  vec, vec, vec,
                  pl.BlockSpec((None, None, 2, nhs, DN_DK, DN_DK), s0_map)],
        out_specs=[pl.BlockSpec((seq, width), lambda b, h: (b, h)),
                   pl.BlockSpec((None, 2, nhs, DN_DK, DN_DK), lambda b, h: (b, 0, h, 0, 0))],
        out_shape=[jax.ShapeDtypeStruct((batch * seq, BRANCH_W), F32),
                   jax.ShapeDtypeStruct((batch, 2, nh, DN_DK, DN_DK), F32)],
        scratch_shapes=[pltpu.VMEM((seq + 2 * SUBLANES, LANES), F32),
                        pltpu.VMEM((nhs, seq, LANES), F32),
                        pltpu.VMEM((nhs, seq, LANES), F32),
                        pltpu.VMEM((nhs, seq, LANES), F32),
                        pltpu.VMEM((nhs, seq, LANES), F32),
                        pltpu.VMEM((2, nhs, seq, LANES), F32),
                        pltpu.VMEM((2, nhs, 2 * seq, LANES), BF16),
                        pltpu.VMEM((2, nhs, seq, LANES), BF16),
                        pltpu.VMEM((2, nhs, seq, DN_TILE), BF16),
                        pltpu.VMEM((2, nhs, nsub, LANES), F32),
                        pltpu.VMEM((nhs, seq, LANES), F32)],
        compiler_params=_params(2),
        name="deltanet",
    )(proj, proj, proj, proj, proj, params["dn_cw"], params["dn_alog"], params["dn_dtb"], params["dn_ng"], s0)


def _prepare_params(w_ffn_gate, w_ffn_up, w_ffn_down, w_in, lru_conv_w, lru_conv_b, lru_w_r, lru_b_r,
                    lru_w_i, lru_b_i, lru_lambda, gqa_q_norm, gqa_k_norm, na_rpb, dn_conv_w, dn_a_log,
                    dn_dt_bias, dn_norm_g, w_branch, w_out, norm_g):
    depth = w_in.shape[0]
    offs = np.cumsum((0,) + IN_WIDTHS)
    seg = [w_in[:, :, offs[i]:offs[i + 1]] for i in range(len(IN_WIDTHS))]
    (a_x, a_y, b_q, b_k, b_v, c_q, c_k, c_v, d_q, d_k, d_v, d_z, d_b, d_a, g_lin) = seg
    perm = np.asarray(GQA_PERM)
    b_q = b_q.reshape(depth, D_MODEL, GQA_HEADS, HEAD_DIM)[:, :, perm].reshape(depth, D_MODEL, BRANCH_W)
    pad = jnp.zeros((depth, D_MODEL, PROJ_W - 5120 - 2 * LANES - 16), F32)
    w_main = jnp.concatenate([a_x, a_y, b_q, c_q, c_k, c_v, d_q, d_k, d_v, d_z, b_k, b_v, d_b, d_a, pad],
                             axis=2).astype(BF16)

    def block_diag(wb):
        wb = wb.reshape(depth, 2, LRU_BLOCKS // 2, 2, LRU_BW, LRU_BW)
        z = jnp.zeros_like(wb[:, :, :, 0])
        return jnp.concatenate([jnp.concatenate([wb[:, :, :, 0], z], axis=-1),
                                jnp.concatenate([z, wb[:, :, :, 1]], axis=-1)], axis=-2)

    wr, wi = block_diag(lru_w_r), block_diag(lru_w_i)
    lru_wg = jnp.concatenate([wr[:, 0], wi[:, 0], wr[:, 1], wi[:, 1]], axis=-1).astype(BF16)
    ncol = BRANCH_W // LANES

    def col_blocks(v):
        return v.reshape(depth, ncol, LANES)

    lru_bg = jnp.concatenate([col_blocks(lru_b_r[:, 0]), col_blocks(lru_b_i[:, 0]),
                              col_blocks(lru_b_r[:, 1]), col_blocks(lru_b_i[:, 1])], axis=-1)[:, :, None, :]

    wb_b = w_branch[:, 1].reshape(depth, GQA_HEADS, HEAD_DIM, D_MODEL)[:, perm].reshape(depth, BRANCH_W, D_MODEL)
    wb = jnp.stack([w_branch[:, 0], wb_b, w_branch[:, 2], w_branch[:, 3]], axis=1).astype(BF16)

    lane_pad = jnp.zeros((depth, LANES - 4 * DN_HEADS), F32)
    lane_zero = jnp.zeros((depth, 2 * DN_HEADS), F32)
    alog = jnp.concatenate([lane_zero, dn_a_log.reshape(depth, -1), lane_pad], axis=1)[:, None, :]
    dtb = jnp.concatenate([lane_zero, dn_dt_bias.reshape(depth, -1), lane_pad], axis=1)[:, None, :]

    return dict(
        norm_g=norm_g[:, :, None, :],
        wg=w_ffn_gate.astype(BF16), wu=w_ffn_up.astype(BF16), wd=w_ffn_down.astype(BF16),
        w_main=w_main, w_gate=g_lin.astype(BF16),
        lru_cw=lru_conv_w, lru_cb=lru_conv_b[:, None, :], lru_wg=lru_wg, lru_bg=lru_bg, lru_lam=lru_lambda,
        gq=jnp.tile(gqa_q_norm, (1, 2))[:, None, :], gk=jnp.tile(gqa_k_norm, (1, 2))[:, None, :],
        na_bias=_na_bias_table(na_rpb),
        dn_cw=dn_conv_w.reshape(depth, CONV_W, 3, BRANCH_W).transpose(0, 2, 1, 3),
        dn_alog=alog, dn_dtb=dtb, dn_ng=dn_norm_g[:, None, :],
        wb=wb, w_out=w_out.astype(BF16),
    )


def _rope_tables(seq):
    pos = jnp.arange(seq)
    half = HEAD_DIM // 2
    inv = jnp.power(ROPE_BASE, -jnp.arange(0, half, 2, dtype=F32) / half)
    ang_r = (pos // GRID_W).astype(F32)[:, None] * inv[None, :]
    ang_c = (pos % GRID_W).astype(F32)[:, None] * inv[None, :]
    cos = jnp.concatenate([jnp.cos(ang_r)] * 2 + [jnp.cos(ang_c)] * 2, axis=-1)
    sin = jnp.concatenate([-jnp.sin(ang_r), jnp.sin(ang_r), -jnp.sin(ang_c), jnp.sin(ang_c)], axis=-1)
    return jnp.tile(cos, (1, 2)), jnp.tile(sin, (1, 2))


def _na_bias_table(rpb):
    qc = np.arange(GRID_W)
    cs = np.clip(qc - NA_WIN_C // 2, 0, GRID_W - NA_WIN_C)
    kc = np.arange(GRID_W)
    inwin = (kc[None, :] >= cs[:, None]) & (kc[None, :] < cs[:, None] + NA_WIN_C)
    coff = kc[None, :] - qc[:, None] + NA_WIN_C - 1
    onehot = (coff[None] == np.arange(2 * NA_WIN_C - 1)[:, None, None]).astype(np.float32)
    t = jnp.einsum("lhrd,dqk->lhrqk", rpb.astype(F32), onehot, precision=lax.Precision.HIGHEST)
    t = jnp.where(inwin, t, NEG_BIG)
    return jnp.concatenate([t[:, :, :-1], t[:, :, 1:]], axis=-1)


def _layer(x, mod_all, params, *, batch, seq, latent, layer, caches, tables, final_g, tm):
    row0 = 1 if latent else 0
    x = _ffn_call(x, mod_all, params, final_g, layer=layer, which=0, row0=row0, final=False, tm=tm)
    proj = _inproj_call(x, mod_all, params, layer=layer, row0=row0, tm=tm)

    dn_heads_per_step = 1 if seq > 1024 else DN_HEADS
    if latent:
        cache_ak, cache_av, cache_nk, cache_nv, state_lru, state_delta = caches
        h0, h0_map = state_lru, lambda b, c: (b, layer, 0, c)
        s0, s0_map = state_delta, lambda b, h: (b, layer, 0, h, 0, 0)
    else:
        h0, h0_map = jnp.zeros((1, 1, 2, BRANCH_W), F32), lambda b, c: (0, 0, 0, c)
        s0 = jnp.zeros((1, 1, 2, dn_heads_per_step, DN_DK, DN_DK), F32)
        s0_map = lambda b, h: (0, 0, 0, 0, 0, 0)

    o_a, lru_fin = _lru_call(proj, h0, h0_map, params, layer=layer, batch=batch, seq=seq)

    cos, sin = tables["rope"] if latent else (None, None)
    qn, kn = _prep_call(proj, params, cos, sin, layer=layer, batch=batch, seq=seq, rope=latent)
    tq = min(seq, 256)
    kv_new = (kn, pl.BlockSpec((seq, LANES), lambda b, i: (b, 0)),
              proj, pl.BlockSpec((seq, LANES), lambda b, i: (b, COL_BV128)))
    if latent:
        gqa_cache = pl.BlockSpec((None, None, PAST_LEN, LANES), lambda b, i: (b, layer, 0, 0))
        sources = [kv_new, (cache_ak, gqa_cache, cache_av, gqa_cache)]
    else:
        sources = [kv_new]
    o_b = _attn_call(qn, 0, sources, batch=batch, seq=seq, tq=tq, nkb=1, qscale=1.0)

    if latent:
        o_c = _na_call(proj, cache_nk, cache_nv, params, layer=layer, batch=batch, seq=seq)
    else:
        src = (proj, pl.BlockSpec((seq, BRANCH_W), lambda b, i: (b, COL_CK)),
               proj, pl.BlockSpec((seq, BRANCH_W), lambda b, i: (b, COL_CV)))
        o_c = _attn_call(proj, COL_CQ, [src], batch=batch, seq=seq, tq=tq, nkb=4, qscale=HEAD_DIM ** -0.5)

    o_d, dn_fin = _dn_call(proj, params, s0, s0_map, layer=layer, batch=batch, seq=seq, nhs=dn_heads_per_step)

    x = _merge_call(x, mod_all, params, (o_a, o_b, o_c, o_d), layer=layer, row0=row0, tm=tm)
    x = _ffn_call(x, mod_all, params, final_g, layer=layer, which=1, row0=row0,
                  final=(layer == DEPTH - 1), tm=tm)

    new_ctx = None
    if not latent:
        new_ctx = (kn.reshape(batch, seq, GQA_KV, HEAD_DIM),
                   proj[:, COL_BV128 * LANES:(COL_BV128 + 1) * LANES].reshape(batch, seq, GQA_KV, HEAD_DIM),
                   proj[:, COL_CK * BRANCH_W:(COL_CK + 1) * BRANCH_W].reshape(batch, seq, NA_HEADS, HEAD_DIM),
                   proj[:, COL_CV * BRANCH_W:(COL_CV + 1) * BRANCH_W].reshape(batch, seq, NA_HEADS, HEAD_DIM),
                   lru_fin, dn_fin)
    return x, new_ctx


def kernel(x_prompt, x_sample, c, cache_attn_k, cache_attn_v, cache_na_k, cache_na_v, state_lru, state_delta, c_ctx, w_mod, b_mod, norm_g, w_ffn_gate, w_ffn_up, w_ffn_down, w_in, lru_conv_w, lru_conv_b, lru_w_r, lru_b_r, lru_w_i, lru_b_i, lru_lambda, gqa_q_norm, gqa_k_norm, na_rpb, dn_conv_w, dn_a_log, dn_dt_bias, dn_norm_g, w_branch, w_out, final_norm_g):
    batch_c, seq_c, _ = x_prompt.shape
    batch_l, seq_l, _ = x_sample.shape
    assert batch_l + 1 <= SUBLANES

    cs = jnp.concatenate([c_ctx[None, :], c, jnp.zeros((SUBLANES - 1 - batch_l, D_MODEL), F32)], axis=0)
    mod_all = _mod_call(cs, w_mod, b_mod).reshape(DEPTH, SUBLANES, N_MOD, D_MODEL)
    params = _prepare_params(w_ffn_gate, w_ffn_up, w_ffn_down, w_in, lru_conv_w, lru_conv_b, lru_w_r, lru_b_r,
                             lru_w_i, lru_b_i, lru_lambda, gqa_q_norm, gqa_k_norm, na_rpb, dn_conv_w, dn_a_log,
                             dn_dt_bias, dn_norm_g, w_branch, w_out, norm_g)

    caches = (cache_attn_k.reshape(batch_l, DEPTH, PAST_LEN, GQA_KV * HEAD_DIM),
              cache_attn_v.reshape(batch_l, DEPTH, PAST_LEN, GQA_KV * HEAD_DIM),
              cache_na_k.reshape(batch_l, DEPTH, PAST_LEN, BRANCH_W),
              cache_na_v.reshape(batch_l, DEPTH, PAST_LEN, BRANCH_W),
              state_lru, state_delta)
    tables = {"rope": _rope_tables(seq_l)}
    final_g = final_norm_g[None, :]

    xc = x_prompt.reshape(1, batch_c * seq_c, D_MODEL)
    xl = x_sample
    ctx_out = []
    for l in range(DEPTH):
        xc, new_ctx = _layer(xc, mod_all, params, batch=batch_c, seq=seq_c, latent=False, layer=l, caches=None,
                             tables=tables, final_g=final_g, tm=256)
        ctx_out.append(new_ctx)
        xl, _ = _layer(xl, mod_all, params, batch=batch_l, seq=seq_l, latent=True, layer=l, caches=caches,
                       tables=tables, final_g=final_g, tm=256)

    stacked = [jnp.stack([ctx_out[l][i] for l in range(DEPTH)], axis=1) for i in range(6)]
    return (xc.reshape(batch_c, seq_c, D_MODEL), xl, *stacked)
```

```python
import functools

import numpy as np
import jax
import jax.numpy as jnp
from jax import lax
from jax.experimental import pallas as pl
from jax.experimental.pallas import tpu as pltpu

F32 = jnp.float32
BF16 = jnp.bfloat16

D_MODEL = 1024
DEPTH = 4
GRID_W = 64
N_BRANCH = 4
BRANCH_W = 512
N_MOD = 9
D_FF = 2816
EPS = 1e-6
CONV_W = 4
LRU_BLOCKS = 8
LRU_BW = 64
LRU_C = 8.0
HEAD_DIM = 64
GQA_HEADS = 8
GQA_KV = 2
ROPE_BASE = 10000.0
NA_HEADS = 8
NA_WIN_R = 8
NA_WIN_C = 16
NA_QROWS = 4
NA_KROWS = 12
DN_DK = 128
DN_HEADS = 4
LRU_PITCH_PAD = 8
DN_TILE = 128
DN_CHAINS = 8
PAST_LEN = 512
IN_WIDTHS = (512, 512, 512, 128, 128, 512, 512, 512, 512, 512, 512, 512, 8, 8, 4096)

LANES = 128
SUBLANES = 8
DENSE_ROWS = 512
PROJ_W = 5632
COL_BQ, COL_CQ, COL_CK, COL_CV, COL_DQ, COL_DZ = 2, 3, 4, 5, 6, 9
COL_BK128, COL_BV128, COL_SC128 = 40, 41, 42
VMEM_LIMIT = 56 * 1024 * 1024
NEG_BIG = -1e30
GQA_PERM = (0, 4, 1, 5, 2, 6, 3, 7)


def _params(n):
    return pltpu.CompilerParams(dimension_semantics=("arbitrary",) * n, vmem_limit_bytes=VMEM_LIMIT)


def _const_spec(shape):
    nd = len(shape)
    return pl.BlockSpec(shape, lambda *_: (0,) * nd, pipeline_mode=pl.Buffered(1))


def _layer_spec(arr, layer, *sub):
    tail = arr.shape[1 + len(sub):]
    index = (layer,) + tuple(sub) + (0,) * len(tail)
    return pl.BlockSpec((None,) * (1 + len(sub)) + tuple(tail), lambda *_: index, pipeline_mode=pl.Buffered(1))


def _mod_spec(layer, row0):
    return pl.BlockSpec((None, None, N_MOD, D_MODEL), lambda b, i: (layer, row0 + b, 0, 0))


def _dot(a, b):
    return jnp.dot(a, b, preferred_element_type=F32)


def _dot_nt(a, b):
    return lax.dot_general(a, b, (((1,), (1,)), ((), ())), preferred_element_type=F32)


def _dot_tn(a, b):
    return lax.dot_general(a, b, (((0,), (0,)), ((), ())), preferred_element_type=F32)


def _split(x):
    hi = x.astype(BF16)
    lo = (x - hi.astype(F32)).astype(BF16)
    return hi, lo


def _dot3(a, b):
    ah, al = _split(a)
    bh, bl = _split(b)
    return _dot(ah, bh) + (_dot(al, bh) + _dot(ah, bl))


def _sigmoid(x):
    return 0.5 * jnp.tanh(0.5 * x) + 0.5


def _modnorm(x, g, shift, scale):
    ms = jnp.mean(x * x, axis=-1, keepdims=True)
    return (x * lax.rsqrt(ms + EPS) * g) * (1.0 + scale) + shift


def _mod_kernel(c_ref, w_ref, b_ref, o_ref):
    c = c_ref[...]
    o_ref[0] = _dot3(c * _sigmoid(c), w_ref[0]) + b_ref[0]


def _mod_call(cs, w_mod, b_mod):
    tn = 1024
    n = N_MOD * D_MODEL
    return pl.pallas_call(
        _mod_kernel,
        grid=(DEPTH, n // tn),
        in_specs=[pl.BlockSpec((SUBLANES, D_MODEL), lambda l, j: (0, 0)),
                  pl.BlockSpec((1, D_MODEL, tn), lambda l, j: (l, 0, j)),
                  pl.BlockSpec((1, 1, tn), lambda l, j: (l, 0, j))],
        out_specs=pl.BlockSpec((1, SUBLANES, tn), lambda l, j: (l, 0, j)),
        out_shape=jax.ShapeDtypeStruct((DEPTH, SUBLANES, n), F32),
        compiler_params=_params(2),
        name="mod",
    )(cs, w_mod, b_mod.reshape(DEPTH, 1, n))


def _ffn_kernel(x_ref, mod_ref, g_ref, wg_ref, wu_ref, wd_ref, gf_ref, o_ref, *, mi, final):
    x = x_ref[0]
    mod = mod_ref[...]
    h = _modnorm(x, g_ref[...], mod[mi:mi + 1], mod[mi + 1:mi + 2]).astype(BF16)
    gt = _dot(h, wg_ref[...])
    up = _dot(h, wu_ref[...])
    a = (gt * _sigmoid(gt) * up).astype(BF16)
    y = x + 0.5 * mod[mi + 2:mi + 3] * _dot(a, wd_ref[...])
    if final:
        ms = jnp.mean(y * y, axis=-1, keepdims=True)
        y = y * lax.rsqrt(ms + EPS) * gf_ref[...]
    o_ref[0] = y


def _ffn_call(x, mod_all, params, gf, *, layer, which, row0, final, tm):
    nb, rows, _ = x.shape
    return pl.pallas_call(
        functools.partial(_ffn_kernel, mi=6 * which, final=final),
        grid=(nb, rows // tm),
        in_specs=[pl.BlockSpec((1, tm, D_MODEL), lambda b, i: (b, i, 0)),
                  _mod_spec(layer, row0),
                  _layer_spec(params["norm_g"], layer, 2 * which),
                  _layer_spec(params["wg"], layer, which),
                  _layer_spec(params["wu"], layer, which),
                  _layer_spec(params["wd"], layer, which),
                  _const_spec((1, D_MODEL))],
        out_specs=pl.BlockSpec((1, tm, D_MODEL), lambda b, i: (b, i, 0)),
        out_shape=jax.ShapeDtypeStruct(x.shape, F32),
        compiler_params=_params(2),
        name="ffn",
    )(x, mod_all, params["norm_g"], params["wg"], params["wu"], params["wd"], gf)


def _inproj_kernel(x_ref, mod_ref, g_ref, w_ref, o_ref):
    mod = mod_ref[...]
    h = _modnorm(x_ref[0], g_ref[...], mod[3:4], mod[4:5]).astype(BF16)
    o_ref[...] = _dot(h, w_ref[...])


def _inproj_call(x, mod_all, params, *, layer, row0, tm):
    nb, rows, _ = x.shape
    nt = rows // tm
    return pl.pallas_call(
        _inproj_kernel,
        grid=(nb, nt),
        in_specs=[pl.BlockSpec((1, tm, D_MODEL), lambda b, i: (b, i, 0)),
                  _mod_spec(layer, row0),
                  _layer_spec(params["norm_g"], layer, 1),
                  _layer_spec(params["w_main"], layer)],
        out_specs=pl.BlockSpec((tm, PROJ_W), lambda b, i: (b * nt + i, 0)),
        out_shape=jax.ShapeDtypeStruct((nb * rows, PROJ_W), F32),
        compiler_params=_params(2),
        name="inproj",
    )(x, mod_all, params["norm_g"], params["w_main"])


def _merge_kernel(x_ref, mod_ref, g_ref, oa_ref, ob_ref, oc_ref, od_ref, wgate_ref, wb_ref, wout_ref, o_ref):
    x = x_ref[0]
    mod = mod_ref[...]
    h = _modnorm(x, g_ref[...], mod[3:4], mod[4:5]).astype(BF16)
    acc = None
    for n, ref in enumerate((oa_ref, ob_ref, oc_ref, od_ref)):
        gate = _sigmoid(_dot(h, wgate_ref[:, n * D_MODEL:(n + 1) * D_MODEL]))
        term = gate * _dot(ref[...].astype(BF16), wb_ref[n])
        acc = term if acc is None else acc + term
    o_ref[0] = x + mod[5:6] * _dot(acc.astype(BF16), wout_ref[...])


def _merge_call(x, mod_all, params, outs, *, layer, row0, tm):
    nb, rows, _ = x.shape
    nt = rows // tm
    ospec = pl.BlockSpec((tm, BRANCH_W), lambda b, i: (b * nt + i, 0))
    return pl.pallas_call(
        _merge_kernel,
        grid=(nb, nt),
        in_specs=[pl.BlockSpec((1, tm, D_MODEL), lambda b, i: (b, i, 0)),
                  _mod_spec(layer, row0),
                  _layer_spec(params["norm_g"], layer, 1),
                  ospec, ospec, ospec, ospec,
                  _layer_spec(params["w_gate"], layer),
                  _layer_spec(params["wb"], layer),
                  _layer_spec(params["w_out"], layer)],
        out_specs=pl.BlockSpec((1, tm, D_MODEL), lambda b, i: (b, i, 0)),
        out_shape=jax.ShapeDtypeStruct(x.shape, F32),
        compiler_params=_params(2),
        name="merge",
    )(x, mod_all, params["norm_g"], *outs, params["w_gate"], params["wb"], params["w_out"])


def _log_sigmoid(x):
    return jnp.minimum(x, 0.0) - jnp.log1p(jnp.exp(-jnp.abs(x)))


def _gelu_tanh(x):
    return x * (0.5 * (1.0 + jnp.tanh(0.7978845608028654 * (x + 0.044715 * (x * x * x)))))


def _lru_kernel(ax_ref, ay_ref, h0_ref, cw_ref, cb_ref, wg_ref, bg_ref, lam_ref, o_ref, fin_ref,
                xp_ref, a_ref, u_ref, h_ref, p_ref, *, seq):
    lc = seq // SUBLANES
    pitch = lc + LRU_PITCH_PAD
    zero8 = jnp.zeros((SUBLANES, LANES), F32)
    xp_ref[0:SUBLANES, :] = zero8
    xp_ref[SUBLANES + seq:2 * SUBLANES + seq, :] = zero8
    xp_ref[SUBLANES:SUBLANES + seq, :] = ax_ref[...]
    cw = cw_ref[...]
    xa = cb_ref[...] + cw[0:1] * xp_ref[pl.ds(SUBLANES - 2, seq), :]
    for k in range(1, CONV_W):
        xa = xa + cw[k:k + 1] * xp_ref[pl.ds(SUBLANES - 2 + k, seq), :]
    gates = _dot(xa.astype(BF16), wg_ref[...]) + bg_ref[...]
    lam = lam_ref[...]
    for dr in range(2):
        r = _sigmoid(gates[:, (2 * dr) * LANES:(2 * dr + 1) * LANES])
        i = _sigmoid(gates[:, (2 * dr + 1) * LANES:(2 * dr + 2) * LANES])
        log_a = (LRU_C * r) * _log_sigmoid(lam[dr:dr + 1])
        a = jnp.exp(log_a)
        one_m_a2 = -jnp.tanh(log_a) * (a * a + 1.0)
        root = jnp.where(one_m_a2 > 0.0, one_m_a2 * lax.rsqrt(one_m_a2), 0.0)
        u = root * (i * xa)
        for k in range(SUBLANES):
            a_ref[dr, pl.ds(k * pitch, lc), :] = a[k * lc:(k + 1) * lc]
            u_ref[dr, pl.ds(k * pitch, lc), :] = u[k * lc:(k + 1) * lc]

    sub = lax.broadcasted_iota(jnp.int32, (SUBLANES, LANES), 0)
    h0 = h0_ref[...]
    hf0 = jnp.where(sub == 0, h0[0:1], 0.0)
    hb0 = jnp.where(sub == SUBLANES - 1, h0[1:2], 0.0)
    ones = jnp.ones((SUBLANES, LANES), F32)

    def body(s, carry):
        hf, pf, hb, pb = carry
        rows_f = pl.ds(s, SUBLANES, stride=pitch)
        rows_b = pl.ds(lc - 1 - s, SUBLANES, stride=pitch)
        af = a_ref[0, rows_f, :]
        hf = af * hf + u_ref[0, rows_f, :]
        pf = af * pf
        h_ref[0, rows_f, :] = hf
        p_ref[0, rows_f, :] = pf
        ab = a_ref[1, rows_b, :]
        hb = ab * hb + u_ref[1, rows_b, :]
        pb = ab * pb
        h_ref[1, rows_b, :] = hb
        p_ref[1, rows_b, :] = pb
        return hf, pf, hb, pb

    hf, pf, hb, pb = lax.fori_loop(0, lc, body, (hf0, ones, hb0, ones), unroll=8)

    cf = [jnp.zeros((1, LANES), F32)]
    for k in range(1, SUBLANES):
        cf.append(hf[k - 1:k] + pf[k - 1:k] * cf[k - 1])
    fin_ref[0:1, :] = hf[SUBLANES - 1:SUBLANES] + pf[SUBLANES - 1:SUBLANES] * cf[SUBLANES - 1]
    cb = [None] * SUBLANES
    cb[SUBLANES - 1] = jnp.zeros((1, LANES), F32)
    for k in range(SUBLANES - 2, -1, -1):
        cb[k] = hb[k + 1:k + 2] + pb[k + 1:k + 2] * cb[k + 1]
    fin_ref[1:2, :] = hb[0:1] + pb[0:1] * cb[0]

    for k in range(SUBLANES):
        rows = pl.ds(k * lc, lc)
        held = pl.ds(k * pitch, lc)
        h = (h_ref[0, held, :] + p_ref[0, held, :] * cf[k]) + (h_ref[1, held, :] + p_ref[1, held, :] * cb[k])
        o_ref[rows, :] = h * _gelu_tanh(ay_ref[rows, :])


def _lru_call(proj, h0, h0_map, params, *, layer, batch, seq):
    ncol = BRANCH_W // LANES
    pad_rows = SUBLANES * LRU_PITCH_PAD
    return pl.pallas_call(
        functools.partial(_lru_kernel, seq=seq),
        grid=(batch, ncol),
        in_specs=[pl.BlockSpec((seq, LANES), lambda b, c: (b, c)),
                  pl.BlockSpec((seq, LANES), lambda b, c: (b, ncol + c)),
                  pl.BlockSpec((None, None, 2, LANES), h0_map),
                  pl.BlockSpec((None, CONV_W, LANES), lambda b, c: (layer, 0, c)),
                  pl.BlockSpec((None, 1, LANES), lambda b, c: (layer, 0, c)),
                  pl.BlockSpec((None, None, LANES, 4 * LANES), lambda b, c: (layer, c, 0, 0)),
                  pl.BlockSpec((None, None, 1, 4 * LANES), lambda b, c: (layer, c, 0, 0)),
                  pl.BlockSpec((None, 2, LANES), lambda b, c: (layer, 0, c))],
        out_specs=[pl.BlockSpec((seq, LANES), lambda b, c: (b, c)),
                   pl.BlockSpec((None, 2, LANES), lambda b, c: (b, 0, c))],
        out_shape=[jax.ShapeDtypeStruct((batch * seq, BRANCH_W), F32),
                   jax.ShapeDtypeStruct((batch, 2, BRANCH_W), F32)],
        scratch_shapes=[pltpu.VMEM((seq + 2 * SUBLANES, LANES), F32),
                        pltpu.VMEM((2, seq + pad_rows, LANES), F32),
                        pltpu.VMEM((2, seq + pad_rows, LANES), F32),
                        pltpu.VMEM((2, seq + pad_rows, LANES), F32),
                        pltpu.VMEM((2, seq + pad_rows, LANES), F32)],
        compiler_params=_params(2),
        name="lru",
    )(proj, proj, h0, params["lru_cw"], params["lru_cb"], params["lru_wg"], params["lru_bg"], params["lru_lam"])


def _prep_kernel(*refs, rope):
    if rope:
        q_ref, kv_ref, gq_ref, gk_ref, cos_ref, sin_ref, qn_ref, kn_ref = refs
    else:
        q_ref, kv_ref, gq_ref, gk_ref, qn_ref, kn_ref = refs
    rows = q_ref.shape[0]
    lane = lax.broadcasted_iota(jnp.int32, (rows, LANES), 1)
    lo = lane < HEAD_DIM
    first16 = (lane & 16) == 0

    def head_norm(x, g):
        sq = x * x
        s_lo = jnp.sum(jnp.where(lo, sq, 0.0), axis=-1, keepdims=True)
        s_hi = jnp.sum(jnp.where(lo, 0.0, sq), axis=-1, keepdims=True)
        ms = jnp.where(lo, s_lo, s_hi) * (1.0 / HEAD_DIM)
        return x * lax.rsqrt(ms + EPS) * g

    def rotate(y):
        if not rope:
            return y
        partner = jnp.where(first16, pltpu.roll(y, LANES - 16, 1), pltpu.roll(y, 16, 1))
        return y * cos_ref[...] + partner * sin_ref[...]

    for p in range(BRANCH_W // LANES):
        cols = slice(p * LANES, (p + 1) * LANES)
        qn_ref[:, cols] = rotate(head_norm(q_ref[:, cols], gq_ref[...])) * (HEAD_DIM ** -0.5)
    kn_ref[...] = rotate(head_norm(kv_ref[:, 0:LANES], gk_ref[...]))


def _prep_call(proj, params, cos, sin, *, layer, batch, seq, rope):
    tab = pl.BlockSpec((seq, LANES), lambda b: (0, 0))
    in_specs = [pl.BlockSpec((seq, BRANCH_W), lambda b: (b, COL_BQ)),
                pl.BlockSpec((seq, BRANCH_W), lambda b: (b, COL_BK128 // 4)),
                _layer_spec(params["gq"], layer), _layer_spec(params["gk"], layer)]
    args = [proj, proj, params["gq"], params["gk"]]
    if rope:
        in_specs += [tab, tab]
        args += [cos, sin]
    return pl.pallas_call(
        functools.partial(_prep_kernel, rope=rope),
        grid=(batch,),
        in_specs=in_specs,
        out_specs=[pl.BlockSpec((seq, BRANCH_W), lambda b: (b, 0)),
                   pl.BlockSpec((seq, LANES), lambda b: (b, 0))],
        out_shape=[jax.ShapeDtypeStruct((batch * seq, BRANCH_W), F32),
                   jax.ShapeDtypeStruct((batch * seq, LANES), F32)],
        compiler_params=_params(1),
        name="attn_prep",
    )(*args)


def _softmax_pv(scores, values):
    m = None
    for s in scores:
        sm = jnp.max(s, axis=-1, keepdims=True)
        m = sm if m is None else jnp.maximum(m, sm)
    den = None
    out = None
    for s, v in zip(scores, values):
        p = jnp.exp(s - m)
        ps = jnp.sum(p, axis=-1, keepdims=True)
        den = ps if den is None else den + ps
        o = _dot(p.astype(BF16), v)
        out = o if out is None else out + o
    return out / den


def _attn_kernel(*refs, nsrc, nkb, qscale):
    q_ref = refs[0]
    src = refs[1:1 + 2 * nsrc]
    o_ref = refs[1 + 2 * nsrc]
    tq = q_ref.shape[0]
    lo = lax.broadcasted_iota(jnp.int32, (tq, LANES), 1) < HEAD_DIM

    loaded = {}

    def kv(p):
        col = p if nkb != 1 else 0
        if col not in loaded:
            cols = slice(col * LANES, (col + 1) * LANES)
            loaded[col] = ([src[2 * i][:, cols].astype(BF16) for i in range(nsrc)],
                           [src[2 * i + 1][:, cols].astype(BF16) for i in range(nsrc)])
        return loaded[col]

    def scores(unit):
        p, half = unit
        qb = q_ref[:, p * LANES:(p + 1) * LANES]
        if qscale != 1.0:
            qb = qb * qscale
        qm = jnp.where(lo if half == 0 else jnp.logical_not(lo), qb, 0.0).astype(BF16)
        return [_dot_nt(qm, k) for k in kv(p)[0]]

    units = [(p, half) for p in range(BRANCH_W // LANES) for half in range(2)]
    pending = scores(units[0])
    first_half = None
    for n, (p, half) in enumerate(units):
        current = pending
        if n + 1 < len(units):
            pending = scores(units[n + 1])
        out = _softmax_pv(current, kv(p)[1])
        if half == 0:
            first_half = out
        else:
            o_ref[:, p * LANES:(p + 1) * LANES] = jnp.where(lo, first_half, out)


def _attn_call(q_arr, q_col, sources, *, batch, seq, tq, nkb, qscale):
    nq = seq // tq
    in_specs = [pl.BlockSpec((tq, BRANCH_W), lambda b, i: (b * nq + i, q_col))]
    args = [q_arr]
    for k_arr, k_spec, v_arr, v_spec in sources:
        in_specs += [k_spec, v_spec]
        args += [k_arr, v_arr]
    return pl.pallas_call(
        functools.partial(_attn_kernel, nsrc=len(sources), nkb=nkb, qscale=qscale),
        grid=(batch, nq),
        in_specs=in_specs,
        out_specs=pl.BlockSpec((tq, BRANCH_W), lambda b, i: (b * nq + i, 0)),
        out_shape=jax.ShapeDtypeStruct((batch * seq, BRANCH_W), F32),
        compiler_params=_params(2),
        name="attn",
    )(*args)


def _na_row_start(r, rows):
    return jnp.clip(r - NA_WIN_R // 2, 0, rows - NA_WIN_R)


def _na_kernel(q_ref, k_ref, v_ref, ck_ref, cv_ref, bias_ref, o_ref, *, rows):
    r0 = pl.program_id(1) * NA_QROWS
    w0 = jnp.clip(r0 - NA_WIN_R // 2, 0, rows - NA_KROWS)
    win = pl.ds(pl.multiple_of(w0 * GRID_W, GRID_W), NA_KROWS * GRID_W)
    lo = lax.broadcasted_iota(jnp.int32, (NA_QROWS * GRID_W, LANES), 1) < HEAD_DIM
    lo_row = lax.broadcasted_iota(jnp.int32, (GRID_W, LANES), 1) < HEAD_DIM
    ncol = BRANCH_W // LANES
    scores = []
    for p in range(ncol):
        cols = slice(p * LANES, (p + 1) * LANES)
        qb = q_ref[:, cols] * (HEAD_DIM ** -0.5)
        kw = k_ref[win, cols].astype(BF16)
        kc = ck_ref[:, cols].astype(BF16)
        for half in range(2):
            qm = jnp.where(lo if half == 0 else jnp.logical_not(lo), qb, 0.0).astype(BF16)
            scores.append((_dot_nt(qm, kw), _dot_nt(qm, kc)))
    pair_index, pair_mask = [], []
    for j in range(NA_QROWS):
        r = r0 + j
        rs = _na_row_start(r, rows)
        idx_j, mask_j = [], []
        for t in range(NA_KROWS // 2):
            kr = w0 + 2 * t
            rel = kr - r + NA_WIN_R - 1
            idx_j.append(jnp.clip(rel + 1, 0, 2 * NA_WIN_R - 1))
            in_a = jnp.where((kr >= rs) & (kr < rs + NA_WIN_R), 0.0, NEG_BIG)
            in_b = jnp.where((kr + 1 >= rs) & (kr + 1 < rs + NA_WIN_R), 0.0, NEG_BIG)
            mask_j.append(jnp.where(lo_row, in_a, in_b))
        pair_index.append(idx_j)
        pair_mask.append(mask_j)
    probs = []
    for h, (s_loc, s_ctx) in enumerate(scores):
        bias = jnp.concatenate(
            [jnp.concatenate([bias_ref[h, pair_index[j][t]] + pair_mask[j][t] for t in range(NA_KROWS // 2)],
                             axis=-1) for j in range(NA_QROWS)], axis=0)
        s_loc = s_loc + bias
        m = jnp.maximum(jnp.max(s_loc, axis=-1, keepdims=True), jnp.max(s_ctx, axis=-1, keepdims=True))
        p_loc = jnp.exp(s_loc - m)
        p_ctx = jnp.exp(s_ctx - m)
        den = jnp.sum(p_loc, axis=-1, keepdims=True) + jnp.sum(p_ctx, axis=-1, keepdims=True)
        probs.append((p_loc.astype(BF16), p_ctx.astype(BF16), den))
    for p in range(ncol):
        cols = slice(p * LANES, (p + 1) * LANES)
        vw = v_ref[win, cols].astype(BF16)
        vc = cv_ref[:, cols].astype(BF16)
        halves = []
        for half in range(2):
            p_loc, p_ctx, den = probs[2 * p + half]
            halves.append((_dot(p_loc, vw) + _dot(p_ctx, vc)) / den)
        o_ref[:, cols] = jnp.where(lo, halves[0], halves[1])


def _na_call(proj, cache_k, cache_v, params, *, layer, batch, seq):
    rows = seq // GRID_W
    steps = rows // NA_QROWS
    qrows = NA_QROWS * GRID_W
    cache_spec = pl.BlockSpec((None, None, PAST_LEN, BRANCH_W), lambda b, r: (b, layer, 0, 0))
    return pl.pallas_call(
        functools.partial(_na_kernel, rows=rows),
        grid=(batch, steps),
        in_specs=[pl.BlockSpec((qrows, BRANCH_W), lambda b, r: (b * steps + r, COL_CQ)),
                  pl.BlockSpec((seq, BRANCH_W), lambda b, r: (b, COL_CK)),
                  pl.BlockSpec((seq, BRANCH_W), lambda b, r: (b, COL_CV)),
                  cache_spec, cache_spec,
                  _layer_spec(params["na_bias"], layer)],
        out_specs=pl.BlockSpec((qrows, BRANCH_W), lambda b, r: (b * steps + r, 0)),
        out_shape=jax.ShapeDtypeStruct((batch * seq, BRANCH_W), F32),
        compiler_params=_params(2),
        name="na",
    )(proj, proj, proj, cache_k, cache_v, params["na_bias"])


def _split3(x):
    hi = x.astype(BF16)
    r = x - hi.astype(F32)
    mid = r.astype(BF16)
    lo = (r - mid.astype(F32)).astype(BF16)
    return hi, mid, lo


def _dn_kernel(q_ref, k_ref, v_ref, z_ref, sl_ref, cw_ref, alog_ref, dtb_ref, ng_ref, s0_ref,
               o_ref, sfin_ref,
               xp_ref, qn_ref, kn_ref, vn_ref, col_ref, u_ref, wq_ref, kd_ref, in_ref, gl_ref, acc_ref,
               *, seq, nhs):
    c_len = DN_TILE
    n_chunks = seq // c_len

    zero8 = jnp.zeros((SUBLANES, LANES), F32)
    xp_ref[0:SUBLANES, :] = zero8
    xp_ref[SUBLANES + seq:2 * SUBLANES + seq, :] = zero8
    for hh in range(nhs):
        lanes = slice(hh * LANES, (hh + 1) * LANES)
        for j, (src, dst) in enumerate(((q_ref, qn_ref), (k_ref, kn_ref), (v_ref, vn_ref))):
            xp_ref[SUBLANES:SUBLANES + seq, :] = src[:, lanes]
            cw = cw_ref[j][:, lanes]
            y = cw[0:1] * xp_ref[pl.ds(SUBLANES - 2, seq), :]
            for t in range(1, CONV_W):
                y = y + cw[t:t + 1] * xp_ref[pl.ds(SUBLANES - 2 + t, seq), :]
            y = y * _sigmoid(y)
            if j < 2:
                y = y * lax.rsqrt(jnp.sum(y * y, axis=-1, keepdims=True) + EPS)
            if j == 0:
                y = y * (DN_DK ** -0.5)
            dst[hh] = y

    sl = sl_ref[...]
    lane = lax.broadcasted_iota(jnp.int32, (seq, LANES), 1)
    beta_all = _sigmoid(sl)
    xs = sl + dtb_ref[...]
    softplus = jnp.maximum(xs, 0.0) + jnp.log1p(jnp.exp(-jnp.abs(xs)))
    g_all = -jnp.exp(alog_ref[...]) * softplus
    for hh in range(nhs):
        head = pl.program_id(1) * nhs + hh
        cols = jnp.zeros((seq, LANES), F32)
        for dr in range(2):
            beta = jnp.sum(jnp.where(lane == dr * DN_HEADS + head, beta_all, 0.0), axis=-1, keepdims=True)
            g = jnp.sum(jnp.where(lane == 2 * DN_HEADS + dr * DN_HEADS + head, g_all, 0.0),
                        axis=-1, keepdims=True)
            cols = jnp.where(lane == dr, beta, cols)
            cols = jnp.where(lane == 2 + dr, g, cols)
        col_ref[hh] = cols
        acc_ref[hh] = jnp.zeros((seq, LANES), F32)

    ii = lax.broadcasted_iota(jnp.int32, (c_len, c_len), 0)
    jj = lax.broadcasted_iota(jnp.int32, (c_len, c_len), 1)
    eye = jnp.where(ii == jj, 1.0, 0.0).astype(F32)
    causal = (ii >= jj, ii <= jj)
    strict = (ii > jj, ii < jj)
    tri16 = tuple(jnp.where(m, 1.0, 0.0).astype(BF16) for m in causal)
    level_masks = []
    for dr in range(2):
        hi_idx, lo_idx = (ii, jj) if dr == 0 else (jj, ii)
        masks = []
        lvl = 0
        while (1 << lvl) < c_len:
            masks.append(((hi_idx >> (lvl + 1)) == (lo_idx >> (lvl + 1)))
                         & ((hi_idx >> lvl) == (lo_idx >> lvl) + 1))
            lvl += 1
        level_masks.append(masks)

    group = max(1, min(DN_CHAINS // (2 * nhs), n_chunks))
    units = [(hh, t) for hh in range(nhs) for t in range(group)]
    chains = [(m, dr) for m in range(len(units)) for dr in range(2)]
    lane_sq = jj

    def group_body(i, carry):
        cs = [i * group + t for hh, t in units]
        rows = [pl.ds(pl.multiple_of(c * c_len, c_len), c_len) for c in cs]
        q = [qn_ref[hh, rows[m], :] for m, (hh, t) in enumerate(units)]
        k = [kn_ref[hh, rows[m], :] for m, (hh, t) in enumerate(units)]
        v = [vn_ref[hh, rows[m], :] for m, (hh, t) in enumerate(units)]
        blk = [col_ref[hh, rows[m], :] for m, (hh, t) in enumerate(units)]
        q16 = [x.astype(BF16) for x in q]
        k16 = [x.astype(BF16) for x in k]
        qk = [_dot_nt(a, b) for a, b in zip(q16, k16)]
        kk = [_dot_nt(b, b) for b in k16]
        parts = [_split3(x) for x in blk]
        cum = [jnp.where(lane_sq == 2, sum(_dot(tri16[0], p) for p in ps), sum(_dot(tri16[1], p) for p in ps))
               for ps in parts]
        cum_t = [x.T for x in cum]
        gc_col = [cum[m][:, 2 + dr:3 + dr] for m, dr in chains]
        gc_row = [cum_t[m][2 + dr:3 + dr, :] for m, dr in chains]
        beta = [blk[m][:, dr:dr + 1] for m, dr in chains]
        decay = [jnp.where(causal[dr], jnp.exp(jnp.where(causal[dr], gc_col[n] - gc_row[n], 0.0)), 0.0)
                 for n, (m, dr) in enumerate(chains)]
        lm = [jnp.where(strict[dr], (beta[n] * kk[m]) * decay[n], 0.0) for n, (m, dr) in enumerate(chains)]
        lm16 = [x.astype(BF16) for x in lm]
        zero16 = jnp.zeros((c_len, c_len), BF16)
        xs = [eye - jnp.where(level_masks[dr][0], lm[n], 0.0) for n, (m, dr) in enumerate(chains)]
        for lvl in range(1, len(level_masks[0])):
            ys = [_dot(jnp.where(level_masks[dr][lvl], lm16[n], zero16), xs[n].astype(BF16))
                  for n, (m, dr) in enumerate(chains)]
            xs = [xs[n] - _dot(xs[n].astype(BF16), ys[n].astype(BF16)) for n in range(len(chains))]
        tmat = [x.astype(BF16) for x in xs]
        eg = [jnp.exp(x) for x in gc_col]
        g_last = [gc_col[n][c_len - 1:c_len] if dr == 0 else gc_col[n][0:1] for n, (m, dr) in enumerate(chains)]
        us = [_dot(tmat[n], (v[m] * beta[n]).astype(BF16)) for n, (m, dr) in enumerate(chains)]
        ws = [_dot(tmat[n], (k[m] * (beta[n] * eg[n])).astype(BF16)) for n, (m, dr) in enumerate(chains)]
        for n, (m, dr) in enumerate(chains):
            hh, c = units[m][0], cs[m]
            u_ref[dr, hh, rows[m], :] = us[n]
            wq_ref[dr, hh, pl.ds(pl.multiple_of(2 * c * c_len, c_len), c_len), :] = ws[n].astype(BF16)
            wq_ref[dr, hh, pl.ds(pl.multiple_of(2 * c * c_len + c_len, c_len), c_len), :] = (
                q[m] * eg[n]).astype(BF16)
            kd_ref[dr, hh, rows[m], :] = (k[m] * jnp.exp(g_last[n] - gc_col[n])).astype(BF16)
            in_ref[dr, hh, rows[m], :] = (qk[m] * decay[n]).astype(BF16)
            gl_ref[dr, hh, pl.ds(c, 1), :] = jnp.broadcast_to(jnp.exp(g_last[n]), (1, LANES))
        return carry

    lax.fori_loop(0, n_chunks // group, group_body, 0)

    seq_chains = [(dr, hh) for dr in range(2) for hh in range(nhs)]

    def step(i, states):
        cs = [i, n_chunks - 1 - i]
        rows = [pl.ds(pl.multiple_of(c * c_len, c_len), c_len) for c in cs]
        ws_qs = [_dot(wq_ref[dr, hh, pl.ds(pl.multiple_of(2 * cs[dr] * c_len, 2 * c_len), 2 * c_len), :],
                      states[n].astype(BF16)) for n, (dr, hh) in enumerate(seq_chains)]
        v16 = [(u_ref[dr, hh, rows[dr], :] - ws_qs[n][0:c_len]).astype(BF16)
               for n, (dr, hh) in enumerate(seq_chains)]
        intra = [_dot(in_ref[dr, hh, rows[dr], :], v16[n]) for n, (dr, hh) in enumerate(seq_chains)]
        upd = [_dot_tn(kd_ref[dr, hh, rows[dr], :], v16[n]) for n, (dr, hh) in enumerate(seq_chains)]
        for n, (dr, hh) in enumerate(seq_chains):
            acc_ref[hh, rows[dr], :] += ws_qs[n][c_len:2 * c_len] + intra[n]
        return tuple(states[n] * gl_ref[dr, hh, pl.ds(cs[dr], 1), :] + upd[n]
                     for n, (dr, hh) in enumerate(seq_chains))

    s_fin = lax.fori_loop(0, n_chunks, step, tuple(s0_ref[dr, hh] for dr, hh in seq_chains))
    for n, (dr, hh) in enumerate(seq_chains):
        sfin_ref[dr, hh] = s_fin[n]

    for hh in range(nhs):
        lanes = slice(hh * LANES, (hh + 1) * LANES)
        o = acc_ref[hh]
        y = o * lax.rsqrt(jnp.mean(o * o, axis=-1, keepdims=True) + EPS) * ng_ref[...]
        z = z_ref[:, lanes]
        o_ref[:, lanes] = y * (z * _sigmoid(z))


def _dn_call(proj, params, s0, s0_map, *, layer, batch, seq, nhs):
    nh = DN_HEADS
    nblk = nh // nhs
    width = nhs * LANES
    n_chunks = seq // DN_TILE
    vec = pl.BlockSpec((None, 1, LANES), lambda b, h: (layer, 0, 0))
    nsub = max(n_chunks, SUBLANES)
    return pl.pallas_call(
        functools.partial(_dn_kernel, seq=seq, nhs=nhs),
        grid=(batch, nblk),
        in_specs=[pl.BlockSpec((seq, width), lambda b, h: (b, COL_DQ * nblk + h)),
                  pl.BlockSpec((seq, width), lambda b, h: (b, (COL_DQ + 1) * nblk + h)),
                  pl.BlockSpec((seq, width), lambda b, h: (b, (COL_DQ + 2) * nblk + h)),
                  pl.BlockSpec((seq, width), lambda b, h: (b, COL_DZ * nblk + h)),
                  pl.BlockSpec((seq, LANES), lambda b, h: (b, COL_SC128)),
                  pl.BlockSpec((None, 3, CONV_W, width), lambda b, h: (layer, 0, 0, h)),
                  vec, vec, vec,
                  pl.BlockSpec((None, None, 2, nhs, DN_DK, DN_DK), s0_map)],
        out_specs=[pl.BlockSpec((seq, width), lambda b, h: (b, h)),
                   pl.BlockSpec((None, 2, nhs, DN_DK, DN_DK), lambda b, h: (b, 0, h, 0, 0))],
        out_shape=[jax.ShapeDtypeStruct((batch * seq, BRANCH_W), F32),
                   jax.ShapeDtypeStruct((batch, 2, nh, DN_DK, DN_DK), F32)],
        scratch_shapes=[pltpu.VMEM((seq + 2 * SUBLANES, LANES), F32),
                        pltpu.VMEM((nhs, seq, LANES), F32),
                        pltpu.VMEM((nhs, seq, LANES), F32),
                        pltpu.VMEM((nhs, seq, LANES), F32),
                        pltpu.VMEM((nhs, seq, LANES), F32),
                        pltpu.VMEM((2, nhs, seq, LANES), F32),
                        pltpu.VMEM((2, nhs, 2 * seq, LANES), BF16),
                        pltpu.VMEM((2, nhs, seq, LANES), BF16),
                        pltpu.VMEM((2, nhs, seq, DN_TILE), BF16),
                        pltpu.VMEM((2, nhs, nsub, LANES), F32),
                        pltpu.VMEM((nhs, seq, LANES), F32)],
        compiler_params=_params(2),
        name="deltanet",
    )(proj, proj, proj, proj, proj, params["dn_cw"], params["dn_alog"], params["dn_dtb"], params["dn_ng"], s0)


def _prepare_params(w_ffn_gate, w_ffn_up, w_ffn_down, w_in, lru_conv_w, lru_conv_b, lru_w_r, lru_b_r,
                    lru_w_i, lru_b_i, lru_lambda, gqa_q_norm, gqa_k_norm, na_rpb, dn_conv_w, dn_a_log,
                    dn_dt_bias, dn_norm_g, w_branch, w_out, norm_g):
    depth = w_in.shape[0]
    offs = np.cumsum((0,) + IN_WIDTHS)
    seg = [w_in[:, :, offs[i]:offs[i + 1]] for i in range(len(IN_WIDTHS))]
    (a_x, a_y, b_q, b_k, b_v, c_q, c_k, c_v, d_q, d_k, d_v, d_z, d_b, d_a, g_lin) = seg
    perm = np.asarray(GQA_PERM)
    b_q = b_q.reshape(depth, D_MODEL, GQA_HEADS, HEAD_DIM)[:, :, perm].reshape(depth, D_MODEL, BRANCH_W)
    pad = jnp.zeros((depth, D_MODEL, PROJ_W - 5120 - 2 * LANES - 16), F32)
    w_main = jnp.concatenate([a_x, a_y, b_q, c_q, c_k, c_v, d_q, d_k, d_v, d_z, b_k, b_v, d_b, d_a, pad],
                             axis=2).astype(BF16)

    def block_diag(wb):
        wb = wb.reshape(depth, 2, LRU_BLOCKS // 2, 2, LRU_BW, LRU_BW)
        z = jnp.zeros_like(wb[:, :, :, 0])
        return jnp.concatenate([jnp.concatenate([wb[:, :, :, 0], z], axis=-1),
                                jnp.concatenate([z, wb[:, :, :, 1]], axis=-1)], axis=-2)

    wr, wi = block_diag(lru_w_r), block_diag(lru_w_i)
    lru_wg = jnp.concatenate([wr[:, 0], wi[:, 0], wr[:, 1], wi[:, 1]], axis=-1).astype(BF16)
    ncol = BRANCH_W // LANES

    def col_blocks(v):
        return v.reshape(depth, ncol, LANES)

    lru_bg = jnp.concatenate([col_blocks(lru_b_r[:, 0]), col_blocks(lru_b_i[:, 0]),
                              col_blocks(lru_b_r[:, 1]), col_blocks(lru_b_i[:, 1])], axis=-1)[:, :, None, :]

    wb_b = w_branch[:, 1].reshape(depth, GQA_HEADS, HEAD_DIM, D_MODEL)[:, perm].reshape(depth, BRANCH_W, D_MODEL)
    wb = jnp.stack([w_branch[:, 0], wb_b, w_branch[:, 2], w_branch[:, 3]], axis=1).astype(BF16)

    lane_pad = jnp.zeros((depth, LANES - 4 * DN_HEADS), F32)
    lane_zero = jnp.zeros((depth, 2 * DN_HEADS), F32)
    alog = jnp.concatenate([lane_zero, dn_a_log.reshape(depth, -1), lane_pad], axis=1)[:, None, :]
    dtb = jnp.concatenate([lane_zero, dn_dt_bias.reshape(depth, -1), lane_pad], axis=1)[:, None, :]

    return dict(
        norm_g=norm_g[:, :, None, :],
        wg=w_ffn_gate.astype(BF16), wu=w_ffn_up.astype(BF16), wd=w_ffn_down.astype(BF16),
        w_main=w_main, w_gate=g_lin.astype(BF16),
        lru_cw=lru_conv_w, lru_cb=lru_conv_b[:, None, :], lru_wg=lru_wg, lru_bg=lru_bg, lru_lam=lru_lambda,
        gq=jnp.tile(gqa_q_norm, (1, 2))[:, None, :], gk=jnp.tile(gqa_k_norm, (1, 2))[:, None, :],
        na_bias=_na_bias_table(na_rpb),
        dn_cw=dn_conv_w.reshape(depth, CONV_W, 3, BRANCH_W).transpose(0, 2, 1, 3),
        dn_alog=alog, dn_dtb=dtb, dn_ng=dn_norm_g[:, None, :],
        wb=wb, w_out=w_out.astype(BF16),
    )


def _rope_tables(seq):
    pos = jnp.arange(seq)
    half = HEAD_DIM // 2
    inv = jnp.power(ROPE_BASE, -jnp.arange(0, half, 2, dtype=F32) / half)
    ang_r = (pos // GRID_W).astype(F32)[:, None] * inv[None, :]
    ang_c = (pos % GRID_W).astype(F32)[:, None] * inv[None, :]
    cos = jnp.concatenate([jnp.cos(ang_r)] * 2 + [jnp.cos(ang_c)] * 2, axis=-1)
    sin = jnp.concatenate([-jnp.sin(ang_r), jnp.sin(ang_r), -jnp.sin(ang_c), jnp.sin(ang_c)], axis=-1)
    return jnp.tile(cos, (1, 2)), jnp.tile(sin, (1, 2))


def _na_bias_table(rpb):
    qc = np.arange(GRID_W)
    cs = np.clip(qc - NA_WIN_C // 2, 0, GRID_W - NA_WIN_C)
    kc = np.arange(GRID_W)
    inwin = (kc[None, :] >= cs[:, None]) & (kc[None, :] < cs[:, None] + NA_WIN_C)
    coff = kc[None, :] - qc[:, None] + NA_WIN_C - 1
    onehot = (coff[None] == np.arange(2 * NA_WIN_C - 1)[:, None, None]).astype(np.float32)
    t = jnp.einsum("lhrd,dqk->lhrqk", rpb.astype(F32), onehot, precision=lax.Precision.HIGHEST)
    t = jnp.where(inwin, t, NEG_BIG)
    edge = jnp.full_like(t[:, :, :1], NEG_BIG)
    t = jnp.concatenate([edge, t, edge], axis=2)
    return jnp.concatenate([t[:, :, :-1], t[:, :, 1:]], axis=-1)


def _layer(x, mod_all, params, *, batch, seq, latent, layer, caches, tables, final_g, tm):
    row0 = 1 if latent else 0
    x = _ffn_call(x, mod_all, params, final_g, layer=layer, which=0, row0=row0, final=False, tm=tm)
    proj = _inproj_call(x, mod_all, params, layer=layer, row0=row0, tm=tm)

    dn_heads_per_step = 1 if seq > 1024 else DN_HEADS
    if latent:
        cache_ak, cache_av, cache_nk, cache_nv, state_lru, state_delta = caches
        h0, h0_map = state_lru, lambda b, c: (b, layer, 0, c)
        s0, s0_map = state_delta, lambda b, h: (b, layer, 0, h, 0, 0)
    else:
        h0, h0_map = jnp.zeros((1, 1, 2, BRANCH_W), F32), lambda b, c: (0, 0, 0, c)
        s0 = jnp.zeros((1, 1, 2, dn_heads_per_step, DN_DK, DN_DK), F32)
        s0_map = lambda b, h: (0, 0, 0, 0, 0, 0)

    o_a, lru_fin = _lru_call(proj, h0, h0_map, params, layer=layer, batch=batch, seq=seq)

    cos, sin = tables["rope"] if latent else (None, None)
    qn, kn = _prep_call(proj, params, cos, sin, layer=layer, batch=batch, seq=seq, rope=latent)
    tq = min(seq, 256)
    kv_new = (kn, pl.BlockSpec((seq, LANES), lambda b, i: (b, 0)),
              proj, pl.BlockSpec((seq, LANES), lambda b, i: (b, COL_BV128)))
    if latent:
        gqa_cache = pl.BlockSpec((None, None, PAST_LEN, LANES), lambda b, i: (b, layer, 0, 0))
        sources = [kv_new, (cache_ak, gqa_cache, cache_av, gqa_cache)]
    else:
        sources = [kv_new]
    o_b = _attn_call(qn, 0, sources, batch=batch, seq=seq, tq=tq, nkb=1, qscale=1.0)

    if latent:
        o_c = _na_call(proj, cache_nk, cache_nv, params, layer=layer, batch=batch, seq=seq)
    else:
        src = (proj, pl.BlockSpec((seq, BRANCH_W), lambda b, i: (b, COL_CK)),
               proj, pl.BlockSpec((seq, BRANCH_W), lambda b, i: (b, COL_CV)))
        o_c = _attn_call(proj, COL_CQ, [src], batch=batch, seq=seq, tq=tq, nkb=4, qscale=HEAD_DIM ** -0.5)

    o_d, dn_fin = _dn_call(proj, params, s0, s0_map, layer=layer, batch=batch, seq=seq, nhs=dn_heads_per_step)

    x = _merge_call(x, mod_all, params, (o_a, o_b, o_c, o_d), layer=layer, row0=row0, tm=tm)
    x = _ffn_call(x, mod_all, params, final_g, layer=layer, which=1, row0=row0,
                  final=(layer == DEPTH - 1), tm=tm)

    new_ctx = None
    if not latent:
        new_ctx = (kn.reshape(batch, seq, GQA_KV, HEAD_DIM),
                   proj[:, COL_BV128 * LANES:(COL_BV128 + 1) * LANES].reshape(batch, seq, GQA_KV, HEAD_DIM),
                   proj[:, COL_CK * BRANCH_W:(COL_CK + 1) * BRANCH_W].reshape(batch, seq, NA_HEADS, HEAD_DIM),
                   proj[:, COL_CV * BRANCH_W:(COL_CV + 1) * BRANCH_W].reshape(batch, seq, NA_HEADS, HEAD_DIM),
                   lru_fin, dn_fin)
    return x, new_ctx


def kernel(x_prompt, x_sample, c, cache_attn_k, cache_attn_v, cache_na_k, cache_na_v, state_lru, state_delta, c_ctx, w_mod, b_mod, norm_g, w_ffn_gate, w_ffn_up, w_ffn_down, w_in, lru_conv_w, lru_conv_b, lru_w_r, lru_b_r, lru_w_i, lru_b_i, lru_lambda, gqa_q_norm, gqa_k_norm, na_rpb, dn_conv_w, dn_a_log, dn_dt_bias, dn_norm_g, w_branch, w_out, final_norm_g):
    batch_c, seq_c, _ = x_prompt.shape
    batch_l, seq_l, _ = x_sample.shape
    assert batch_l + 1 <= SUBLANES

    cs = jnp.concatenate([c_ctx[None, :], c, jnp.zeros((SUBLANES - 1 - batch_l, D_MODEL), F32)], axis=0)
    mod_all = _mod_call(cs, w_mod, b_mod).reshape(DEPTH, SUBLANES, N_MOD, D_MODEL)
    params = _prepare_params(w_ffn_gate, w_ffn_up, w_ffn_down, w_in, lru_conv_w, lru_conv_b, lru_w_r, lru_b_r,
                             lru_w_i, lru_b_i, lru_lambda, gqa_q_norm, gqa_k_norm, na_rpb, dn_conv_w, dn_a_log,
                             dn_dt_bias, dn_norm_g, w_branch, w_out, norm_g)

    caches = (cache_attn_k.reshape(batch_l, DEPTH, PAST_LEN, GQA_KV * HEAD_DIM),
              cache_attn_v.reshape(batch_l, DEPTH, PAST_LEN, GQA_KV * HEAD_DIM),
              cache_na_k.reshape(batch_l, DEPTH, PAST_LEN, BRANCH_W),
              cache_na_v.reshape(batch_l, DEPTH, PAST_LEN, BRANCH_W),
              state_lru, state_delta)
    tables = {"rope": _rope_tables(seq_l)}
    final_g = final_norm_g[None, :]

    xc = x_prompt.reshape(1, batch_c * seq_c, D_MODEL)
    xl = x_sample
    ctx_out = []
    for l in range(DEPTH):
        xc, new_ctx = _layer(xc, mod_all, params, batch=batch_c, seq=seq_c, latent=False, layer=l, caches=None,
                             tables=tables, final_g=final_g, tm=DENSE_ROWS)
        ctx_out.append(new_ctx)
        xl, _ = _layer(xl, mod_all, params, batch=batch_l, seq=seq_l, latent=True, layer=l, caches=caches,
                       tables=tables, final_g=final_g, tm=DENSE_ROWS)

    stacked = [jnp.stack([ctx_out[l][i] for l in range(DEPTH)], axis=1) for i in range(6)]
    return (xc.reshape(batch_c, seq_c, D_MODEL), xl, *stacked)
```

```python
import functools

import numpy as np
import jax
import jax.numpy as jnp
from jax import lax
from jax.experimental import pallas as pl
from jax.experimental.pallas import tpu as pltpu

F32 = jnp.float32
BF16 = jnp.bfloat16

D_MODEL = 1024
DEPTH = 4
GRID_W = 64
N_BRANCH = 4
BRANCH_W = 512
N_MOD = 9
D_FF = 2816
EPS = 1e-6
CONV_W = 4
LRU_BLOCKS = 8
LRU_BW = 64
LRU_C = 8.0
HEAD_DIM = 64
GQA_HEADS = 8
GQA_KV = 2
ROPE_BASE = 10000.0
NA_HEADS = 8
NA_WIN_R = 8
NA_WIN_C = 16
NA_QROWS = 4
NA_KROWS = 12
DN_DK = 128
DN_HEADS = 4
LRU_PITCH_PAD = 8
DN_TILE = 128
DN_CHAINS = 16
PAST_LEN = 512
IN_WIDTHS = (512, 512, 512, 128, 128, 512, 512, 512, 512, 512, 512, 512, 8, 8, 4096)

LANES = 128
SUBLANES = 8
DENSE_ROWS = 512
ATTN_QROWS = 512
PROJ_W = 5632
COL_BQ, COL_CQ, COL_CK, COL_CV, COL_DQ, COL_DZ = 2, 3, 4, 5, 6, 9
COL_BK128, COL_BV128, COL_SC128 = 40, 41, 42
VMEM_LIMIT = 56 * 1024 * 1024
NEG_BIG = -1e30
GQA_PERM = (0, 4, 1, 5, 2, 6, 3, 7)


def _params(n):
    return pltpu.CompilerParams(dimension_semantics=("arbitrary",) * n, vmem_limit_bytes=VMEM_LIMIT)


def _const_spec(shape):
    nd = len(shape)
    return pl.BlockSpec(shape, lambda *_: (0,) * nd, pipeline_mode=pl.Buffered(1))


def _layer_spec(arr, layer, *sub):
    tail = arr.shape[1 + len(sub):]
    index = (layer,) + tuple(sub) + (0,) * len(tail)
    return pl.BlockSpec((None,) * (1 + len(sub)) + tuple(tail), lambda *_: index, pipeline_mode=pl.Buffered(1))


def _mod_spec(layer, row0):
    return pl.BlockSpec((None, None, N_MOD, D_MODEL), lambda b, i: (layer, row0 + b, 0, 0))


def _dot(a, b):
    return jnp.dot(a, b, preferred_element_type=F32)


def _dot_nt(a, b):
    return lax.dot_general(a, b, (((1,), (1,)), ((), ())), preferred_element_type=F32)


def _dot_tn(a, b):
    return lax.dot_general(a, b, (((0,), (0,)), ((), ())), preferred_element_type=F32)


def _split(x):
    hi = x.astype(BF16)
    lo = (x - hi.astype(F32)).astype(BF16)
    return hi, lo


def _dot3(a, b):
    ah, al = _split(a)
    bh, bl = _split(b)
    return _dot(ah, bh) + (_dot(al, bh) + _dot(ah, bl))


def _sigmoid(x):
    return 0.5 * jnp.tanh(0.5 * x) + 0.5


def _modnorm(x, g, shift, scale):
    ms = jnp.mean(x * x, axis=-1, keepdims=True)
    return (x * lax.rsqrt(ms + EPS) * g) * (1.0 + scale) + shift


def _mod_kernel(c_ref, w_ref, b_ref, o_ref):
    c = c_ref[...]
    o_ref[0] = _dot3(c * _sigmoid(c), w_ref[0]) + b_ref[0]


def _mod_call(cs, w_mod, b_mod):
    tn = 1024
    n = N_MOD * D_MODEL
    return pl.pallas_call(
        _mod_kernel,
        grid=(DEPTH, n // tn),
        in_specs=[pl.BlockSpec((SUBLANES, D_MODEL), lambda l, j: (0, 0)),
                  pl.BlockSpec((1, D_MODEL, tn), lambda l, j: (l, 0, j)),
                  pl.BlockSpec((1, 1, tn), lambda l, j: (l, 0, j))],
        out_specs=pl.BlockSpec((1, SUBLANES, tn), lambda l, j: (l, 0, j)),
        out_shape=jax.ShapeDtypeStruct((DEPTH, SUBLANES, n), F32),
        compiler_params=_params(2),
        name="mod",
    )(cs, w_mod, b_mod.reshape(DEPTH, 1, n))


def _ffn_kernel(x_ref, mod_ref, g_ref, wg_ref, wu_ref, wd_ref, gf_ref, o_ref, *, mi, final):
    x = x_ref[0]
    mod = mod_ref[...]
    h = _modnorm(x, g_ref[...], mod[mi:mi + 1], mod[mi + 1:mi + 2]).astype(BF16)
    gt = _dot(h, wg_ref[...])
    up = _dot(h, wu_ref[...])
    a = (gt * _sigmoid(gt) * up).astype(BF16)
    y = x + 0.5 * mod[mi + 2:mi + 3] * _dot(a, wd_ref[...])
    if final:
        ms = jnp.mean(y * y, axis=-1, keepdims=True)
        y = y * lax.rsqrt(ms + EPS) * gf_ref[...]
    o_ref[0] = y


def _ffn_call(x, mod_all, params, gf, *, layer, which, row0, final, tm):
    nb, rows, _ = x.shape
    return pl.pallas_call(
        functools.partial(_ffn_kernel, mi=6 * which, final=final),
        grid=(nb, rows // tm),
        in_specs=[pl.BlockSpec((1, tm, D_MODEL), lambda b, i: (b, i, 0)),
                  _mod_spec(layer, row0),
                  _layer_spec(params["norm_g"], layer, 2 * which),
                  _layer_spec(params["wg"], layer, which),
                  _layer_spec(params["wu"], layer, which),
                  _layer_spec(params["wd"], layer, which),
                  _const_spec((1, D_MODEL))],
        out_specs=pl.BlockSpec((1, tm, D_MODEL), lambda b, i: (b, i, 0)),
        out_shape=jax.ShapeDtypeStruct(x.shape, F32),
        compiler_params=_params(2),
        name="ffn",
    )(x, mod_all, params["norm_g"], params["wg"], params["wu"], params["wd"], gf)


def _inproj_kernel(x_ref, mod_ref, g_ref, w_ref, o_ref):
    mod = mod_ref[...]
    h = _modnorm(x_ref[0], g_ref[...], mod[3:4], mod[4:5]).astype(BF16)
    o_ref[...] = _dot(h, w_ref[...])


def _inproj_call(x, mod_all, params, *, layer, row0, tm):
    nb, rows, _ = x.shape
    nt = rows // tm
    return pl.pallas_call(
        _inproj_kernel,
        grid=(nb, nt),
        in_specs=[pl.BlockSpec((1, tm, D_MODEL), lambda b, i: (b, i, 0)),
                  _mod_spec(layer, row0),
                  _layer_spec(params["norm_g"], layer, 1),
                  _layer_spec(params["w_main"], layer)],
        out_specs=pl.BlockSpec((tm, PROJ_W), lambda b, i: (b * nt + i, 0)),
        out_shape=jax.ShapeDtypeStruct((nb * rows, PROJ_W), F32),
        compiler_params=_params(2),
        name="inproj",
    )(x, mod_all, params["norm_g"], params["w_main"])


def _merge_kernel(x_ref, mod_ref, g_ref, oa_ref, ob_ref, oc_ref, od_ref, wgate_ref, wb_ref, wout_ref, o_ref):
    x = x_ref[0]
    mod = mod_ref[...]
    h = _modnorm(x, g_ref[...], mod[3:4], mod[4:5]).astype(BF16)
    acc = None
    for n, ref in enumerate((oa_ref, ob_ref, oc_ref, od_ref)):
        gate = _sigmoid(_dot(h, wgate_ref[:, n * D_MODEL:(n + 1) * D_MODEL]))
        term = gate * _dot(ref[...].astype(BF16), wb_ref[n])
        acc = term if acc is None else acc + term
    o_ref[0] = x + mod[5:6] * _dot(acc.astype(BF16), wout_ref[...])


def _merge_call(x, mod_all, params, outs, *, layer, row0, tm):
    nb, rows, _ = x.shape
    nt = rows // tm
    ospec = pl.BlockSpec((tm, BRANCH_W), lambda b, i: (b * nt + i, 0))
    return pl.pallas_call(
        _merge_kernel,
        grid=(nb, nt),
        in_specs=[pl.BlockSpec((1, tm, D_MODEL), lambda b, i: (b, i, 0)),
                  _mod_spec(layer, row0),
                  _layer_spec(params["norm_g"], layer, 1),
                  ospec, ospec, ospec, ospec,
                  _layer_spec(params["w_gate"], layer),
                  _layer_spec(params["wb"], layer),
                  _layer_spec(params["w_out"], layer)],
        out_specs=pl.BlockSpec((1, tm, D_MODEL), lambda b, i: (b, i, 0)),
        out_shape=jax.ShapeDtypeStruct(x.shape, F32),
        compiler_params=_params(2),
        name="merge",
    )(x, mod_all, params["norm_g"], *outs, params["w_gate"], params["wb"], params["w_out"])


def _log_sigmoid(x):
    return jnp.minimum(x, 0.0) - jnp.log1p(jnp.exp(-jnp.abs(x)))


def _gelu_tanh(x):
    return x * (0.5 * (1.0 + jnp.tanh(0.7978845608028654 * (x + 0.044715 * (x * x * x)))))


def _lru_kernel(ax_ref, ay_ref, h0_ref, cw_ref, cb_ref, wg_ref, bg_ref, lam_ref, o_ref, fin_ref,
                xp_ref, a_ref, u_ref, h_ref, p_ref, *, seq):
    lc = seq // SUBLANES
    pitch = lc + LRU_PITCH_PAD
    zero8 = jnp.zeros((SUBLANES, LANES), F32)
    xp_ref[0:SUBLANES, :] = zero8
    xp_ref[SUBLANES + seq:2 * SUBLANES + seq, :] = zero8
    xp_ref[SUBLANES:SUBLANES + seq, :] = ax_ref[...]
    cw = cw_ref[...]
    xa = cb_ref[...] + cw[0:1] * xp_ref[pl.ds(SUBLANES - 2, seq), :]
    for k in range(1, CONV_W):
        xa = xa + cw[k:k + 1] * xp_ref[pl.ds(SUBLANES - 2 + k, seq), :]
    gates = _dot(xa.astype(BF16), wg_ref[...]) + bg_ref[...]
    lam = lam_ref[...]
    for dr in range(2):
        r = _sigmoid(gates[:, (2 * dr) * LANES:(2 * dr + 1) * LANES])
        i = _sigmoid(gates[:, (2 * dr + 1) * LANES:(2 * dr + 2) * LANES])
        log_a = (LRU_C * r) * _log_sigmoid(lam[dr:dr + 1])
        a = jnp.exp(log_a)
        one_m_a2 = -jnp.tanh(log_a) * (a * a + 1.0)
        root = jnp.where(one_m_a2 > 0.0, one_m_a2 * lax.rsqrt(one_m_a2), 0.0)
        u = root * (i * xa)
        for k in range(SUBLANES):
            a_ref[dr, pl.ds(k * pitch, lc), :] = a[k * lc:(k + 1) * lc]
            u_ref[dr, pl.ds(k * pitch, lc), :] = u[k * lc:(k + 1) * lc]

    sub = lax.broadcasted_iota(jnp.int32, (SUBLANES, LANES), 0)
    h0 = h0_ref[...]
    hf0 = jnp.where(sub == 0, h0[0:1], 0.0)
    hb0 = jnp.where(sub == SUBLANES - 1, h0[1:2], 0.0)
    ones = jnp.ones((SUBLANES, LANES), F32)

    def body(s, carry):
        hf, pf, hb, pb = carry
        rows_f = pl.ds(s, SUBLANES, stride=pitch)
        rows_b = pl.ds(lc - 1 - s, SUBLANES, stride=pitch)
        af = a_ref[0, rows_f, :]
        hf = af * hf + u_ref[0, rows_f, :]
        pf = af * pf
        h_ref[0, rows_f, :] = hf
        p_ref[0, rows_f, :] = pf
        ab = a_ref[1, rows_b, :]
        hb = ab * hb + u_ref[1, rows_b, :]
        pb = ab * pb
        h_ref[1, rows_b, :] = hb
        p_ref[1, rows_b, :] = pb
        return hf, pf, hb, pb

    hf, pf, hb, pb = lax.fori_loop(0, lc, body, (hf0, ones, hb0, ones), unroll=8)

    cf = [jnp.zeros((1, LANES), F32)]
    for k in range(1, SUBLANES):
        cf.append(hf[k - 1:k] + pf[k - 1:k] * cf[k - 1])
    fin_ref[0:1, :] = hf[SUBLANES - 1:SUBLANES] + pf[SUBLANES - 1:SUBLANES] * cf[SUBLANES - 1]
    cb = [None] * SUBLANES
    cb[SUBLANES - 1] = jnp.zeros((1, LANES), F32)
    for k in range(SUBLANES - 2, -1, -1):
        cb[k] = hb[k + 1:k + 2] + pb[k + 1:k + 2] * cb[k + 1]
    fin_ref[1:2, :] = hb[0:1] + pb[0:1] * cb[0]

    for k in range(SUBLANES):
        rows = pl.ds(k * lc, lc)
        held = pl.ds(k * pitch, lc)
        h = (h_ref[0, held, :] + p_ref[0, held, :] * cf[k]) + (h_ref[1, held, :] + p_ref[1, held, :] * cb[k])
        o_ref[rows, :] = h * _gelu_tanh(ay_ref[rows, :])


def _lru_call(proj, h0, h0_map, params, *, layer, batch, seq):
    ncol = BRANCH_W // LANES
    pad_rows = SUBLANES * LRU_PITCH_PAD
    return pl.pallas_call(
        functools.partial(_lru_kernel, seq=seq),
        grid=(batch, ncol),
        in_specs=[pl.BlockSpec((seq, LANES), lambda b, c: (b, c)),
                  pl.BlockSpec((seq, LANES), lambda b, c: (b, ncol + c)),
                  pl.BlockSpec((None, None, 2, LANES), h0_map),
                  pl.BlockSpec((None, CONV_W, LANES), lambda b, c: (layer, 0, c)),
                  pl.BlockSpec((None, 1, LANES), lambda b, c: (layer, 0, c)),
                  pl.BlockSpec((None, None, LANES, 4 * LANES), lambda b, c: (layer, c, 0, 0)),
                  pl.BlockSpec((None, None, 1, 4 * LANES), lambda b, c: (layer, c, 0, 0)),
                  pl.BlockSpec((None, 2, LANES), lambda b, c: (layer, 0, c))],
        out_specs=[pl.BlockSpec((seq, LANES), lambda b, c: (b, c)),
                   pl.BlockSpec((None, 2, LANES), lambda b, c: (b, 0, c))],
        out_shape=[jax.ShapeDtypeStruct((batch * seq, BRANCH_W), F32),
                   jax.ShapeDtypeStruct((batch, 2, BRANCH_W), F32)],
        scratch_shapes=[pltpu.VMEM((seq + 2 * SUBLANES, LANES), F32),
                        pltpu.VMEM((2, seq + pad_rows, LANES), F32),
                        pltpu.VMEM((2, seq + pad_rows, LANES), F32),
                        pltpu.VMEM((2, seq + pad_rows, LANES), F32),
                        pltpu.VMEM((2, seq + pad_rows, LANES), F32)],
        compiler_params=_params(2),
        name="lru",
    )(proj, proj, h0, params["lru_cw"], params["lru_cb"], params["lru_wg"], params["lru_bg"], params["lru_lam"])


def _prep_kernel(*refs, rope):
    if rope:
        q_ref, kv_ref, gq_ref, gk_ref, cos_ref, sin_ref, qn_ref, kn_ref = refs
    else:
        q_ref, kv_ref, gq_ref, gk_ref, qn_ref, kn_ref = refs
    rows = q_ref.shape[0]
    lane = lax.broadcasted_iota(jnp.int32, (rows, LANES), 1)
    lo = lane < HEAD_DIM
    first16 = (lane & 16) == 0

    def head_norm(x, g):
        sq = x * x
        s_lo = jnp.sum(jnp.where(lo, sq, 0.0), axis=-1, keepdims=True)
        s_hi = jnp.sum(jnp.where(lo, 0.0, sq), axis=-1, keepdims=True)
        ms = jnp.where(lo, s_lo, s_hi) * (1.0 / HEAD_DIM)
        return x * lax.rsqrt(ms + EPS) * g

    def rotate(y):
        if not rope:
            return y
        partner = jnp.where(first16, pltpu.roll(y, LANES - 16, 1), pltpu.roll(y, 16, 1))
        return y * cos_ref[...] + partner * sin_ref[...]

    for p in range(BRANCH_W // LANES):
        cols = slice(p * LANES, (p + 1) * LANES)
        qn_ref[:, cols] = rotate(head_norm(q_ref[:, cols], gq_ref[...])) * (HEAD_DIM ** -0.5)
    kn_ref[...] = rotate(head_norm(kv_ref[:, 0:LANES], gk_ref[...]))


def _prep_call(proj, params, cos, sin, *, layer, batch, seq, rope):
    tab = pl.BlockSpec((seq, LANES), lambda b: (0, 0))
    in_specs = [pl.BlockSpec((seq, BRANCH_W), lambda b: (b, COL_BQ)),
                pl.BlockSpec((seq, BRANCH_W), lambda b: (b, COL_BK128 // 4)),
                _layer_spec(params["gq"], layer), _layer_spec(params["gk"], layer)]
    args = [proj, proj, params["gq"], params["gk"]]
    if rope:
        in_specs += [tab, tab]
        args += [cos, sin]
    return pl.pallas_call(
        functools.partial(_prep_kernel, rope=rope),
        grid=(batch,),
        in_specs=in_specs,
        out_specs=[pl.BlockSpec((seq, BRANCH_W), lambda b: (b, 0)),
                   pl.BlockSpec((seq, LANES), lambda b: (b, 0))],
        out_shape=[jax.ShapeDtypeStruct((batch * seq, BRANCH_W), F32),
                   jax.ShapeDtypeStruct((batch * seq, LANES), F32)],
        compiler_params=_params(1),
        name="attn_prep",
    )(*args)


def _softmax_pv(scores, values):
    m = None
    for s in scores:
        sm = jnp.max(s, axis=-1, keepdims=True)
        m = sm if m is None else jnp.maximum(m, sm)
    den = None
    out = None
    for s, v in zip(scores, values):
        p = jnp.exp(s - m)
        ps = jnp.sum(p, axis=-1, keepdims=True)
        den = ps if den is None else den + ps
        o = _dot(p.astype(BF16), v)
        out = o if out is None else out + o
    return out / den


def _attn_kernel(*refs, nsrc, nkb, qscale):
    q_ref = refs[0]
    src = refs[1:1 + 2 * nsrc]
    o_ref = refs[1 + 2 * nsrc]
    tq = q_ref.shape[0]
    lo = lax.broadcasted_iota(jnp.int32, (tq, LANES), 1) < HEAD_DIM

    loaded = {}

    def kv(p):
        col = p if nkb != 1 else 0
        if col not in loaded:
            cols = slice(col * LANES, (col + 1) * LANES)
            loaded[col] = ([src[2 * i][:, cols].astype(BF16) for i in range(nsrc)],
                           [src[2 * i + 1][:, cols].astype(BF16) for i in range(nsrc)])
        return loaded[col]

    def scores(unit):
        p, half = unit
        qb = q_ref[:, p * LANES:(p + 1) * LANES]
        if qscale != 1.0:
            qb = qb * qscale
        qm = jnp.where(lo if half == 0 else jnp.logical_not(lo), qb, 0.0).astype(BF16)
        return [_dot_nt(qm, k) for k in kv(p)[0]]

    units = [(p, half) for p in range(BRANCH_W // LANES) for half in range(2)]
    pending = scores(units[0])
    first_half = None
    for n, (p, half) in enumerate(units):
        current = pending
        if n + 1 < len(units):
            pending = scores(units[n + 1])
        out = _softmax_pv(current, kv(p)[1])
        if half == 0:
            first_half = out
        else:
            o_ref[:, p * LANES:(p + 1) * LANES] = jnp.where(lo, first_half, out)


def _attn_call(q_arr, q_col, sources, *, batch, seq, tq, nkb, qscale):
    nq = seq // tq
    in_specs = [pl.BlockSpec((tq, BRANCH_W), lambda b, i: (b * nq + i, q_col))]
    args = [q_arr]
    for k_arr, k_spec, v_arr, v_spec in sources:
        in_specs += [k_spec, v_spec]
        args += [k_arr, v_arr]
    return pl.pallas_call(
        functools.partial(_attn_kernel, nsrc=len(sources), nkb=nkb, qscale=qscale),
        grid=(batch, nq),
        in_specs=in_specs,
        out_specs=pl.BlockSpec((tq, BRANCH_W), lambda b, i: (b * nq + i, 0)),
        out_shape=jax.ShapeDtypeStruct((batch * seq, BRANCH_W), F32),
        compiler_params=_params(2),
        name="attn",
    )(*args)


def _na_row_start(r, rows):
    return jnp.clip(r - NA_WIN_R // 2, 0, rows - NA_WIN_R)


def _na_kernel(q_ref, k_ref, v_ref, ck_ref, cv_ref, bias_ref, o_ref, *, rows):
    r0 = pl.program_id(1) * NA_QROWS
    w0 = jnp.clip(r0 - NA_WIN_R // 2, 0, rows - NA_KROWS)
    win = pl.ds(pl.multiple_of(w0 * GRID_W, GRID_W), NA_KROWS * GRID_W)
    lo = lax.broadcasted_iota(jnp.int32, (NA_QROWS * GRID_W, LANES), 1) < HEAD_DIM
    lo_row = lax.broadcasted_iota(jnp.int32, (GRID_W, LANES), 1) < HEAD_DIM
    ncol = BRANCH_W // LANES
    scores = []
    for p in range(ncol):
        cols = slice(p * LANES, (p + 1) * LANES)
        qb = q_ref[:, cols] * (HEAD_DIM ** -0.5)
        kw = k_ref[win, cols].astype(BF16)
        kc = ck_ref[:, cols].astype(BF16)
        for half in range(2):
            qm = jnp.where(lo if half == 0 else jnp.logical_not(lo), qb, 0.0).astype(BF16)
            scores.append((_dot_nt(qm, kw), _dot_nt(qm, kc)))
    pair_index, pair_mask = [], []
    for j in range(NA_QROWS):
        r = r0 + j
        rs = _na_row_start(r, rows)
        idx_j, mask_j = [], []
        for t in range(NA_KROWS // 2):
            kr = w0 + 2 * t
            rel = kr - r + NA_WIN_R - 1
            idx_j.append(jnp.clip(rel + 1, 0, 2 * NA_WIN_R - 1))
            in_a = jnp.where((kr >= rs) & (kr < rs + NA_WIN_R), 0.0, NEG_BIG)
            in_b = jnp.where((kr + 1 >= rs) & (kr + 1 < rs + NA_WIN_R), 0.0, NEG_BIG)
            mask_j.append(jnp.where(lo_row, in_a, in_b))
        pair_index.append(idx_j)
        pair_mask.append(mask_j)
    probs = []
    for h, (s_loc, s_ctx) in enumerate(scores):
        bias = jnp.concatenate(
            [jnp.concatenate([bias_ref[h, pair_index[j][t]] + pair_mask[j][t] for t in range(NA_KROWS // 2)],
                             axis=-1) for j in range(NA_QROWS)], axis=0)
        s_loc = s_loc + bias
        m = jnp.maximum(jnp.max(s_loc, axis=-1, keepdims=True), jnp.max(s_ctx, axis=-1, keepdims=True))
        p_loc = jnp.exp(s_loc - m)
        p_ctx = jnp.exp(s_ctx - m)
        den = jnp.sum(p_loc, axis=-1, keepdims=True) + jnp.sum(p_ctx, axis=-1, keepdims=True)
        probs.append((p_loc.astype(BF16), p_ctx.astype(BF16), den))
    for p in range(ncol):
        cols = slice(p * LANES, (p + 1) * LANES)
        vw = v_ref[win, cols].astype(BF16)
        vc = cv_ref[:, cols].astype(BF16)
        halves = []
        for half in range(2):
            p_loc, p_ctx, den = probs[2 * p + half]
            halves.append((_dot(p_loc, vw) + _dot(p_ctx, vc)) / den)
        o_ref[:, cols] = jnp.where(lo, halves[0], halves[1])


def _na_call(proj, cache_k, cache_v, params, *, layer, batch, seq):
    rows = seq // GRID_W
    steps = rows // NA_QROWS
    qrows = NA_QROWS * GRID_W
    cache_spec = pl.BlockSpec((None, None, PAST_LEN, BRANCH_W), lambda b, r: (b, layer, 0, 0))
    return pl.pallas_call(
        functools.partial(_na_kernel, rows=rows),
        grid=(batch, steps),
        in_specs=[pl.BlockSpec((qrows, BRANCH_W), lambda b, r: (b * steps + r, COL_CQ)),
                  pl.BlockSpec((seq, BRANCH_W), lambda b, r: (b, COL_CK)),
                  pl.BlockSpec((seq, BRANCH_W), lambda b, r: (b, COL_CV)),
                  cache_spec, cache_spec,
                  _layer_spec(params["na_bias"], layer)],
        out_specs=pl.BlockSpec((qrows, BRANCH_W), lambda b, r: (b * steps + r, 0)),
        out_shape=jax.ShapeDtypeStruct((batch * seq, BRANCH_W), F32),
        compiler_params=_params(2),
        name="na",
    )(proj, proj, proj, cache_k, cache_v, params["na_bias"])


def _split3(x):
    hi = x.astype(BF16)
    r = x - hi.astype(F32)
    mid = r.astype(BF16)
    lo = (r - mid.astype(F32)).astype(BF16)
    return hi, mid, lo


def _dn_kernel(q_ref, k_ref, v_ref, z_ref, sl_ref, cw_ref, alog_ref, dtb_ref, ng_ref, s0_ref,
               o_ref, sfin_ref,
               xp_ref, qn_ref, kn_ref, vn_ref, col_ref, u_ref, wq_ref, kd_ref, in_ref, gl_ref, acc_ref,
               *, seq, nhs):
    c_len = DN_TILE
    n_chunks = seq // c_len

    zero8 = jnp.zeros((SUBLANES, LANES), F32)
    xp_ref[0:SUBLANES, :] = zero8
    xp_ref[SUBLANES + seq:2 * SUBLANES + seq, :] = zero8
    for hh in range(nhs):
        lanes = slice(hh * LANES, (hh + 1) * LANES)
        for j, (src, dst) in enumerate(((q_ref, qn_ref), (k_ref, kn_ref), (v_ref, vn_ref))):
            xp_ref[SUBLANES:SUBLANES + seq, :] = src[:, lanes]
            cw = cw_ref[j][:, lanes]
            y = cw[0:1] * xp_ref[pl.ds(SUBLANES - 2, seq), :]
            for t in range(1, CONV_W):
                y = y + cw[t:t + 1] * xp_ref[pl.ds(SUBLANES - 2 + t, seq), :]
            y = y * _sigmoid(y)
            if j < 2:
                y = y * lax.rsqrt(jnp.sum(y * y, axis=-1, keepdims=True) + EPS)
            if j == 0:
                y = y * (DN_DK ** -0.5)
            dst[hh] = y

    sl = sl_ref[...]
    lane = lax.broadcasted_iota(jnp.int32, (seq, LANES), 1)
    beta_all = _sigmoid(sl)
    xs = sl + dtb_ref[...]
    softplus = jnp.maximum(xs, 0.0) + jnp.log1p(jnp.exp(-jnp.abs(xs)))
    g_all = -jnp.exp(alog_ref[...]) * softplus
    for hh in range(nhs):
        head = pl.program_id(1) * nhs + hh
        cols = jnp.zeros((seq, LANES), F32)
        for dr in range(2):
            beta = jnp.sum(jnp.where(lane == dr * DN_HEADS + head, beta_all, 0.0), axis=-1, keepdims=True)
            g = jnp.sum(jnp.where(lane == 2 * DN_HEADS + dr * DN_HEADS + head, g_all, 0.0),
                        axis=-1, keepdims=True)
            cols = jnp.where(lane == dr, beta, cols)
            cols = jnp.where(lane == 2 + dr, g, cols)
        col_ref[hh] = cols
        acc_ref[hh] = jnp.zeros((seq, LANES), F32)

    ii = lax.broadcasted_iota(jnp.int32, (c_len, c_len), 0)
    jj = lax.broadcasted_iota(jnp.int32, (c_len, c_len), 1)
    eye = jnp.where(ii == jj, 1.0, 0.0).astype(F32)
    causal = (ii >= jj, ii <= jj)
    strict = (ii > jj, ii < jj)
    tri16 = tuple(jnp.where(m, 1.0, 0.0).astype(BF16) for m in causal)
    level_masks = []
    for dr in range(2):
        hi_idx, lo_idx = (ii, jj) if dr == 0 else (jj, ii)
        masks = []
        lvl = 0
        while (1 << lvl) < c_len:
            masks.append(((hi_idx >> (lvl + 1)) == (lo_idx >> (lvl + 1)))
                         & ((hi_idx >> lvl) == (lo_idx >> lvl) + 1))
            lvl += 1
        level_masks.append(masks)

    group = max(1, min(DN_CHAINS // (2 * nhs), n_chunks))
    units = [(hh, t) for hh in range(nhs) for t in range(group)]
    chains = [(m, dr) for m in range(len(units)) for dr in range(2)]
    lane_sq = jj

    def group_body(i, carry):
        cs = [i * group + t for hh, t in units]
        rows = [pl.ds(pl.multiple_of(c * c_len, c_len), c_len) for c in cs]
        q = [qn_ref[hh, rows[m], :] for m, (hh, t) in enumerate(units)]
        k = [kn_ref[hh, rows[m], :] for m, (hh, t) in enumerate(units)]
        v = [vn_ref[hh, rows[m], :] for m, (hh, t) in enumerate(units)]
        blk = [col_ref[hh, rows[m], :] for m, (hh, t) in enumerate(units)]
        q16 = [x.astype(BF16) for x in q]
        k16 = [x.astype(BF16) for x in k]
        qk = [_dot_nt(a, b) for a, b in zip(q16, k16)]
        kk = [_dot_nt(b, b) for b in k16]
        parts = [_split3(x) for x in blk]
        cum = [jnp.where(lane_sq == 2, sum(_dot(tri16[0], p) for p in ps), sum(_dot(tri16[1], p) for p in ps))
               for ps in parts]
        cum_t = [x.T for x in cum]
        gc_col = [cum[m][:, 2 + dr:3 + dr] for m, dr in chains]
        gc_row = [cum_t[m][2 + dr:3 + dr, :] for m, dr in chains]
        beta = [blk[m][:, dr:dr + 1] for m, dr in chains]
        decay = [jnp.where(causal[dr], jnp.exp(jnp.where(causal[dr], gc_col[n] - gc_row[n], 0.0)), 0.0)
                 for n, (m, dr) in enumerate(chains)]
        lm = [jnp.where(strict[dr], (beta[n] * kk[m]) * decay[n], 0.0) for n, (m, dr) in enumerate(chains)]
        lm16 = [x.astype(BF16) for x in lm]
        zero16 = jnp.zeros((c_len, c_len), BF16)
        xs = [eye - jnp.where(level_masks[dr][0], lm[n], 0.0) for n, (m, dr) in enumerate(chains)]
        for lvl in range(1, len(level_masks[0])):
            ys = [_dot(jnp.where(level_masks[dr][lvl], lm16[n], zero16), xs[n].astype(BF16))
                  for n, (m, dr) in enumerate(chains)]
            xs = [xs[n] - _dot(xs[n].astype(BF16), ys[n].astype(BF16)) for n in range(len(chains))]
        tmat = [x.astype(BF16) for x in xs]
        eg = [jnp.exp(x) for x in gc_col]
        g_last = [gc_col[n][c_len - 1:c_len] if dr == 0 else gc_col[n][0:1] for n, (m, dr) in enumerate(chains)]
        us = [_dot(tmat[n], (v[m] * beta[n]).astype(BF16)) for n, (m, dr) in enumerate(chains)]
        ws = [_dot(tmat[n], (k[m] * (beta[n] * eg[n])).astype(BF16)) for n, (m, dr) in enumerate(chains)]
        for n, (m, dr) in enumerate(chains):
            hh, c = units[m][0], cs[m]
            u_ref[dr, hh, rows[m], :] = us[n]
            wq_ref[dr, hh, pl.ds(pl.multiple_of(2 * c * c_len, c_len), c_len), :] = ws[n].astype(BF16)
            wq_ref[dr, hh, pl.ds(pl.multiple_of(2 * c * c_len + c_len, c_len), c_len), :] = (
                q[m] * eg[n]).astype(BF16)
            kd_ref[dr, hh, rows[m], :] = (k[m] * jnp.exp(g_last[n] - gc_col[n])).astype(BF16)
            in_ref[dr, hh, rows[m], :] = (qk[m] * decay[n]).astype(BF16)
            gl_ref[dr, hh, pl.ds(c, 1), :] = jnp.broadcast_to(jnp.exp(g_last[n]), (1, LANES))
        return carry

    lax.fori_loop(0, n_chunks // group, group_body, 0)

    seq_chains = [(dr, hh) for dr in range(2) for hh in range(nhs)]

    def step(i, states):
        cs = [i, n_chunks - 1 - i]
        rows = [pl.ds(pl.multiple_of(c * c_len, c_len), c_len) for c in cs]
        ws_qs = [_dot(wq_ref[dr, hh, pl.ds(pl.multiple_of(2 * cs[dr] * c_len, 2 * c_len), 2 * c_len), :],
                      states[n].astype(BF16)) for n, (dr, hh) in enumerate(seq_chains)]
        v16 = [(u_ref[dr, hh, rows[dr], :] - ws_qs[n][0:c_len]).astype(BF16)
               for n, (dr, hh) in enumerate(seq_chains)]
        intra = [_dot(in_ref[dr, hh, rows[dr], :], v16[n]) for n, (dr, hh) in enumerate(seq_chains)]
        upd = [_dot_tn(kd_ref[dr, hh, rows[dr], :], v16[n]) for n, (dr, hh) in enumerate(seq_chains)]
        for n, (dr, hh) in enumerate(seq_chains):
            acc_ref[hh, rows[dr], :] += ws_qs[n][c_len:2 * c_len] + intra[n]
        return tuple(states[n] * gl_ref[dr, hh, pl.ds(cs[dr], 1), :] + upd[n]
                     for n, (dr, hh) in enumerate(seq_chains))

    s_fin = lax.fori_loop(0, n_chunks, step, tuple(s0_ref[dr, hh] for dr, hh in seq_chains))
    for n, (dr, hh) in enumerate(seq_chains):
        sfin_ref[dr, hh] = s_fin[n]

    for hh in range(nhs):
        lanes = slice(hh * LANES, (hh + 1) * LANES)
        o = acc_ref[hh]
        y = o * lax.rsqrt(jnp.mean(o * o, axis=-1, keepdims=True) + EPS) * ng_ref[...]
        z = z_ref[:, lanes]
        o_ref[:, lanes] = y * (z * _sigmoid(z))


def _dn_call(proj, params, s0, s0_map, *, layer, batch, seq, nhs):
    nh = DN_HEADS
    nblk = nh // nhs
    width = nhs * LANES
    n_chunks = seq // DN_TILE
    vec = pl.BlockSpec((None, 1, LANES), lambda b, h: (layer, 0, 0))
    nsub = max(n_chunks, SUBLANES)
    return pl.pallas_call(
        functools.partial(_dn_kernel, seq=seq, nhs=nhs),
        grid=(batch, nblk),
        in_specs=[pl.BlockSpec((seq, width), lambda b, h: (b, COL_DQ * nblk + h)),
                  pl.BlockSpec((seq, width), lambda b, h: (b, (COL_DQ + 1) * nblk + h)),
                  pl.BlockSpec((seq, width), lambda b, h: (b, (COL_DQ + 2) * nblk + h)),
                  pl.BlockSpec((seq, width), lambda b, h: (b, COL_DZ * nblk + h)),
                  pl.BlockSpec((seq, LANES), lambda b, h: (b, COL_SC128)),
                  pl.BlockSpec((None, 3, CONV_W, width), lambda b, h: (layer, 0, 0, h)),
                  vec, vec, vec,
                  pl.BlockSpec((None, None, 2, nhs, DN_DK, DN_DK), s0_map)],
        out_specs=[pl.BlockSpec((seq, width), lambda b, h: (b, h)),
                   pl.BlockSpec((None, 2, nhs, DN_DK, DN_DK), lambda b, h: (b, 0, h, 0, 0))],
        out_shape=[jax.ShapeDtypeStruct((batch * seq, BRANCH_W), F32),
                   jax.ShapeDtypeStruct((batch, 2, nh, DN_DK, DN_DK), F32)],
        scratch_shapes=[pltpu.VMEM((seq + 2 * SUBLANES, LANES), F32),
                        pltpu.VMEM((nhs, seq, LANES), F32),
                        pltpu.VMEM((nhs, seq, LANES), F32),
                        pltpu.VMEM((nhs, seq, LANES), F32),
                        pltpu.VMEM((nhs, seq, LANES), F32),
                        pltpu.VMEM((2, nhs, seq, LANES), F32),
                        pltpu.VMEM((2, nhs, 2 * seq, LANES), BF16),
                        pltpu.VMEM((2, nhs, seq, LANES), BF16),
                        pltpu.VMEM((2, nhs, seq, DN_TILE), BF16),
                        pltpu.VMEM((2, nhs, nsub, LANES), F32),
                        pltpu.VMEM((nhs, seq, LANES), F32)],
        compiler_params=_params(2),
        name="deltanet",
    )(proj, proj, proj, proj, proj, params["dn_cw"], params["dn_alog"], params["dn_dtb"], params["dn_ng"], s0)


def _prepare_params(w_ffn_gate, w_ffn_up, w_ffn_down, w_in, lru_conv_w, lru_conv_b, lru_w_r, lru_b_r,
                    lru_w_i, lru_b_i, lru_lambda, gqa_q_norm, gqa_k_norm, na_rpb, dn_conv_w, dn_a_log,
                    dn_dt_bias, dn_norm_g, w_branch, w_out, norm_g):
    depth = w_in.shape[0]
    offs = np.cumsum((0,) + IN_WIDTHS)
    seg = [w_in[:, :, offs[i]:offs[i + 1]] for i in range(len(IN_WIDTHS))]
    (a_x, a_y, b_q, b_k, b_v, c_q, c_k, c_v, d_q, d_k, d_v, d_z, d_b, d_a, g_lin) = seg
    perm = np.asarray(GQA_PERM)
    b_q = b_q.reshape(depth, D_MODEL, GQA_HEADS, HEAD_DIM)[:, :, perm].reshape(depth, D_MODEL, BRANCH_W)
    pad = jnp.zeros((depth, D_MODEL, PROJ_W - 5120 - 2 * LANES - 16), F32)
    w_main = jnp.concatenate([a_x, a_y, b_q, c_q, c_k, c_v, d_q, d_k, d_v, d_z, b_k, b_v, d_b, d_a, pad],
                             axis=2).astype(BF16)

    def block_diag(wb):
        wb = wb.reshape(depth, 2, LRU_BLOCKS // 2, 2, LRU_BW, LRU_BW)
        z = jnp.zeros_like(wb[:, :, :, 0])
        return jnp.concatenate([jnp.concatenate([wb[:, :, :, 0], z], axis=-1),
                                jnp.concatenate([z, wb[:, :, :, 1]], axis=-1)], axis=-2)

    wr, wi = block_diag(lru_w_r), block_diag(lru_w_i)
    lru_wg = jnp.concatenate([wr[:, 0], wi[:, 0], wr[:, 1], wi[:, 1]], axis=-1).astype(BF16)
    ncol = BRANCH_W // LANES

    def col_blocks(v):
        return v.reshape(depth, ncol, LANES)

    lru_bg = jnp.concatenate([col_blocks(lru_b_r[:, 0]), col_blocks(lru_b_i[:, 0]),
                              col_blocks(lru_b_r[:, 1]), col_blocks(lru_b_i[:, 1])], axis=-1)[:, :, None, :]

    wb_b = w_branch[:, 1].reshape(depth, GQA_HEADS, HEAD_DIM, D_MODEL)[:, perm].reshape(depth, BRANCH_W, D_MODEL)
    wb = jnp.stack([w_branch[:, 0], wb_b, w_branch[:, 2], w_branch[:, 3]], axis=1).astype(BF16)

    lane_pad = jnp.zeros((depth, LANES - 4 * DN_HEADS), F32)
    lane_zero = jnp.zeros((depth, 2 * DN_HEADS), F32)
    alog = jnp.concatenate([lane_zero, dn_a_log.reshape(depth, -1), lane_pad], axis=1)[:, None, :]
    dtb = jnp.concatenate([lane_zero, dn_dt_bias.reshape(depth, -1), lane_pad], axis=1)[:, None, :]

    return dict(
        norm_g=norm_g[:, :, None, :],
        wg=w_ffn_gate.astype(BF16), wu=w_ffn_up.astype(BF16), wd=w_ffn_down.astype(BF16),
        w_main=w_main, w_gate=g_lin.astype(BF16),
        lru_cw=lru_conv_w, lru_cb=lru_conv_b[:, None, :], lru_wg=lru_wg, lru_bg=lru_bg, lru_lam=lru_lambda,
        gq=jnp.tile(gqa_q_norm, (1, 2))[:, None, :], gk=jnp.tile(gqa_k_norm, (1, 2))[:, None, :],
        na_bias=_na_bias_table(na_rpb),
        dn_cw=dn_conv_w.reshape(depth, CONV_W, 3, BRANCH_W).transpose(0, 2, 1, 3),
        dn_alog=alog, dn_dtb=dtb, dn_ng=dn_norm_g[:, None, :],
        wb=wb, w_out=w_out.astype(BF16),
    )


def _rope_tables(seq):
    pos = jnp.arange(seq)
    half = HEAD_DIM // 2
    inv = jnp.power(ROPE_BASE, -jnp.arange(0, half, 2, dtype=F32) / half)
    ang_r = (pos // GRID_W).astype(F32)[:, None] * inv[None, :]
    ang_c = (pos % GRID_W).astype(F32)[:, None] * inv[None, :]
    cos = jnp.concatenate([jnp.cos(ang_r)] * 2 + [jnp.cos(ang_c)] * 2, axis=-1)
    sin = jnp.concatenate([-jnp.sin(ang_r), jnp.sin(ang_r), -jnp.sin(ang_c), jnp.sin(ang_c)], axis=-1)
    return jnp.tile(cos, (1, 2)), jnp.tile(sin, (1, 2))


def _na_bias_table(rpb):
    qc = np.arange(GRID_W)
    cs = np.clip(qc - NA_WIN_C // 2, 0, GRID_W - NA_WIN_C)
    kc = np.arange(GRID_W)
    inwin = (kc[None, :] >= cs[:, None]) & (kc[None, :] < cs[:, None] + NA_WIN_C)
    coff = kc[None, :] - qc[:, None] + NA_WIN_C - 1
    onehot = (coff[None] == np.arange(2 * NA_WIN_C - 1)[:, None, None]).astype(np.float32)
    t = jnp.einsum("lhrd,dqk->lhrqk", rpb.astype(F32), onehot, precision=lax.Precision.HIGHEST)
    t = jnp.where(inwin, t, NEG_BIG)
    edge = jnp.full_like(t[:, :, :1], NEG_BIG)
    t = jnp.concatenate([edge, t, edge], axis=2)
    return jnp.concatenate([t[:, :, :-1], t[:, :, 1:]], axis=-1)


def _layer(x, mod_all, params, *, batch, seq, latent, layer, caches, tables, final_g, tm):
    row0 = 1 if latent else 0
    x = _ffn_call(x, mod_all, params, final_g, layer=layer, which=0, row0=row0, final=False, tm=tm)
    proj = _inproj_call(x, mod_all, params, layer=layer, row0=row0, tm=tm)

    dn_heads_per_step = 1 if seq > 1024 else DN_HEADS
    if latent:
        cache_ak, cache_av, cache_nk, cache_nv, state_lru, state_delta = caches
        h0, h0_map = state_lru, lambda b, c: (b, layer, 0, c)
        s0, s0_map = state_delta, lambda b, h: (b, layer, 0, h, 0, 0)
    else:
        h0, h0_map = jnp.zeros((1, 1, 2, BRANCH_W), F32), lambda b, c: (0, 0, 0, c)
        s0 = jnp.zeros((1, 1, 2, dn_heads_per_step, DN_DK, DN_DK), F32)
        s0_map = lambda b, h: (0, 0, 0, 0, 0, 0)

    o_a, lru_fin = _lru_call(proj, h0, h0_map, params, layer=layer, batch=batch, seq=seq)

    cos, sin = tables["rope"] if latent else (None, None)
    qn, kn = _prep_call(proj, params, cos, sin, layer=layer, batch=batch, seq=seq, rope=latent)
    tq = min(seq, ATTN_QROWS)
    kv_new = (kn, pl.BlockSpec((seq, LANES), lambda b, i: (b, 0)),
              proj, pl.BlockSpec((seq, LANES), lambda b, i: (b, COL_BV128)))
    if latent:
        gqa_cache = pl.BlockSpec((None, None, PAST_LEN, LANES), lambda b, i: (b, layer, 0, 0))
        sources = [kv_new, (cache_ak, gqa_cache, cache_av, gqa_cache)]
    else:
        sources = [kv_new]
    o_b = _attn_call(qn, 0, sources, batch=batch, seq=seq, tq=tq, nkb=1, qscale=1.0)

    if latent:
        o_c = _na_call(proj, cache_nk, cache_nv, params, layer=layer, batch=batch, seq=seq)
    else:
        src = (proj, pl.BlockSpec((seq, BRANCH_W), lambda b, i: (b, COL_CK)),
               proj, pl.BlockSpec((seq, BRANCH_W), lambda b, i: (b, COL_CV)))
        o_c = _attn_call(proj, COL_CQ, [src], batch=batch, seq=seq, tq=tq, nkb=4, qscale=HEAD_DIM ** -0.5)

    o_d, dn_fin = _dn_call(proj, params, s0, s0_map, layer=layer, batch=batch, seq=seq, nhs=dn_heads_per_step)

    x = _merge_call(x, mod_all, params, (o_a, o_b, o_c, o_d), layer=layer, row0=row0, tm=tm)
    x = _ffn_call(x, mod_all, params, final_g, layer=layer, which=1, row0=row0,
                  final=(layer == DEPTH - 1), tm=tm)

    new_ctx = None
    if not latent:
        new_ctx = (kn.reshape(batch, seq, GQA_KV, HEAD_DIM),
                   proj[:, COL_BV128 * LANES:(COL_BV128 + 1) * LANES].reshape(batch, seq, GQA_KV, HEAD_DIM),
                   proj[:, COL_CK * BRANCH_W:(COL_CK + 1) * BRANCH_W].reshape(batch, seq, NA_HEADS, HEAD_DIM),
                   proj[:, COL_CV * BRANCH_W:(COL_CV + 1) * BRANCH_W].reshape(batch, seq, NA_HEADS, HEAD_DIM),
                   lru_fin, dn_fin)
    return x, new_ctx


def kernel(x_prompt, x_sample, c, cache_attn_k, cache_attn_v, cache_na_k, cache_na_v, state_lru, state_delta, c_ctx, w_mod, b_mod, norm_g, w_ffn_gate, w_ffn_up, w_ffn_down, w_in, lru_conv_w, lru_conv_b, lru_w_r, lru_b_r, lru_w_i, lru_b_i, lru_lambda, gqa_q_norm, gqa_k_norm, na_rpb, dn_conv_w, dn_a_log, dn_dt_bias, dn_norm_g, w_branch, w_out, final_norm_g):
    batch_c, seq_c, _ = x_prompt.shape
    batch_l, seq_l, _ = x_sample.shape
    assert batch_l + 1 <= SUBLANES

    cs = jnp.concatenate([c_ctx[None, :], c, jnp.zeros((SUBLANES - 1 - batch_l, D_MODEL), F32)], axis=0)
    mod_all = _mod_call(cs, w_mod, b_mod).reshape(DEPTH, SUBLANES, N_MOD, D_MODEL)
    params = _prepare_params(w_ffn_gate, w_ffn_up, w_ffn_down, w_in, lru_conv_w, lru_conv_b, lru_w_r, lru_b_r,
                             lru_w_i, lru_b_i, lru_lambda, gqa_q_norm, gqa_k_norm, na_rpb, dn_conv_w, dn_a_log,
                             dn_dt_bias, dn_norm_g, w_branch, w_out, norm_g)

    caches = (cache_attn_k.reshape(batch_l, DEPTH, PAST_LEN, GQA_KV * HEAD_DIM),
              cache_attn_v.reshape(batch_l, DEPTH, PAST_LEN, GQA_KV * HEAD_DIM),
              cache_na_k.reshape(batch_l, DEPTH, PAST_LEN, BRANCH_W),
              cache_na_v.reshape(batch_l, DEPTH, PAST_LEN, BRANCH_W),
              state_lru, state_delta)
    tables = {"rope": _rope_tables(seq_l)}
    final_g = final_norm_g[None, :]

    xc = x_prompt.reshape(1, batch_c * seq_c, D_MODEL)
    xl = x_sample
    ctx_out = []
    for l in range(DEPTH):
        xc, new_ctx = _layer(xc, mod_all, params, batch=batch_c, seq=seq_c, latent=False, layer=l, caches=None,
                             tables=tables, final_g=final_g, tm=DENSE_ROWS)
        ctx_out.append(new_ctx)
        xl, _ = _layer(xl, mod_all, params, batch=batch_l, seq=seq_l, latent=True, layer=l, caches=caches,
                       tables=tables, final_g=final_g, tm=DENSE_ROWS)

    stacked = [jnp.stack([ctx_out[l][i] for l in range(DEPTH)], axis=1) for i in range(6)]
    return (xc.reshape(batch_c, seq_c, D_MODEL), xl, *stacked)
```

```python
import functools

import numpy as np
import jax
import jax.numpy as jnp
from jax import lax
from jax.experimental import pallas as pl
from jax.experimental.pallas import tpu as pltpu

F32 = jnp.float32
BF16 = jnp.bfloat16

D_MODEL = 1024
DEPTH = 4
GRID_W = 64
N_BRANCH = 4
BRANCH_W = 512
N_MOD = 9
D_FF = 2816
EPS = 1e-6
CONV_W = 4
LRU_BLOCKS = 8
LRU_BW = 64
LRU_C = 8.0
HEAD_DIM = 64
GQA_HEADS = 8
GQA_KV = 2
ROPE_BASE = 10000.0
NA_HEADS = 8
NA_WIN_R = 8
NA_WIN_C = 16
NA_QROWS = 4
NA_KROWS = 12
DN_DK = 128
DN_HEADS = 4
LRU_PITCH_PAD = 8
DN_TILE = 128
DN_CHAINS = 16
PAST_LEN = 512
IN_WIDTHS = (512, 512, 512, 128, 128, 512, 512, 512, 512, 512, 512, 512, 8, 8, 4096)

LANES = 128
SUBLANES = 8
DENSE_ROWS = 512
DENSE_PARTS = 2
ATTN_QROWS = 512
PROJ_W = 5632
COL_BQ, COL_CQ, COL_CK, COL_CV, COL_DQ, COL_DZ = 2, 3, 4, 5, 6, 9
COL_BK128, COL_BV128, COL_SC128 = 40, 41, 42
VMEM_LIMIT = 56 * 1024 * 1024
NEG_BIG = -1e30
GQA_PERM = (0, 4, 1, 5, 2, 6, 3, 7)


def _params(n):
    return pltpu.CompilerParams(dimension_semantics=("arbitrary",) * n, vmem_limit_bytes=VMEM_LIMIT)


def _const_spec(shape):
    nd = len(shape)
    return pl.BlockSpec(shape, lambda *_: (0,) * nd, pipeline_mode=pl.Buffered(1))


def _layer_spec(arr, layer, *sub):
    tail = arr.shape[1 + len(sub):]
    index = (layer,) + tuple(sub) + (0,) * len(tail)
    return pl.BlockSpec((None,) * (1 + len(sub)) + tuple(tail), lambda *_: index, pipeline_mode=pl.Buffered(1))


def _mod_spec(layer, row0):
    return pl.BlockSpec((None, None, N_MOD, D_MODEL), lambda b, i: (layer, row0 + b, 0, 0))


def _dot(a, b):
    return jnp.dot(a, b, preferred_element_type=F32)


def _dot_nt(a, b):
    return lax.dot_general(a, b, (((1,), (1,)), ((), ())), preferred_element_type=F32)


def _dot_tn(a, b):
    return lax.dot_general(a, b, (((0,), (0,)), ((), ())), preferred_element_type=F32)


def _split(x):
    hi = x.astype(BF16)
    lo = (x - hi.astype(F32)).astype(BF16)
    return hi, lo


def _dot3(a, b):
    ah, al = _split(a)
    bh, bl = _split(b)
    return _dot(ah, bh) + (_dot(al, bh) + _dot(ah, bl))


def _sigmoid(x):
    return 0.5 * jnp.tanh(0.5 * x) + 0.5


def _modnorm(x, g, shift, scale):
    ms = jnp.mean(x * x, axis=-1, keepdims=True)
    return (x * lax.rsqrt(ms + EPS) * g) * (1.0 + scale) + shift


def _mod_kernel(c_ref, w_ref, b_ref, o_ref):
    c = c_ref[...]
    o_ref[0] = _dot3(c * _sigmoid(c), w_ref[0]) + b_ref[0]


def _mod_call(cs, w_mod, b_mod):
    tn = 1024
    n = N_MOD * D_MODEL
    return pl.pallas_call(
        _mod_kernel,
        grid=(DEPTH, n // tn),
        in_specs=[pl.BlockSpec((SUBLANES, D_MODEL), lambda l, j: (0, 0)),
                  pl.BlockSpec((1, D_MODEL, tn), lambda l, j: (l, 0, j)),
                  pl.BlockSpec((1, 1, tn), lambda l, j: (l, 0, j))],
        out_specs=pl.BlockSpec((1, SUBLANES, tn), lambda l, j: (l, 0, j)),
        out_shape=jax.ShapeDtypeStruct((DEPTH, SUBLANES, n), F32),
        compiler_params=_params(2),
        name="mod",
    )(cs, w_mod, b_mod.reshape(DEPTH, 1, n))


def _ffn_kernel(x_ref, mod_ref, g_ref, wg_ref, wu_ref, wd_ref, gf_ref, o_ref, *, mi, final):
    mod = mod_ref[...]
    rows = x_ref.shape[1]
    parts = [pl.ds(n * (rows // DENSE_PARTS), rows // DENSE_PARTS) for n in range(DENSE_PARTS)]
    xs = [x_ref[0, p, :] for p in parts]
    hs = [_modnorm(x, g_ref[...], mod[mi:mi + 1], mod[mi + 1:mi + 2]).astype(BF16) for x in xs]
    acts = []
    for h in hs:
        gt = _dot(h, wg_ref[...])
        up = _dot(h, wu_ref[...])
        acts.append((gt * _sigmoid(gt) * up).astype(BF16))
    downs = [_dot(a, wd_ref[...]) for a in acts]
    for p, x, d in zip(parts, xs, downs):
        y = x + 0.5 * mod[mi + 2:mi + 3] * d
        if final:
            ms = jnp.mean(y * y, axis=-1, keepdims=True)
            y = y * lax.rsqrt(ms + EPS) * gf_ref[...]
        o_ref[0, p, :] = y


def _ffn_call(x, mod_all, params, gf, *, layer, which, row0, final, tm):
    nb, rows, _ = x.shape
    return pl.pallas_call(
        functools.partial(_ffn_kernel, mi=6 * which, final=final),
        grid=(nb, rows // tm),
        in_specs=[pl.BlockSpec((1, tm, D_MODEL), lambda b, i: (b, i, 0)),
                  _mod_spec(layer, row0),
                  _layer_spec(params["norm_g"], layer, 2 * which),
                  _layer_spec(params["wg"], layer, which),
                  _layer_spec(params["wu"], layer, which),
                  _layer_spec(params["wd"], layer, which),
                  _const_spec((1, D_MODEL))],
        out_specs=pl.BlockSpec((1, tm, D_MODEL), lambda b, i: (b, i, 0)),
        out_shape=jax.ShapeDtypeStruct(x.shape, F32),
        compiler_params=_params(2),
        name="ffn",
    )(x, mod_all, params["norm_g"], params["wg"], params["wu"], params["wd"], gf)


def _inproj_kernel(x_ref, mod_ref, g_ref, w_ref, o_ref):
    mod = mod_ref[...]
    h = _modnorm(x_ref[0], g_ref[...], mod[3:4], mod[4:5]).astype(BF16)
    o_ref[...] = _dot(h, w_ref[...])


def _inproj_call(x, mod_all, params, *, layer, row0, tm):
    nb, rows, _ = x.shape
    nt = rows // tm
    return pl.pallas_call(
        _inproj_kernel,
        grid=(nb, nt),
        in_specs=[pl.BlockSpec((1, tm, D_MODEL), lambda b, i: (b, i, 0)),
                  _mod_spec(layer, row0),
                  _layer_spec(params["norm_g"], layer, 1),
                  _layer_spec(params["w_main"], layer)],
        out_specs=pl.BlockSpec((tm, PROJ_W), lambda b, i: (b * nt + i, 0)),
        out_shape=jax.ShapeDtypeStruct((nb * rows, PROJ_W), F32),
        compiler_params=_params(2),
        name="inproj",
    )(x, mod_all, params["norm_g"], params["w_main"])


def _merge_kernel(x_ref, mod_ref, g_ref, oa_ref, ob_ref, oc_ref, od_ref, wgate_ref, wb_ref, wout_ref, o_ref):
    x = x_ref[0]
    mod = mod_ref[...]
    h = _modnorm(x, g_ref[...], mod[3:4], mod[4:5]).astype(BF16)
    acc = None
    for n, ref in enumerate((oa_ref, ob_ref, oc_ref, od_ref)):
        gate = _sigmoid(_dot(h, wgate_ref[:, n * D_MODEL:(n + 1) * D_MODEL]))
        term = gate * _dot(ref[...].astype(BF16), wb_ref[n])
        acc = term if acc is None else acc + term
    o_ref[0] = x + mod[5:6] * _dot(acc.astype(BF16), wout_ref[...])


def _merge_call(x, mod_all, params, outs, *, layer, row0, tm):
    nb, rows, _ = x.shape
    nt = rows // tm
    ospec = pl.BlockSpec((tm, BRANCH_W), lambda b, i: (b * nt + i, 0))
    return pl.pallas_call(
        _merge_kernel,
        grid=(nb, nt),
        in_specs=[pl.BlockSpec((1, tm, D_MODEL), lambda b, i: (b, i, 0)),
                  _mod_spec(layer, row0),
                  _layer_spec(params["norm_g"], layer, 1),
                  ospec, ospec, ospec, ospec,
                  _layer_spec(params["w_gate"], layer),
                  _layer_spec(params["wb"], layer),
                  _layer_spec(params["w_out"], layer)],
        out_specs=pl.BlockSpec((1, tm, D_MODEL), lambda b, i: (b, i, 0)),
        out_shape=jax.ShapeDtypeStruct(x.shape, F32),
        compiler_params=_params(2),
        name="merge",
    )(x, mod_all, params["norm_g"], *outs, params["w_gate"], params["wb"], params["w_out"])


def _log_sigmoid(x):
    return jnp.minimum(x, 0.0) - jnp.log1p(jnp.exp(-jnp.abs(x)))


def _gelu_tanh(x):
    return x * (0.5 * (1.0 + jnp.tanh(0.7978845608028654 * (x + 0.044715 * (x * x * x)))))


def _lru_kernel(ax_ref, ay_ref, h0_ref, cw_ref, cb_ref, wg_ref, bg_ref, lam_ref, o_ref, fin_ref,
                xp_ref, a_ref, u_ref, h_ref, p_ref, *, seq):
    lc = seq // SUBLANES
    pitch = lc + LRU_PITCH_PAD
    zero8 = jnp.zeros((SUBLANES, LANES), F32)
    xp_ref[0:SUBLANES, :] = zero8
    xp_ref[SUBLANES + seq:2 * SUBLANES + seq, :] = zero8
    xp_ref[SUBLANES:SUBLANES + seq, :] = ax_ref[...]
    cw = cw_ref[...]
    xa = cb_ref[...] + cw[0:1] * xp_ref[pl.ds(SUBLANES - 2, seq), :]
    for k in range(1, CONV_W):
        xa = xa + cw[k:k + 1] * xp_ref[pl.ds(SUBLANES - 2 + k, seq), :]
    half = _dot(xa.astype(BF16), wg_ref[...]) + bg_ref[...]
    lam = lam_ref[...]
    xh = 0.5 * xa
    for dr in range(2):
        tr = jnp.tanh(half[:, (2 * dr) * LANES:(2 * dr + 1) * LANES])
        ti = jnp.tanh(half[:, (2 * dr + 1) * LANES:(2 * dr + 2) * LANES])
        c_half = (0.5 * LRU_C) * _log_sigmoid(lam[dr:dr + 1])
        log_a = tr * c_half + c_half
        a = jnp.exp(log_a)
        one_m_a2 = -jnp.tanh(log_a) * (a * a + 1.0)
        root = jnp.where(one_m_a2 > 0.0, one_m_a2 * lax.rsqrt(one_m_a2), 0.0)
        u = root * ((ti + 1.0) * xh)
        for k in range(SUBLANES):
            a_ref[dr, pl.ds(k * pitch, lc), :] = a[k * lc:(k + 1) * lc]
            u_ref[dr, pl.ds(k * pitch, lc), :] = u[k * lc:(k + 1) * lc]

    sub = lax.broadcasted_iota(jnp.int32, (SUBLANES, LANES), 0)
    h0 = h0_ref[...]
    hf0 = jnp.where(sub == 0, h0[0:1], 0.0)
    hb0 = jnp.where(sub == SUBLANES - 1, h0[1:2], 0.0)
    ones = jnp.ones((SUBLANES, LANES), F32)

    def body(s, carry):
        hf, pf, hb, pb = carry
        rows_f = pl.ds(s, SUBLANES, stride=pitch)
        rows_b = pl.ds(lc - 1 - s, SUBLANES, stride=pitch)
        af = a_ref[0, rows_f, :]
        hf = af * hf + u_ref[0, rows_f, :]
        pf = af * pf
        h_ref[0, rows_f, :] = hf
        p_ref[0, rows_f, :] = pf
        ab = a_ref[1, rows_b, :]
        hb = ab * hb + u_ref[1, rows_b, :]
        pb = ab * pb
        h_ref[1, rows_b, :] = hb
        p_ref[1, rows_b, :] = pb
        return hf, pf, hb, pb

    hf, pf, hb, pb = lax.fori_loop(0, lc, body, (hf0, ones, hb0, ones), unroll=8)

    cf = [jnp.zeros((1, LANES), F32)]
    for k in range(1, SUBLANES):
        cf.append(hf[k - 1:k] + pf[k - 1:k] * cf[k - 1])
    fin_ref[0:1, :] = hf[SUBLANES - 1:SUBLANES] + pf[SUBLANES - 1:SUBLANES] * cf[SUBLANES - 1]
    cb = [None] * SUBLANES
    cb[SUBLANES - 1] = jnp.zeros((1, LANES), F32)
    for k in range(SUBLANES - 2, -1, -1):
        cb[k] = hb[k + 1:k + 2] + pb[k + 1:k + 2] * cb[k + 1]
    fin_ref[1:2, :] = hb[0:1] + pb[0:1] * cb[0]

    for k in range(SUBLANES):
        rows = pl.ds(k * lc, lc)
        held = pl.ds(k * pitch, lc)
        h = (h_ref[0, held, :] + p_ref[0, held, :] * cf[k]) + (h_ref[1, held, :] + p_ref[1, held, :] * cb[k])
        o_ref[rows, :] = h * _gelu_tanh(ay_ref[rows, :])


def _lru_call(proj, h0, h0_map, params, *, layer, batch, seq):
    ncol = BRANCH_W // LANES
    pad_rows = SUBLANES * LRU_PITCH_PAD
    return pl.pallas_call(
        functools.partial(_lru_kernel, seq=seq),
        grid=(batch, ncol),
        in_specs=[pl.BlockSpec((seq, LANES), lambda b, c: (b, c)),
                  pl.BlockSpec((seq, LANES), lambda b, c: (b, ncol + c)),
                  pl.BlockSpec((None, None, 2, LANES), h0_map),
                  pl.BlockSpec((None, CONV_W, LANES), lambda b, c: (layer, 0, c)),
                  pl.BlockSpec((None, 1, LANES), lambda b, c: (layer, 0, c)),
                  pl.BlockSpec((None, None, LANES, 4 * LANES), lambda b, c: (layer, c, 0, 0)),
                  pl.BlockSpec((None, None, 1, 4 * LANES), lambda b, c: (layer, c, 0, 0)),
                  pl.BlockSpec((None, 2, LANES), lambda b, c: (layer, 0, c))],
        out_specs=[pl.BlockSpec((seq, LANES), lambda b, c: (b, c)),
                   pl.BlockSpec((None, 2, LANES), lambda b, c: (b, 0, c))],
        out_shape=[jax.ShapeDtypeStruct((batch * seq, BRANCH_W), F32),
                   jax.ShapeDtypeStruct((batch, 2, BRANCH_W), F32)],
        scratch_shapes=[pltpu.VMEM((seq + 2 * SUBLANES, LANES), F32),
                        pltpu.VMEM((2, seq + pad_rows, LANES), F32),
                        pltpu.VMEM((2, seq + pad_rows, LANES), F32),
                        pltpu.VMEM((2, seq + pad_rows, LANES), F32),
                        pltpu.VMEM((2, seq + pad_rows, LANES), F32)],
        compiler_params=_params(2),
        name="lru",
    )(proj, proj, h0, params["lru_cw"], params["lru_cb"], params["lru_wg"], params["lru_bg"], params["lru_lam"])


def _prep_kernel(*refs, rope):
    if rope:
        q_ref, k_ref, gq_ref, gk_ref, cos_ref, sin_ref, qn_ref, kn_ref = refs
    else:
        q_ref, k_ref, gq_ref, gk_ref, qn_ref, kn_ref = refs
    rows = q_ref.shape[0]
    lane = lax.broadcasted_iota(jnp.int32, (rows, LANES), 1)
    lo = lane < HEAD_DIM
    first16 = (lane & 16) == 0

    def head_norm(x, g):
        sq = x * x
        s_lo = jnp.sum(jnp.where(lo, sq, 0.0), axis=-1, keepdims=True)
        s_hi = jnp.sum(jnp.where(lo, 0.0, sq), axis=-1, keepdims=True)
        ms = jnp.where(lo, s_lo, s_hi) * (1.0 / HEAD_DIM)
        return x * lax.rsqrt(ms + EPS) * g

    def rotate(y):
        if not rope:
            return y
        partner = jnp.where(first16, pltpu.roll(y, LANES - 16, 1), pltpu.roll(y, 16, 1))
        return y * cos_ref[...] + partner * sin_ref[...]

    for p in range(BRANCH_W // LANES):
        cols = slice(p * LANES, (p + 1) * LANES)
        qn_ref[:, cols] = rotate(head_norm(q_ref[:, cols], gq_ref[...])) * (HEAD_DIM ** -0.5)
    kn_ref[...] = rotate(head_norm(k_ref[...], gk_ref[...]))


def _prep_call(proj, params, cos, sin, *, layer, batch, seq, rope):
    tab = pl.BlockSpec((seq, LANES), lambda b: (0, 0))
    in_specs = [pl.BlockSpec((seq, BRANCH_W), lambda b: (b, COL_BQ)),
                pl.BlockSpec((seq, LANES), lambda b: (b, COL_BK128)),
                _layer_spec(params["gq"], layer), _layer_spec(params["gk"], layer)]
    args = [proj, proj, params["gq"], params["gk"]]
    if rope:
        in_specs += [tab, tab]
        args += [cos, sin]
    return pl.pallas_call(
        functools.partial(_prep_kernel, rope=rope),
        grid=(batch,),
        in_specs=in_specs,
        out_specs=[pl.BlockSpec((seq, BRANCH_W), lambda b: (b, 0)),
                   pl.BlockSpec((seq, LANES), lambda b: (b, 0))],
        out_shape=[jax.ShapeDtypeStruct((batch * seq, BRANCH_W), F32),
                   jax.ShapeDtypeStruct((batch * seq, LANES), F32)],
        compiler_params=_params(1),
        name="attn_prep",
    )(*args)


def _softmax_pv(scores, values):
    m = None
    for s in scores:
        sm = jnp.max(s, axis=-1, keepdims=True)
        m = sm if m is None else jnp.maximum(m, sm)
    den = None
    out = None
    for s, v in zip(scores, values):
        p = jnp.exp(s - m)
        ps = jnp.sum(p, axis=-1, keepdims=True)
        den = ps if den is None else den + ps
        o = _dot(p.astype(BF16), v)
        out = o if out is None else out + o
    return out / den


def _attn_kernel(*refs, nsrc, nkb, qscale):
    q_ref = refs[0]
    src = refs[1:1 + 2 * nsrc]
    o_ref = refs[1 + 2 * nsrc]
    tq = q_ref.shape[0]
    lo = lax.broadcasted_iota(jnp.int32, (tq, LANES), 1) < HEAD_DIM

    loaded = {}

    def kv(p):
        col = p if nkb != 1 else 0
        if col not in loaded:
            cols = slice(col * LANES, (col + 1) * LANES)
            loaded[col] = ([src[2 * i][:, cols].astype(BF16) for i in range(nsrc)],
                           [src[2 * i + 1][:, cols].astype(BF16) for i in range(nsrc)])
        return loaded[col]

    def scores(unit):
        p, half = unit
        qb = q_ref[:, p * LANES:(p + 1) * LANES]
        if qscale != 1.0:
            qb = qb * qscale
        qm = jnp.where(lo if half == 0 else jnp.logical_not(lo), qb, 0.0).astype(BF16)
        return [_dot_nt(qm, k) for k in kv(p)[0]]

    units = [(p, half) for p in range(BRANCH_W // LANES) for half in range(2)]
    pending = scores(units[0])
    first_half = None
    for n, (p, half) in enumerate(units):
        current = pending
        if n + 1 < len(units):
            pending = scores(units[n + 1])
        out = _softmax_pv(current, kv(p)[1])
        if half == 0:
            first_half = out
        else:
            o_ref[:, p * LANES:(p + 1) * LANES] = jnp.where(lo, first_half, out)


def _attn_call(q_arr, q_col, sources, *, batch, seq, tq, nkb, qscale):
    nq = seq // tq
    in_specs = [pl.BlockSpec((tq, BRANCH_W), lambda b, i: (b * nq + i, q_col))]
    args = [q_arr]
    for k_arr, k_spec, v_arr, v_spec in sources:
        in_specs += [k_spec, v_spec]
        args += [k_arr, v_arr]
    return pl.pallas_call(
        functools.partial(_attn_kernel, nsrc=len(sources), nkb=nkb, qscale=qscale),
        grid=(batch, nq),
        in_specs=in_specs,
        out_specs=pl.BlockSpec((tq, BRANCH_W), lambda b, i: (b * nq + i, 0)),
        out_shape=jax.ShapeDtypeStruct((batch * seq, BRANCH_W), F32),
        compiler_params=_params(2),
        name="attn",
    )(*args)


def _na_row_start(r, rows):
    return jnp.clip(r - NA_WIN_R // 2, 0, rows - NA_WIN_R)


def _na_kernel(q_ref, k_ref, v_ref, ck_ref, cv_ref, bias_ref, o_ref, *, rows):
    r0 = pl.program_id(1) * NA_QROWS
    w0 = jnp.clip(r0 - NA_WIN_R // 2, 0, rows - NA_KROWS)
    win = pl.ds(pl.multiple_of(w0 * GRID_W, GRID_W), NA_KROWS * GRID_W)
    lo = lax.broadcasted_iota(jnp.int32, (NA_QROWS * GRID_W, LANES), 1) < HEAD_DIM
    lo_row = lax.broadcasted_iota(jnp.int32, (GRID_W, LANES), 1) < HEAD_DIM
    ncol = BRANCH_W // LANES
    scores = []
    for p in range(ncol):
        cols = slice(p * LANES, (p + 1) * LANES)
        qb = q_ref[:, cols] * (HEAD_DIM ** -0.5)
        kw = k_ref[win, cols].astype(BF16)
        kc = ck_ref[:, cols].astype(BF16)
        for half in range(2):
            qm = jnp.where(lo if half == 0 else jnp.logical_not(lo), qb, 0.0).astype(BF16)
            scores.append((_dot_nt(qm, kw), _dot_nt(qm, kc)))
    pair_index, pair_mask = [], []
    for j in range(NA_QROWS):
        r = r0 + j
        rs = _na_row_start(r, rows)
        idx_j, mask_j = [], []
        for t in range(NA_KROWS // 2):
            kr = w0 + 2 * t
            rel = kr - r + NA_WIN_R - 1
            idx_j.append(jnp.clip(rel + 1, 0, 2 * NA_WIN_R - 1))
            in_a = jnp.where((kr >= rs) & (kr < rs + NA_WIN_R), 0.0, NEG_BIG)
            in_b = jnp.where((kr + 1 >= rs) & (kr + 1 < rs + NA_WIN_R), 0.0, NEG_BIG)
            mask_j.append(jnp.where(lo_row, in_a, in_b))
        pair_index.append(idx_j)
        pair_mask.append(mask_j)
    probs = []
    for h, (s_loc, s_ctx) in enumerate(scores):
        bias = jnp.concatenate(
            [jnp.concatenate([bias_ref[h, pair_index[j][t]] + pair_mask[j][t] for t in range(NA_KROWS // 2)],
                             axis=-1) for j in range(NA_QROWS)], axis=0)
        s_loc = s_loc + bias
        m = jnp.maximum(jnp.max(s_loc, axis=-1, keepdims=True), jnp.max(s_ctx, axis=-1, keepdims=True))
        p_loc = jnp.exp(s_loc - m)
        p_ctx = jnp.exp(s_ctx - m)
        den = jnp.sum(p_loc, axis=-1, keepdims=True) + jnp.sum(p_ctx, axis=-1, keepdims=True)
        probs.append((p_loc.astype(BF16), p_ctx.astype(BF16), den))
    for p in range(ncol):
        cols = slice(p * LANES, (p + 1) * LANES)
        vw = v_ref[win, cols].astype(BF16)
        vc = cv_ref[:, cols].astype(BF16)
        halves = []
        for half in range(2):
            p_loc, p_ctx, den = probs[2 * p + half]
            halves.append((_dot(p_loc, vw) + _dot(p_ctx, vc)) / den)
        o_ref[:, cols] = jnp.where(lo, halves[0], halves[1])


def _na_call(proj, cache_k, cache_v, params, *, layer, batch, seq):
    rows = seq // GRID_W
    steps = rows // NA_QROWS
    qrows = NA_QROWS * GRID_W
    cache_spec = pl.BlockSpec((None, None, PAST_LEN, BRANCH_W), lambda b, r: (b, layer, 0, 0))
    return pl.pallas_call(
        functools.partial(_na_kernel, rows=rows),
        grid=(batch, steps),
        in_specs=[pl.BlockSpec((qrows, BRANCH_W), lambda b, r: (b * steps + r, COL_CQ)),
                  pl.BlockSpec((seq, BRANCH_W), lambda b, r: (b, COL_CK)),
                  pl.BlockSpec((seq, BRANCH_W), lambda b, r: (b, COL_CV)),
                  cache_spec, cache_spec,
                  _layer_spec(params["na_bias"], layer)],
        out_specs=pl.BlockSpec((qrows, BRANCH_W), lambda b, r: (b * steps + r, 0)),
        out_shape=jax.ShapeDtypeStruct((batch * seq, BRANCH_W), F32),
        compiler_params=_params(2),
        name="na",
    )(proj, proj, proj, cache_k, cache_v, params["na_bias"])


def _split3(x):
    hi = x.astype(BF16)
    r = x - hi.astype(F32)
    mid = r.astype(BF16)
    lo = (r - mid.astype(F32)).astype(BF16)
    return hi, mid, lo


def _dn_kernel(q_ref, k_ref, v_ref, z_ref, sl_ref, cw_ref, alog_ref, dtb_ref, ng_ref, s0_ref,
               o_ref, sfin_ref,
               xp_ref, qn_ref, kn_ref, vn_ref, col_ref, u_ref, wq_ref, kd_ref, in_ref, gl_ref, acc_ref,
               *, seq, nhs):
    c_len = DN_TILE
    n_chunks = seq // c_len

    zero8 = jnp.zeros((SUBLANES, LANES), F32)
    xp_ref[0:SUBLANES, :] = zero8
    xp_ref[SUBLANES + seq:2 * SUBLANES + seq, :] = zero8
    for hh in range(nhs):
        lanes = slice(hh * LANES, (hh + 1) * LANES)
        for j, (src, dst) in enumerate(((q_ref, qn_ref), (k_ref, kn_ref), (v_ref, vn_ref))):
            xp_ref[SUBLANES:SUBLANES + seq, :] = src[:, lanes]
            cw = cw_ref[j][:, lanes]
            y = cw[0:1] * xp_ref[pl.ds(SUBLANES - 2, seq), :]
            for t in range(1, CONV_W):
                y = y + cw[t:t + 1] * xp_ref[pl.ds(SUBLANES - 2 + t, seq), :]
            y = y * _sigmoid(y)
            if j < 2:
                y = y * lax.rsqrt(jnp.sum(y * y, axis=-1, keepdims=True) + EPS)
            if j == 0:
                y = y * (DN_DK ** -0.5)
            dst[hh] = y

    sl = sl_ref[...]
    lane = lax.broadcasted_iota(jnp.int32, (seq, LANES), 1)
    beta_all = _sigmoid(sl)
    xs = sl + dtb_ref[...]
    softplus = jnp.maximum(xs, 0.0) + jnp.log1p(jnp.exp(-jnp.abs(xs)))
    g_all = -jnp.exp(alog_ref[...]) * softplus
    for hh in range(nhs):
        head = pl.program_id(1) * nhs + hh
        cols = jnp.zeros((seq, LANES), F32)
        for dr in range(2):
            beta = jnp.sum(jnp.where(lane == dr * DN_HEADS + head, beta_all, 0.0), axis=-1, keepdims=True)
            g = jnp.sum(jnp.where(lane == 2 * DN_HEADS + dr * DN_HEADS + head, g_all, 0.0),
                        axis=-1, keepdims=True)
            cols = jnp.where(lane == dr, beta, cols)
            cols = jnp.where(lane == 2 + dr, g, cols)
        col_ref[hh] = cols
        acc_ref[hh] = jnp.zeros((seq, LANES), F32)

    ii = lax.broadcasted_iota(jnp.int32, (c_len, c_len), 0)
    jj = lax.broadcasted_iota(jnp.int32, (c_len, c_len), 1)
    eye = jnp.where(ii == jj, 1.0, 0.0).astype(F32)
    causal = (ii >= jj, ii <= jj)
    strict = (ii > jj, ii < jj)
    tri16 = tuple(jnp.where(m, 1.0, 0.0).astype(BF16) for m in causal)
    level_masks = []
    for dr in range(2):
        hi_idx, lo_idx = (ii, jj) if dr == 0 else (jj, ii)
        masks = []
        lvl = 0
        while (1 << lvl) < c_len:
            masks.append(((hi_idx >> (lvl + 1)) == (lo_idx >> (lvl + 1)))
                         & ((hi_idx >> lvl) == (lo_idx >> lvl) + 1))
            lvl += 1
        level_masks.append(masks)

    group = max(1, min(DN_CHAINS // (2 * nhs), n_chunks))
    units = [(hh, t) for hh in range(nhs) for t in range(group)]
    chains = [(m, dr) for m in range(len(units)) for dr in range(2)]
    lane_sq = jj

    def group_body(i, carry):
        cs = [i * group + t for hh, t in units]
        rows = [pl.ds(pl.multiple_of(c * c_len, c_len), c_len) for c in cs]
        q = [qn_ref[hh, rows[m], :] for m, (hh, t) in enumerate(units)]
        k = [kn_ref[hh, rows[m], :] for m, (hh, t) in enumerate(units)]
        v = [vn_ref[hh, rows[m], :] for m, (hh, t) in enumerate(units)]
        blk = [col_ref[hh, rows[m], :] for m, (hh, t) in enumerate(units)]
        q16 = [x.astype(BF16) for x in q]
        k16 = [x.astype(BF16) for x in k]
        qk = [_dot_nt(a, b) for a, b in zip(q16, k16)]
        kk = [_dot_nt(b, b) for b in k16]
        parts = [_split3(x) for x in blk]
        cum = [jnp.where(lane_sq == 2, sum(_dot(tri16[0], p) for p in ps), sum(_dot(tri16[1], p) for p in ps))
               for ps in parts]
        cum_t = [x.T for x in cum]
        gc_col = [cum[m][:, 2 + dr:3 + dr] for m, dr in chains]
        gc_row = [cum_t[m][2 + dr:3 + dr, :] for m, dr in chains]
        beta = [blk[m][:, dr:dr + 1] for m, dr in chains]
        decay = [jnp.where(causal[dr], jnp.exp(jnp.where(causal[dr], gc_col[n] - gc_row[n], 0.0)), 0.0)
                 for n, (m, dr) in enumerate(chains)]
        lm = [jnp.where(strict[dr], (beta[n] * kk[m]) * decay[n], 0.0) for n, (m, dr) in enumerate(chains)]
        lm16 = [x.astype(BF16) for x in lm]
        zero16 = jnp.zeros((c_len, c_len), BF16)
        xs = [eye - jnp.where(level_masks[dr][0], lm[n], 0.0) for n, (m, dr) in enumerate(chains)]
        for lvl in range(1, len(level_masks[0])):
            ys = [_dot(jnp.where(level_masks[dr][lvl], lm16[n], zero16), xs[n].astype(BF16))
                  for n, (m, dr) in enumerate(chains)]
            xs = [xs[n] - _dot(xs[n].astype(BF16), ys[n].astype(BF16)) for n in range(len(chains))]
        tmat = [x.astype(BF16) for x in xs]
        eg = [jnp.exp(x) for x in gc_col]
        g_last = [gc_col[n][c_len - 1:c_len] if dr == 0 else gc_col[n][0:1] for n, (m, dr) in enumerate(chains)]
        us = [_dot(tmat[n], (v[m] * beta[n]).astype(BF16)) for n, (m, dr) in enumerate(chains)]
        ws = [_dot(tmat[n], (k[m] * (beta[n] * eg[n])).astype(BF16)) for n, (m, dr) in enumerate(chains)]
        for n, (m, dr) in enumerate(chains):
            hh, c = units[m][0], cs[m]
            u_ref[dr, hh, rows[m], :] = us[n]
            wq_ref[dr, hh, pl.ds(pl.multiple_of(2 * c * c_len, c_len), c_len), :] = ws[n].astype(BF16)
            wq_ref[dr, hh, pl.ds(pl.multiple_of(2 * c * c_len + c_len, c_len), c_len), :] = (
                q[m] * eg[n]).astype(BF16)
            kd_ref[dr, hh, rows[m], :] = (k[m] * jnp.exp(g_last[n] - gc_col[n])).astype(BF16)
            in_ref[dr, hh, rows[m], :] = (qk[m] * decay[n]).astype(BF16)
            gl_ref[dr, hh, pl.ds(c, 1), :] = jnp.broadcast_to(jnp.exp(g_last[n]), (1, LANES))
        return carry

    lax.fori_loop(0, n_chunks // group, group_body, 0)

    seq_chains = [(dr, hh) for dr in range(2) for hh in range(nhs)]

    def step(i, states):
        cs = [i, n_chunks - 1 - i]
        rows = [pl.ds(pl.multiple_of(c * c_len, c_len), c_len) for c in cs]
        ws_qs = [_dot(wq_ref[dr, hh, pl.ds(pl.multiple_of(2 * cs[dr] * c_len, 2 * c_len), 2 * c_len), :],
                      states[n].astype(BF16)) for n, (dr, hh) in enumerate(seq_chains)]
        v16 = [(u_ref[dr, hh, rows[dr], :] - ws_qs[n][0:c_len]).astype(BF16)
               for n, (dr, hh) in enumerate(seq_chains)]
        intra = [_dot(in_ref[dr, hh, rows[dr], :], v16[n]) for n, (dr, hh) in enumerate(seq_chains)]
        upd = [_dot_tn(kd_ref[dr, hh, rows[dr], :], v16[n]) for n, (dr, hh) in enumerate(seq_chains)]
        for n, (dr, hh) in enumerate(seq_chains):
            acc_ref[hh, rows[dr], :] += ws_qs[n][c_len:2 * c_len] + intra[n]
        return tuple(states[n] * gl_ref[dr, hh, pl.ds(cs[dr], 1), :] + upd[n]
                     for n, (dr, hh) in enumerate(seq_chains))

    s_fin = lax.fori_loop(0, n_chunks, step, tuple(s0_ref[dr, hh] for dr, hh in seq_chains))
    for n, (dr, hh) in enumerate(seq_chains):
        sfin_ref[dr, hh] = s_fin[n]

    for hh in range(nhs):
        lanes = slice(hh * LANES, (hh + 1) * LANES)
        o = acc_ref[hh]
        y = o * lax.rsqrt(jnp.mean(o * o, axis=-1, keepdims=True) + EPS) * ng_ref[...]
        z = z_ref[:, lanes]
        o_ref[:, lanes] = y * (z * _sigmoid(z))


def _dn_call(proj, params, s0, s0_map, *, layer, batch, seq, nhs):
    nh = DN_HEADS
    nblk = nh // nhs
    width = nhs * LANES
    n_chunks = seq // DN_TILE
    vec = pl.BlockSpec((None, 1, LANES), lambda b, h: (layer, 0, 0))
    nsub = max(n_chunks, SUBLANES)
    return pl.pallas_call(
        functools.partial(_dn_kernel, seq=seq, nhs=nhs),
        grid=(batch, nblk),
        in_specs=[pl.BlockSpec((seq, width), lambda b, h: (b, COL_DQ * nblk + h)),
                  pl.BlockSpec((seq, width), lambda b, h: (b, (COL_DQ + 1) * nblk + h)),
                  pl.BlockSpec((seq, width), lambda b, h: (b, (COL_DQ + 2) * nblk + h)),
                  pl.BlockSpec((seq, width), lambda b, h: (b, COL_DZ * nblk + h)),
                  pl.BlockSpec((seq, LANES), lambda b, h: (b, COL_SC128)),
                  pl.BlockSpec((None, 3, CONV_W, width), lambda b, h: (layer, 0, 0, h)),
                  vec, vec, vec,
                  pl.BlockSpec((None, None, 2, nhs, DN_DK, DN_DK), s0_map)],
        out_specs=[pl.BlockSpec((seq, width), lambda b, h: (b, h)),
                   pl.BlockSpec((None, 2, nhs, DN_DK, DN_DK), lambda b, h: (b, 0, h, 0, 0))],
        out_shape=[jax.ShapeDtypeStruct((batch * seq, BRANCH_W), F32),
                   jax.ShapeDtypeStruct((batch, 2, nh, DN_DK, DN_DK), F32)],
        scratch_shapes=[pltpu.VMEM((seq + 2 * SUBLANES, LANES), F32),
                        pltpu.VMEM((nhs, seq, LANES), F32),
                        pltpu.VMEM((nhs, seq, LANES), F32),
                        pltpu.VMEM((nhs, seq, LANES), F32),
                        pltpu.VMEM((nhs, seq, LANES), F32),
                        pltpu.VMEM((2, nhs, seq, LANES), F32),
                        pltpu.VMEM((2, nhs, 2 * seq, LANES), BF16),
                        pltpu.VMEM((2, nhs, seq, LANES), BF16),
                        pltpu.VMEM((2, nhs, seq, DN_TILE), BF16),
                        pltpu.VMEM((2, nhs, nsub, LANES), F32),
                        pltpu.VMEM((nhs, seq, LANES), F32)],
        compiler_params=_params(2),
        name="deltanet",
    )(proj, proj, proj, proj, proj, params["dn_cw"], params["dn_alog"], params["dn_dtb"], params["dn_ng"], s0)


def _prepare_params(w_ffn_gate, w_ffn_up, w_ffn_down, w_in, lru_conv_w, lru_conv_b, lru_w_r, lru_b_r,
                    lru_w_i, lru_b_i, lru_lambda, gqa_q_norm, gqa_k_norm, na_rpb, dn_conv_w, dn_a_log,
                    dn_dt_bias, dn_norm_g, w_branch, w_out, norm_g):
    depth = w_in.shape[0]
    offs = np.cumsum((0,) + IN_WIDTHS)
    seg = [w_in[:, :, offs[i]:offs[i + 1]] for i in range(len(IN_WIDTHS))]
    (a_x, a_y, b_q, b_k, b_v, c_q, c_k, c_v, d_q, d_k, d_v, d_z, d_b, d_a, g_lin) = seg
    perm = np.asarray(GQA_PERM)
    b_q = b_q.reshape(depth, D_MODEL, GQA_HEADS, HEAD_DIM)[:, :, perm].reshape(depth, D_MODEL, BRANCH_W)
    pad = jnp.zeros((depth, D_MODEL, PROJ_W - 5120 - 2 * LANES - 16), F32)
    w_main = jnp.concatenate([a_x, a_y, b_q, c_q, c_k, c_v, d_q, d_k, d_v, d_z, b_k, b_v, d_b, d_a, pad],
                             axis=2).astype(BF16)

    def block_diag(wb):
        wb = wb.reshape(depth, 2, LRU_BLOCKS // 2, 2, LRU_BW, LRU_BW)
        z = jnp.zeros_like(wb[:, :, :, 0])
        return jnp.concatenate([jnp.concatenate([wb[:, :, :, 0], z], axis=-1),
                                jnp.concatenate([z, wb[:, :, :, 1]], axis=-1)], axis=-2)

    wr, wi = block_diag(lru_w_r), block_diag(lru_w_i)
    lru_wg = (0.5 * jnp.concatenate([wr[:, 0], wi[:, 0], wr[:, 1], wi[:, 1]], axis=-1)).astype(BF16)
    ncol = BRANCH_W // LANES

    def col_blocks(v):
        return v.reshape(depth, ncol, LANES)

    lru_bg = 0.5 * jnp.concatenate([col_blocks(lru_b_r[:, 0]), col_blocks(lru_b_i[:, 0]),
                                    col_blocks(lru_b_r[:, 1]), col_blocks(lru_b_i[:, 1])], axis=-1)[:, :, None, :]

    wb_b = w_branch[:, 1].reshape(depth, GQA_HEADS, HEAD_DIM, D_MODEL)[:, perm].reshape(depth, BRANCH_W, D_MODEL)
    wb = jnp.stack([w_branch[:, 0], wb_b, w_branch[:, 2], w_branch[:, 3]], axis=1).astype(BF16)

    lane_pad = jnp.zeros((depth, LANES - 4 * DN_HEADS), F32)
    lane_zero = jnp.zeros((depth, 2 * DN_HEADS), F32)
    alog = jnp.concatenate([lane_zero, dn_a_log.reshape(depth, -1), lane_pad], axis=1)[:, None, :]
    dtb = jnp.concatenate([lane_zero, dn_dt_bias.reshape(depth, -1), lane_pad], axis=1)[:, None, :]

    return dict(
        norm_g=norm_g[:, :, None, :],
        wg=w_ffn_gate.astype(BF16), wu=w_ffn_up.astype(BF16), wd=w_ffn_down.astype(BF16),
        w_main=w_main, w_gate=g_lin.astype(BF16),
        lru_cw=lru_conv_w, lru_cb=lru_conv_b[:, None, :], lru_wg=lru_wg, lru_bg=lru_bg, lru_lam=lru_lambda,
        gq=jnp.tile(gqa_q_norm, (1, 2))[:, None, :], gk=jnp.tile(gqa_k_norm, (1, 2))[:, None, :],
        na_bias=_na_bias_table(na_rpb),
        dn_cw=dn_conv_w.reshape(depth, CONV_W, 3, BRANCH_W).transpose(0, 2, 1, 3),
        dn_alog=alog, dn_dtb=dtb, dn_ng=dn_norm_g[:, None, :],
        wb=wb, w_out=w_out.astype(BF16),
    )


def _rope_tables(seq):
    pos = jnp.arange(seq)
    half = HEAD_DIM // 2
    inv = jnp.power(ROPE_BASE, -jnp.arange(0, half, 2, dtype=F32) / half)
    ang_r = (pos // GRID_W).astype(F32)[:, None] * inv[None, :]
    ang_c = (pos % GRID_W).astype(F32)[:, None] * inv[None, :]
    cos = jnp.concatenate([jnp.cos(ang_r)] * 2 + [jnp.cos(ang_c)] * 2, axis=-1)
    sin = jnp.concatenate([-jnp.sin(ang_r), jnp.sin(ang_r), -jnp.sin(ang_c), jnp.sin(ang_c)], axis=-1)
    return jnp.tile(cos, (1, 2)), jnp.tile(sin, (1, 2))


def _na_bias_table(rpb):
    qc = np.arange(GRID_W)
    cs = np.clip(qc - NA_WIN_C // 2, 0, GRID_W - NA_WIN_C)
    kc = np.arange(GRID_W)
    inwin = (kc[None, :] >= cs[:, None]) & (kc[None, :] < cs[:, None] + NA_WIN_C)
    coff = kc[None, :] - qc[:, None] + NA_WIN_C - 1
    onehot = (coff[None] == np.arange(2 * NA_WIN_C - 1)[:, None, None]).astype(np.float32)
    t = jnp.einsum("lhrd,dqk->lhrqk", rpb.astype(F32), onehot, precision=lax.Precision.HIGHEST)
    t = jnp.where(inwin, t, NEG_BIG)
    edge = jnp.full_like(t[:, :, :1], NEG_BIG)
    t = jnp.concatenate([edge, t, edge], axis=2)
    return jnp.concatenate([t[:, :, :-1], t[:, :, 1:]], axis=-1)


def _layer(x, mod_all, params, *, batch, seq, latent, layer, caches, tables, final_g, tm):
    row0 = 1 if latent else 0
    x = _ffn_call(x, mod_all, params, final_g, layer=layer, which=0, row0=row0, final=False, tm=tm)
    proj = _inproj_call(x, mod_all, params, layer=layer, row0=row0, tm=tm)

    dn_heads_per_step = 1 if seq > 1024 else DN_HEADS
    if latent:
        cache_ak, cache_av, cache_nk, cache_nv, state_lru, state_delta = caches
        h0, h0_map = state_lru, lambda b, c: (b, layer, 0, c)
        s0, s0_map = state_delta, lambda b, h: (b, layer, 0, h, 0, 0)
    else:
        h0, h0_map = jnp.zeros((1, 1, 2, BRANCH_W), F32), lambda b, c: (0, 0, 0, c)
        s0 = jnp.zeros((1, 1, 2, dn_heads_per_step, DN_DK, DN_DK), F32)
        s0_map = lambda b, h: (0, 0, 0, 0, 0, 0)

    o_a, lru_fin = _lru_call(proj, h0, h0_map, params, layer=layer, batch=batch, seq=seq)

    cos, sin = tables["rope"] if latent else (None, None)
    qn, kn = _prep_call(proj, params, cos, sin, layer=layer, batch=batch, seq=seq, rope=latent)
    tq = min(seq, ATTN_QROWS)
    kv_new = (kn, pl.BlockSpec((seq, LANES), lambda b, i: (b, 0)),
              proj, pl.BlockSpec((seq, LANES), lambda b, i: (b, COL_BV128)))
    if latent:
        gqa_cache = pl.BlockSpec((None, None, PAST_LEN, LANES), lambda b, i: (b, layer, 0, 0))
        sources = [kv_new, (cache_ak, gqa_cache, cache_av, gqa_cache)]
    else:
        sources = [kv_new]
    o_b = _attn_call(qn, 0, sources, batch=batch, seq=seq, tq=tq, nkb=1, qscale=1.0)

    if latent:
        o_c = _na_call(proj, cache_nk, cache_nv, params, layer=layer, batch=batch, seq=seq)
    else:
        src = (proj, pl.BlockSpec((seq, BRANCH_W), lambda b, i: (b, COL_CK)),
               proj, pl.BlockSpec((seq, BRANCH_W), lambda b, i: (b, COL_CV)))
        o_c = _attn_call(proj, COL_CQ, [src], batch=batch, seq=seq, tq=tq, nkb=4, qscale=HEAD_DIM ** -0.5)

    o_d, dn_fin = _dn_call(proj, params, s0, s0_map, layer=layer, batch=batch, seq=seq, nhs=dn_heads_per_step)

    x = _merge_call(x, mod_all, params, (o_a, o_b, o_c, o_d), layer=layer, row0=row0, tm=tm)
    x = _ffn_call(x, mod_all, params, final_g, layer=layer, which=1, row0=row0,
                  final=(layer == DEPTH - 1), tm=tm)

    new_ctx = None
    if not latent:
        new_ctx = (kn.reshape(batch, seq, GQA_KV, HEAD_DIM),
                   proj[:, COL_BV128 * LANES:(COL_BV128 + 1) * LANES].reshape(batch, seq, GQA_KV, HEAD_DIM),
                   proj[:, COL_CK * BRANCH_W:(COL_CK + 1) * BRANCH_W].reshape(batch, seq, NA_HEADS, HEAD_DIM),
                   proj[:, COL_CV * BRANCH_W:(COL_CV + 1) * BRANCH_W].reshape(batch, seq, NA_HEADS, HEAD_DIM),
                   lru_fin, dn_fin)
    return x, new_ctx


def kernel(x_prompt, x_sample, c, cache_attn_k, cache_attn_v, cache_na_k, cache_na_v, state_lru, state_delta, c_ctx, w_mod, b_mod, norm_g, w_ffn_gate, w_ffn_up, w_ffn_down, w_in, lru_conv_w, lru_conv_b, lru_w_r, lru_b_r, lru_w_i, lru_b_i, lru_lambda, gqa_q_norm, gqa_k_norm, na_rpb, dn_conv_w, dn_a_log, dn_dt_bias, dn_norm_g, w_branch, w_out, final_norm_g):
    batch_c, seq_c, _ = x_prompt.shape
    batch_l, seq_l, _ = x_sample.shape
    assert batch_l + 1 <= SUBLANES

    cs = jnp.concatenate([c_ctx[None, :], c, jnp.zeros((SUBLANES - 1 - batch_l, D_MODEL), F32)], axis=0)
    mod_all = _mod_call(cs, w_mod, b_mod).reshape(DEPTH, SUBLANES, N_MOD, D_MODEL)
    params = _prepare_params(w_ffn_gate, w_ffn_up, w_ffn_down, w_in, lru_conv_w, lru_conv_b, lru_w_r, lru_b_r,
                             lru_w_i, lru_b_i, lru_lambda, gqa_q_norm, gqa_k_norm, na_rpb, dn_conv_w, dn_a_log,
                             dn_dt_bias, dn_norm_g, w_branch, w_out, norm_g)

    caches = (cache_attn_k.reshape(batch_l, DEPTH, PAST_LEN, GQA_KV * HEAD_DIM),
              cache_attn_v.reshape(batch_l, DEPTH, PAST_LEN, GQA_KV * HEAD_DIM),
              cache_na_k.reshape(batch_l, DEPTH, PAST_LEN, BRANCH_W),
              cache_na_v.reshape(batch_l, DEPTH, PAST_LEN, BRANCH_W),
              state_lru, state_delta)
    tables = {"rope": _rope_tables(seq_l)}
    final_g = final_norm_g[None, :]

    xc = x_prompt.reshape(1, batch_c * seq_c, D_MODEL)
    xl = x_sample
    ctx_out = []
    for l in range(DEPTH):
        xc, new_ctx = _layer(xc, mod_all, params, batch=batch_c, seq=seq_c, latent=False, layer=l, caches=None,
                             tables=tables, final_g=final_g, tm=DENSE_ROWS)
        ctx_out.append(new_ctx)
        xl, _ = _layer(xl, mod_all, params, batch=batch_l, seq=seq_l, latent=True, layer=l, caches=caches,
                       tables=tables, final_g=final_g, tm=DENSE_ROWS)

    stacked = [jnp.stack([ctx_out[l][i] for l in range(DEPTH)], axis=1) for i in range(6)]
    return (xc.reshape(batch_c, seq_c, D_MODEL), xl, *stacked)
```

```python
import functools

import numpy as np
import jax
import jax.numpy as jnp
from jax import lax
from jax.experimental import pallas as pl
from jax.experimental.pallas import tpu as pltpu

F32 = jnp.float32
BF16 = jnp.bfloat16

D_MODEL = 1024
DEPTH = 4
GRID_W = 64
N_BRANCH = 4
BRANCH_W = 512
N_MOD = 9
D_FF = 2816
EPS = 1e-6
CONV_W = 4
LRU_BLOCKS = 8
LRU_BW = 64
LRU_C = 8.0
HEAD_DIM = 64
GQA_HEADS = 8
GQA_KV = 2
ROPE_BASE = 10000.0
NA_HEADS = 8
NA_WIN_R = 8
NA_WIN_C = 16
NA_QROWS = 4
NA_KROWS = 12
DN_DK = 128
DN_HEADS = 4
LRU_PITCH_PAD = 8
DN_TILE = 128
DN_CHAINS = 16
PAST_LEN = 512
IN_WIDTHS = (512, 512, 512, 128, 128, 512, 512, 512, 512, 512, 512, 512, 8, 8, 4096)

LANES = 128
SUBLANES = 8
DENSE_ROWS = 512
DENSE_PARTS = 2
ATTN_QROWS = 512
PROJ_W = 5632
COL_BQ, COL_CQ, COL_CK, COL_CV, COL_DQ, COL_DZ = 2, 3, 4, 5, 6, 9
COL_BK128, COL_BV128, COL_SC128 = 40, 41, 42
VMEM_LIMIT = 56 * 1024 * 1024
NEG_BIG = -1e30
GQA_PERM = (0, 4, 1, 5, 2, 6, 3, 7)


def _params(n):
    return pltpu.CompilerParams(dimension_semantics=("arbitrary",) * n, vmem_limit_bytes=VMEM_LIMIT)


def _const_spec(shape):
    nd = len(shape)
    return pl.BlockSpec(shape, lambda *_: (0,) * nd, pipeline_mode=pl.Buffered(1))


def _layer_spec(arr, layer, *sub):
    tail = arr.shape[1 + len(sub):]
    index = (layer,) + tuple(sub) + (0,) * len(tail)
    return pl.BlockSpec((None,) * (1 + len(sub)) + tuple(tail), lambda *_: index, pipeline_mode=pl.Buffered(1))


def _mod_spec(layer, row0):
    return pl.BlockSpec((None, None, N_MOD, D_MODEL), lambda b, i: (layer, row0 + b, 0, 0))


def _dot(a, b):
    return jnp.dot(a, b, preferred_element_type=F32)


def _dot_nt(a, b):
    return lax.dot_general(a, b, (((1,), (1,)), ((), ())), preferred_element_type=F32)


def _dot_tn(a, b):
    return lax.dot_general(a, b, (((0,), (0,)), ((), ())), preferred_element_type=F32)


def _split(x):
    hi = x.astype(BF16)
    lo = (x - hi.astype(F32)).astype(BF16)
    return hi, lo


def _dot3(a, b):
    ah, al = _split(a)
    bh, bl = _split(b)
    return _dot(ah, bh) + (_dot(al, bh) + _dot(ah, bl))


def _sigmoid(x):
    return 0.5 * jnp.tanh(0.5 * x) + 0.5


def _modnorm(x, g, shift, scale):
    ms = jnp.mean(x * x, axis=-1, keepdims=True)
    return (x * lax.rsqrt(ms + EPS) * g) * (1.0 + scale) + shift


def _mod_kernel(c_ref, w_ref, b_ref, o_ref):
    c = c_ref[...]
    o_ref[0] = _dot3(c * _sigmoid(c), w_ref[0]) + b_ref[0]


def _mod_call(cs, w_mod, b_mod):
    tn = 1024
    n = N_MOD * D_MODEL
    return pl.pallas_call(
        _mod_kernel,
        grid=(DEPTH, n // tn),
        in_specs=[pl.BlockSpec((SUBLANES, D_MODEL), lambda l, j: (0, 0)),
                  pl.BlockSpec((1, D_MODEL, tn), lambda l, j: (l, 0, j)),
                  pl.BlockSpec((1, 1, tn), lambda l, j: (l, 0, j))],
        out_specs=pl.BlockSpec((1, SUBLANES, tn), lambda l, j: (l, 0, j)),
        out_shape=jax.ShapeDtypeStruct((DEPTH, SUBLANES, n), F32),
        compiler_params=_params(2),
        name="mod",
    )(cs, w_mod, b_mod.reshape(DEPTH, 1, n))


def _ffn_kernel(x_ref, mod_ref, g_ref, wg_ref, wu_ref, wd_ref, gf_ref, o_ref, *, mi, final):
    mod = mod_ref[...]
    rows = x_ref.shape[1]
    parts = [pl.ds(n * (rows // DENSE_PARTS), rows // DENSE_PARTS) for n in range(DENSE_PARTS)]
    xs = [x_ref[0, p, :] for p in parts]
    hs = [_modnorm(x, g_ref[...], mod[mi:mi + 1], mod[mi + 1:mi + 2]).astype(BF16) for x in xs]
    acts = []
    for h in hs:
        gt = _dot(h, wg_ref[...])
        up = _dot(h, wu_ref[...])
        acts.append((gt * _sigmoid(gt) * up).astype(BF16))
    downs = [_dot(a, wd_ref[...]) for a in acts]
    for p, x, d in zip(parts, xs, downs):
        y = x + 0.5 * mod[mi + 2:mi + 3] * d
        if final:
            ms = jnp.mean(y * y, axis=-1, keepdims=True)
            y = y * lax.rsqrt(ms + EPS) * gf_ref[...]
        o_ref[0, p, :] = y


def _ffn_call(x, mod_all, params, gf, *, layer, which, row0, final, tm):
    nb, rows, _ = x.shape
    return pl.pallas_call(
        functools.partial(_ffn_kernel, mi=6 * which, final=final),
        grid=(nb, rows // tm),
        in_specs=[pl.BlockSpec((1, tm, D_MODEL), lambda b, i: (b, i, 0)),
                  _mod_spec(layer, row0),
                  _layer_spec(params["norm_g"], layer, 2 * which),
                  _layer_spec(params["wg"], layer, which),
                  _layer_spec(params["wu"], layer, which),
                  _layer_spec(params["wd"], layer, which),
                  _const_spec((1, D_MODEL))],
        out_specs=pl.BlockSpec((1, tm, D_MODEL), lambda b, i: (b, i, 0)),
        out_shape=jax.ShapeDtypeStruct(x.shape, F32),
        compiler_params=_params(2),
        name="ffn",
    )(x, mod_all, params["norm_g"], params["wg"], params["wu"], params["wd"], gf)


def _inproj_kernel(x_ref, mod_ref, g_ref, w_ref, o_ref):
    mod = mod_ref[...]
    h = _modnorm(x_ref[0], g_ref[...], mod[3:4], mod[4:5]).astype(BF16)
    o_ref[...] = _dot_nt(h, w_ref[...])


def _inproj_call(x, mod_all, params, *, layer, row0, tm):
    nb, rows, _ = x.shape
    nt = rows // tm
    return pl.pallas_call(
        _inproj_kernel,
        grid=(nb, nt),
        in_specs=[pl.BlockSpec((1, tm, D_MODEL), lambda b, i: (b, i, 0)),
                  _mod_spec(layer, row0),
                  _layer_spec(params["norm_g"], layer, 1),
                  _layer_spec(params["w_main"], layer)],
        out_specs=pl.BlockSpec((tm, PROJ_W), lambda b, i: (b * nt + i, 0)),
        out_shape=jax.ShapeDtypeStruct((nb * rows, PROJ_W), F32),
        compiler_params=_params(2),
        name="inproj",
    )(x, mod_all, params["norm_g"], params["w_main"])


def _merge_kernel(x_ref, mod_ref, g_ref, oa_ref, ob_ref, oc_ref, od_ref, wgate_ref, wb_ref, wout_ref, o_ref):
    x = x_ref[0]
    mod = mod_ref[...]
    h = _modnorm(x, g_ref[...], mod[3:4], mod[4:5]).astype(BF16)
    acc = None
    for n, ref in enumerate((oa_ref, ob_ref, oc_ref, od_ref)):
        gate = _sigmoid(_dot_nt(h, wgate_ref[n * D_MODEL:(n + 1) * D_MODEL, :]))
        term = gate * _dot(ref[...].astype(BF16), wb_ref[n])
        acc = term if acc is None else acc + term
    o_ref[0] = x + mod[5:6] * _dot(acc.astype(BF16), wout_ref[...])


def _merge_call(x, mod_all, params, outs, *, layer, row0, tm):
    nb, rows, _ = x.shape
    nt = rows // tm
    ospec = pl.BlockSpec((tm, BRANCH_W), lambda b, i: (b * nt + i, 0))
    return pl.pallas_call(
        _merge_kernel,
        grid=(nb, nt),
        in_specs=[pl.BlockSpec((1, tm, D_MODEL), lambda b, i: (b, i, 0)),
                  _mod_spec(layer, row0),
                  _layer_spec(params["norm_g"], layer, 1),
                  ospec, ospec, ospec, ospec,
                  _layer_spec(params["w_gate"], layer),
                  _layer_spec(params["wb"], layer),
                  _layer_spec(params["w_out"], layer)],
        out_specs=pl.BlockSpec((1, tm, D_MODEL), lambda b, i: (b, i, 0)),
        out_shape=jax.ShapeDtypeStruct(x.shape, F32),
        compiler_params=_params(2),
        name="merge",
    )(x, mod_all, params["norm_g"], *outs, params["w_gate"], params["wb"], params["w_out"])


def _log_sigmoid(x):
    return jnp.minimum(x, 0.0) - jnp.log1p(jnp.exp(-jnp.abs(x)))


def _gelu_tanh(x):
    return x * (0.5 * (1.0 + jnp.tanh(0.7978845608028654 * (x + 0.044715 * (x * x * x)))))


def _lru_kernel(ax_ref, ay_ref, h0_ref, cw_ref, cb_ref, wg_ref, bg_ref, lam_ref, o_ref, fin_ref,
                xp_ref, a_ref, u_ref, h_ref, p_ref, *, seq):
    lc = seq // SUBLANES
    pitch = lc + LRU_PITCH_PAD
    zero8 = jnp.zeros((SUBLANES, LANES), F32)
    xp_ref[0:SUBLANES, :] = zero8
    xp_ref[SUBLANES + seq:2 * SUBLANES + seq, :] = zero8
    xp_ref[SUBLANES:SUBLANES + seq, :] = ax_ref[...]
    cw = cw_ref[...]
    xa = cb_ref[...] + cw[0:1] * xp_ref[pl.ds(SUBLANES - 2, seq), :]
    for k in range(1, CONV_W):
        xa = xa + cw[k:k + 1] * xp_ref[pl.ds(SUBLANES - 2 + k, seq), :]
    half = _dot(xa.astype(BF16), wg_ref[...]) + bg_ref[...]
    lam = lam_ref[...]
    xh = 0.5 * xa
    for dr in range(2):
        tr = jnp.tanh(half[:, (2 * dr) * LANES:(2 * dr + 1) * LANES])
        ti = jnp.tanh(half[:, (2 * dr + 1) * LANES:(2 * dr + 2) * LANES])
        c_half = (0.5 * LRU_C) * _log_sigmoid(lam[dr:dr + 1])
        log_a = tr * c_half + c_half
        a = jnp.exp(log_a)
        one_m_a2 = -jnp.tanh(log_a) * (a * a + 1.0)
        root = jnp.where(one_m_a2 > 0.0, one_m_a2 * lax.rsqrt(one_m_a2), 0.0)
        u = root * ((ti + 1.0) * xh)
        for k in range(SUBLANES):
            a_ref[dr, pl.ds(k * pitch, lc), :] = a[k * lc:(k + 1) * lc]
            u_ref[dr, pl.ds(k * pitch, lc), :] = u[k * lc:(k + 1) * lc]

    sub = lax.broadcasted_iota(jnp.int32, (SUBLANES, LANES), 0)
    h0 = h0_ref[...]
    hf0 = jnp.where(sub == 0, h0[0:1], 0.0)
    hb0 = jnp.where(sub == SUBLANES - 1, h0[1:2], 0.0)
    ones = jnp.ones((SUBLANES, LANES), F32)

    def body(s, carry):
        hf, pf, hb, pb = carry
        rows_f = pl.ds(s, SUBLANES, stride=pitch)
        rows_b = pl.ds(lc - 1 - s, SUBLANES, stride=pitch)
        af = a_ref[0, rows_f, :]
        hf = af * hf + u_ref[0, rows_f, :]
        pf = af * pf
        h_ref[0, rows_f, :] = hf
        p_ref[0, rows_f, :] = pf
        ab = a_ref[1, rows_b, :]
        hb = ab * hb + u_ref[1, rows_b, :]
        pb = ab * pb
        h_ref[1, rows_b, :] = hb
        p_ref[1, rows_b, :] = pb
        return hf, pf, hb, pb

    hf, pf, hb, pb = lax.fori_loop(0, lc, body, (hf0, ones, hb0, ones), unroll=8)

    cf = [jnp.zeros((1, LANES), F32)]
    for k in range(1, SUBLANES):
        cf.append(hf[k - 1:k] + pf[k - 1:k] * cf[k - 1])
    fin_ref[0:1, :] = hf[SUBLANES - 1:SUBLANES] + pf[SUBLANES - 1:SUBLANES] * cf[SUBLANES - 1]
    cb = [None] * SUBLANES
    cb[SUBLANES - 1] = jnp.zeros((1, LANES), F32)
    for k in range(SUBLANES - 2, -1, -1):
        cb[k] = hb[k + 1:k + 2] + pb[k + 1:k + 2] * cb[k + 1]
    fin_ref[1:2, :] = hb[0:1] + pb[0:1] * cb[0]

    for k in range(SUBLANES):
        rows = pl.ds(k * lc, lc)
        held = pl.ds(k * pitch, lc)
        h = (h_ref[0, held, :] + p_ref[0, held, :] * cf[k]) + (h_ref[1, held, :] + p_ref[1, held, :] * cb[k])
        o_ref[rows, :] = h * _gelu_tanh(ay_ref[rows, :])


def _lru_call(proj, h0, h0_map, params, *, layer, batch, seq):
    ncol = BRANCH_W // LANES
    pad_rows = SUBLANES * LRU_PITCH_PAD
    return pl.pallas_call(
        functools.partial(_lru_kernel, seq=seq),
        grid=(batch, ncol),
        in_specs=[pl.BlockSpec((seq, LANES), lambda b, c: (b, c)),
                  pl.BlockSpec((seq, LANES), lambda b, c: (b, ncol + c)),
                  pl.BlockSpec((None, None, 2, LANES), h0_map),
                  pl.BlockSpec((None, CONV_W, LANES), lambda b, c: (layer, 0, c)),
                  pl.BlockSpec((None, 1, LANES), lambda b, c: (layer, 0, c)),
                  pl.BlockSpec((None, None, LANES, 4 * LANES), lambda b, c: (layer, c, 0, 0)),
                  pl.BlockSpec((None, None, 1, 4 * LANES), lambda b, c: (layer, c, 0, 0)),
                  pl.BlockSpec((None, 2, LANES), lambda b, c: (layer, 0, c))],
        out_specs=[pl.BlockSpec((seq, LANES), lambda b, c: (b, c)),
                   pl.BlockSpec((None, 2, LANES), lambda b, c: (b, 0, c))],
        out_shape=[jax.ShapeDtypeStruct((batch * seq, BRANCH_W), F32),
                   jax.ShapeDtypeStruct((batch, 2, BRANCH_W), F32)],
        scratch_shapes=[pltpu.VMEM((seq + 2 * SUBLANES, LANES), F32),
                        pltpu.VMEM((2, seq + pad_rows, LANES), F32),
                        pltpu.VMEM((2, seq + pad_rows, LANES), F32),
                        pltpu.VMEM((2, seq + pad_rows, LANES), F32),
                        pltpu.VMEM((2, seq + pad_rows, LANES), F32)],
        compiler_params=_params(2),
        name="lru",
    )(proj, proj, h0, params["lru_cw"], params["lru_cb"], params["lru_wg"], params["lru_bg"], params["lru_lam"])


def _prep_kernel(*refs, rope):
    if rope:
        q_ref, k_ref, gq_ref, gk_ref, cos_ref, sin_ref, qn_ref, kn_ref = refs
    else:
        q_ref, k_ref, gq_ref, gk_ref, qn_ref, kn_ref = refs
    rows = q_ref.shape[0]
    lane = lax.broadcasted_iota(jnp.int32, (rows, LANES), 1)
    lo = lane < HEAD_DIM
    first16 = (lane & 16) == 0

    def head_norm(x, g):
        sq = x * x
        s_lo = jnp.sum(jnp.where(lo, sq, 0.0), axis=-1, keepdims=True)
        s_hi = jnp.sum(jnp.where(lo, 0.0, sq), axis=-1, keepdims=True)
        ms = jnp.where(lo, s_lo, s_hi) * (1.0 / HEAD_DIM)
        return x * lax.rsqrt(ms + EPS) * g

    def rotate(y):
        if not rope:
            return y
        partner = jnp.where(first16, pltpu.roll(y, LANES - 16, 1), pltpu.roll(y, 16, 1))
        return y * cos_ref[...] + partner * sin_ref[...]

    for p in range(BRANCH_W // LANES):
        cols = slice(p * LANES, (p + 1) * LANES)
        qn_ref[:, cols] = rotate(head_norm(q_ref[:, cols], gq_ref[...])) * (HEAD_DIM ** -0.5)
    kn_ref[...] = rotate(head_norm(k_ref[...], gk_ref[...]))


def _prep_call(proj, params, cos, sin, *, layer, batch, seq, rope):
    tab = pl.BlockSpec((seq, LANES), lambda b: (0, 0))
    in_specs = [pl.BlockSpec((seq, BRANCH_W), lambda b: (b, COL_BQ)),
                pl.BlockSpec((seq, LANES), lambda b: (b, COL_BK128)),
                _layer_spec(params["gq"], layer), _layer_spec(params["gk"], layer)]
    args = [proj, proj, params["gq"], params["gk"]]
    if rope:
        in_specs += [tab, tab]
        args += [cos, sin]
    return pl.pallas_call(
        functools.partial(_prep_kernel, rope=rope),
        grid=(batch,),
        in_specs=in_specs,
        out_specs=[pl.BlockSpec((seq, BRANCH_W), lambda b: (b, 0)),
                   pl.BlockSpec((seq, LANES), lambda b: (b, 0))],
        out_shape=[jax.ShapeDtypeStruct((batch * seq, BRANCH_W), F32),
                   jax.ShapeDtypeStruct((batch * seq, LANES), F32)],
        compiler_params=_params(1),
        name="attn_prep",
    )(*args)


def _softmax_pv(scores, values):
    m = None
    for s in scores:
        sm = jnp.max(s, axis=-1, keepdims=True)
        m = sm if m is None else jnp.maximum(m, sm)
    den = None
    out = None
    for s, v in zip(scores, values):
        p = jnp.exp(s - m)
        ps = jnp.sum(p, axis=-1, keepdims=True)
        den = ps if den is None else den + ps
        o = _dot(p.astype(BF16), v)
        out = o if out is None else out + o
    return out / den


def _attn_kernel(*refs, nsrc, nkb, qscale):
    q_ref = refs[0]
    src = refs[1:1 + 2 * nsrc]
    o_ref = refs[1 + 2 * nsrc]
    tq = q_ref.shape[0]
    lo = lax.broadcasted_iota(jnp.int32, (tq, LANES), 1) < HEAD_DIM

    loaded = {}

    def kv(p):
        col = p if nkb != 1 else 0
        if col not in loaded:
            cols = slice(col * LANES, (col + 1) * LANES)
            loaded[col] = ([src[2 * i][:, cols].astype(BF16) for i in range(nsrc)],
                           [src[2 * i + 1][:, cols].astype(BF16) for i in range(nsrc)])
        return loaded[col]

    def scores(unit):
        p, half = unit
        qb = q_ref[:, p * LANES:(p + 1) * LANES]
        if qscale != 1.0:
            qb = qb * qscale
        qm = jnp.where(lo if half == 0 else jnp.logical_not(lo), qb, 0.0).astype(BF16)
        return [_dot_nt(qm, k) for k in kv(p)[0]]

    units = [(p, half) for p in range(BRANCH_W // LANES) for half in range(2)]
    pending = scores(units[0])
    first_half = None
    for n, (p, half) in enumerate(units):
        current = pending
        if n + 1 < len(units):
            pending = scores(units[n + 1])
        out = _softmax_pv(current, kv(p)[1])
        if half == 0:
            first_half = out
        else:
            o_ref[:, p * LANES:(p + 1) * LANES] = jnp.where(lo, first_half, out)


def _attn_call(q_arr, q_col, sources, *, batch, seq, tq, nkb, qscale):
    nq = seq // tq
    in_specs = [pl.BlockSpec((tq, BRANCH_W), lambda b, i: (b * nq + i, q_col))]
    args = [q_arr]
    for k_arr, k_spec, v_arr, v_spec in sources:
        in_specs += [k_spec, v_spec]
        args += [k_arr, v_arr]
    return pl.pallas_call(
        functools.partial(_attn_kernel, nsrc=len(sources), nkb=nkb, qscale=qscale),
        grid=(batch, nq),
        in_specs=in_specs,
        out_specs=pl.BlockSpec((tq, BRANCH_W), lambda b, i: (b * nq + i, 0)),
        out_shape=jax.ShapeDtypeStruct((batch * seq, BRANCH_W), F32),
        compiler_params=_params(2),
        name="attn",
    )(*args)


def _na_row_start(r, rows):
    return jnp.clip(r - NA_WIN_R // 2, 0, rows - NA_WIN_R)


def _na_kernel(q_ref, k_ref, v_ref, ck_ref, cv_ref, bias_ref, o_ref, *, rows):
    r0 = pl.program_id(1) * NA_QROWS
    w0 = jnp.clip(r0 - NA_WIN_R // 2, 0, rows - NA_KROWS)
    win = pl.ds(pl.multiple_of(w0 * GRID_W, GRID_W), NA_KROWS * GRID_W)
    lo = lax.broadcasted_iota(jnp.int32, (NA_QROWS * GRID_W, LANES), 1) < HEAD_DIM
    lo_row = lax.broadcasted_iota(jnp.int32, (GRID_W, LANES), 1) < HEAD_DIM
    ncol = BRANCH_W // LANES
    scores = []
    for p in range(ncol):
        cols = slice(p * LANES, (p + 1) * LANES)
        qb = q_ref[:, cols] * (HEAD_DIM ** -0.5)
        kw = k_ref[win, cols].astype(BF16)
        kc = ck_ref[:, cols].astype(BF16)
        for half in range(2):
            qm = jnp.where(lo if half == 0 else jnp.logical_not(lo), qb, 0.0).astype(BF16)
            scores.append((_dot_nt(qm, kw), _dot_nt(qm, kc)))
    pair_index, pair_mask = [], []
    for j in range(NA_QROWS):
        r = r0 + j
        rs = _na_row_start(r, rows)
        idx_j, mask_j = [], []
        for t in range(NA_KROWS // 2):
            kr = w0 + 2 * t
            rel = kr - r + NA_WIN_R - 1
            idx_j.append(jnp.clip(rel + 1, 0, 2 * NA_WIN_R - 1))
            in_a = jnp.where((kr >= rs) & (kr < rs + NA_WIN_R), 0.0, NEG_BIG)
            in_b = jnp.where((kr + 1 >= rs) & (kr + 1 < rs + NA_WIN_R), 0.0, NEG_BIG)
            mask_j.append(jnp.where(lo_row, in_a, in_b))
        pair_index.append(idx_j)
        pair_mask.append(mask_j)
    probs = []
    for h, (s_loc, s_ctx) in enumerate(scores):
        bias = jnp.concatenate(
            [jnp.concatenate([bias_ref[h, pair_index[j][t]] + pair_mask[j][t] for t in range(NA_KROWS // 2)],
                             axis=-1) for j in range(NA_QROWS)], axis=0)
        s_loc = s_loc + bias
        m = jnp.maximum(jnp.max(s_loc, axis=-1, keepdims=True), jnp.max(s_ctx, axis=-1, keepdims=True))
        p_loc = jnp.exp(s_loc - m)
        p_ctx = jnp.exp(s_ctx - m)
        den = jnp.sum(p_loc, axis=-1, keepdims=True) + jnp.sum(p_ctx, axis=-1, keepdims=True)
        probs.append((p_loc.astype(BF16), p_ctx.astype(BF16), den))
    for p in range(ncol):
        cols = slice(p * LANES, (p + 1) * LANES)
        vw = v_ref[win, cols].astype(BF16)
        vc = cv_ref[:, cols].astype(BF16)
        halves = []
        for half in range(2):
            p_loc, p_ctx, den = probs[2 * p + half]
            halves.append((_dot(p_loc, vw) + _dot(p_ctx, vc)) / den)
        o_ref[:, cols] = jnp.where(lo, halves[0], halves[1])


def _na_call(proj, cache_k, cache_v, params, *, layer, batch, seq):
    rows = seq // GRID_W
    steps = rows // NA_QROWS
    qrows = NA_QROWS * GRID_W
    cache_spec = pl.BlockSpec((None, None, PAST_LEN, BRANCH_W), lambda b, r: (b, layer, 0, 0))
    return pl.pallas_call(
        functools.partial(_na_kernel, rows=rows),
        grid=(batch, steps),
        in_specs=[pl.BlockSpec((qrows, BRANCH_W), lambda b, r: (b * steps + r, COL_CQ)),
                  pl.BlockSpec((seq, BRANCH_W), lambda b, r: (b, COL_CK)),
                  pl.BlockSpec((seq, BRANCH_W), lambda b, r: (b, COL_CV)),
                  cache_spec, cache_spec,
                  _layer_spec(params["na_bias"], layer)],
        out_specs=pl.BlockSpec((qrows, BRANCH_W), lambda b, r: (b * steps + r, 0)),
        out_shape=jax.ShapeDtypeStruct((batch * seq, BRANCH_W), F32),
        compiler_params=_params(2),
        name="na",
    )(proj, proj, proj, cache_k, cache_v, params["na_bias"])


def _split3(x):
    hi = x.astype(BF16)
    r = x - hi.astype(F32)
    mid = r.astype(BF16)
    lo = (r - mid.astype(F32)).astype(BF16)
    return hi, mid, lo


def _dn_kernel(q_ref, k_ref, v_ref, z_ref, sl_ref, cw_ref, alog_ref, dtb_ref, ng_ref, s0_ref,
               o_ref, sfin_ref,
               xp_ref, qn_ref, kn_ref, vn_ref, col_ref, u_ref, wq_ref, kd_ref, in_ref, gl_ref, acc_ref,
               *, seq, nhs):
    c_len = DN_TILE
    n_chunks = seq // c_len

    zero8 = jnp.zeros((SUBLANES, LANES), F32)
    xp_ref[0:SUBLANES, :] = zero8
    xp_ref[SUBLANES + seq:2 * SUBLANES + seq, :] = zero8
    for hh in range(nhs):
        lanes = slice(hh * LANES, (hh + 1) * LANES)
        for j, (src, dst) in enumerate(((q_ref, qn_ref), (k_ref, kn_ref), (v_ref, vn_ref))):
            xp_ref[SUBLANES:SUBLANES + seq, :] = src[:, lanes]
            cw = cw_ref[j][:, lanes]
            y = cw[0:1] * xp_ref[pl.ds(SUBLANES - 2, seq), :]
            for t in range(1, CONV_W):
                y = y + cw[t:t + 1] * xp_ref[pl.ds(SUBLANES - 2 + t, seq), :]
            y = y * _sigmoid(y)
            if j < 2:
                y = y * lax.rsqrt(jnp.sum(y * y, axis=-1, keepdims=True) + EPS)
            if j == 0:
                y = y * (DN_DK ** -0.5)
            dst[hh] = y

    sl = sl_ref[...]
    lane = lax.broadcasted_iota(jnp.int32, (seq, LANES), 1)
    beta_all = _sigmoid(sl)
    xs = sl + dtb_ref[...]
    softplus = jnp.maximum(xs, 0.0) + jnp.log1p(jnp.exp(-jnp.abs(xs)))
    g_all = -jnp.exp(alog_ref[...]) * softplus
    for hh in range(nhs):
        head = pl.program_id(1) * nhs + hh
        cols = jnp.zeros((seq, LANES), F32)
        for dr in range(2):
            beta = jnp.sum(jnp.where(lane == dr * DN_HEADS + head, beta_all, 0.0), axis=-1, keepdims=True)
            g = jnp.sum(jnp.where(lane == 2 * DN_HEADS + dr * DN_HEADS + head, g_all, 0.0),
                        axis=-1, keepdims=True)
            cols = jnp.where(lane == dr, beta, cols)
            cols = jnp.where(lane == 2 + dr, g, cols)
        col_ref[hh] = cols
        acc_ref[hh] = jnp.zeros((seq, LANES), F32)

    ii = lax.broadcasted_iota(jnp.int32, (c_len, c_len), 0)
    jj = lax.broadcasted_iota(jnp.int32, (c_len, c_len), 1)
    eye = jnp.where(ii == jj, 1.0, 0.0).astype(F32)
    causal = (ii >= jj, ii <= jj)
    strict = (ii > jj, ii < jj)
    tri16 = tuple(jnp.where(m, 1.0, 0.0).astype(BF16) for m in causal)
    level_masks = []
    for dr in range(2):
        hi_idx, lo_idx = (ii, jj) if dr == 0 else (jj, ii)
        masks = []
        lvl = 0
        while (1 << lvl) < c_len:
            masks.append(((hi_idx >> (lvl + 1)) == (lo_idx >> (lvl + 1)))
                         & ((hi_idx >> lvl) == (lo_idx >> lvl) + 1))
            lvl += 1
        level_masks.append(masks)

    group = max(1, min(DN_CHAINS // (2 * nhs), n_chunks))
    units = [(hh, t) for hh in range(nhs) for t in range(group)]
    chains = [(m, dr) for m in range(len(units)) for dr in range(2)]
    lane_sq = jj

    def group_body(i, carry):
        cs = [i * group + t for hh, t in units]
        rows = [pl.ds(pl.multiple_of(c * c_len, c_len), c_len) for c in cs]
        q = [qn_ref[hh, rows[m], :] for m, (hh, t) in enumerate(units)]
        k = [kn_ref[hh, rows[m], :] for m, (hh, t) in enumerate(units)]
        v = [vn_ref[hh, rows[m], :] for m, (hh, t) in enumerate(units)]
        blk = [col_ref[hh, rows[m], :] for m, (hh, t) in enumerate(units)]
        q16 = [x.astype(BF16) for x in q]
        k16 = [x.astype(BF16) for x in k]
        qk = [_dot_nt(a, b) for a, b in zip(q16, k16)]
        kk = [_dot_nt(b, b) for b in k16]
        parts = [_split3(x) for x in blk]
        cum = [jnp.where(lane_sq == 2, sum(_dot(tri16[0], p) for p in ps), sum(_dot(tri16[1], p) for p in ps))
               for ps in parts]
        cum_t = [x.T for x in cum]
        gc_col = [cum[m][:, 2 + dr:3 + dr] for m, dr in chains]
        gc_row = [cum_t[m][2 + dr:3 + dr, :] for m, dr in chains]
        beta = [blk[m][:, dr:dr + 1] for m, dr in chains]
        decay = [jnp.where(causal[dr], jnp.exp(jnp.where(causal[dr], gc_col[n] - gc_row[n], 0.0)), 0.0)
                 for n, (m, dr) in enumerate(chains)]
        lm = [jnp.where(strict[dr], (beta[n] * kk[m]) * decay[n], 0.0) for n, (m, dr) in enumerate(chains)]
        lm16 = [x.astype(BF16) for x in lm]
        zero16 = jnp.zeros((c_len, c_len), BF16)
        xs = [eye - jnp.where(level_masks[dr][0], lm[n], 0.0) for n, (m, dr) in enumerate(chains)]
        for lvl in range(1, len(level_masks[0])):
            ys = [_dot(jnp.where(level_masks[dr][lvl], lm16[n], zero16), xs[n].astype(BF16))
                  for n, (m, dr) in enumerate(chains)]
            xs = [xs[n] - _dot(xs[n].astype(BF16), ys[n].astype(BF16)) for n in range(len(chains))]
        tmat = [x.astype(BF16) for x in xs]
        eg = [jnp.exp(x) for x in gc_col]
        g_last = [gc_col[n][c_len - 1:c_len] if dr == 0 else gc_col[n][0:1] for n, (m, dr) in enumerate(chains)]
        us = [_dot(tmat[n], (v[m] * beta[n]).astype(BF16)) for n, (m, dr) in enumerate(chains)]
        ws = [_dot(tmat[n], (k[m] * (beta[n] * eg[n])).astype(BF16)) for n, (m, dr) in enumerate(chains)]
        for n, (m, dr) in enumerate(chains):
            hh, c = units[m][0], cs[m]
            u_ref[dr, hh, rows[m], :] = us[n]
            wq_ref[dr, hh, pl.ds(pl.multiple_of(2 * c * c_len, c_len), c_len), :] = ws[n].astype(BF16)
            wq_ref[dr, hh, pl.ds(pl.multiple_of(2 * c * c_len + c_len, c_len), c_len), :] = (
                q[m] * eg[n]).astype(BF16)
            kd_ref[dr, hh, rows[m], :] = (k[m] * jnp.exp(g_last[n] - gc_col[n])).astype(BF16)
            in_ref[dr, hh, rows[m], :] = (qk[m] * decay[n]).astype(BF16)
            gl_ref[dr, hh, pl.ds(c, 1), :] = jnp.broadcast_to(jnp.exp(g_last[n]), (1, LANES))
        return carry

    lax.fori_loop(0, n_chunks // group, group_body, 0)

    seq_chains = [(dr, hh) for dr in range(2) for hh in range(nhs)]

    def step(i, states):
        cs = [i, n_chunks - 1 - i]
        rows = [pl.ds(pl.multiple_of(c * c_len, c_len), c_len) for c in cs]
        ws_qs = [_dot(wq_ref[dr, hh, pl.ds(pl.multiple_of(2 * cs[dr] * c_len, 2 * c_len), 2 * c_len), :],
                      states[n].astype(BF16)) for n, (dr, hh) in enumerate(seq_chains)]
        v16 = [(u_ref[dr, hh, rows[dr], :] - ws_qs[n][0:c_len]).astype(BF16)
               for n, (dr, hh) in enumerate(seq_chains)]
        intra = [_dot(in_ref[dr, hh, rows[dr], :], v16[n]) for n, (dr, hh) in enumerate(seq_chains)]
        upd = [_dot_tn(kd_ref[dr, hh, rows[dr], :], v16[n]) for n, (dr, hh) in enumerate(seq_chains)]
        for n, (dr, hh) in enumerate(seq_chains):
            acc_ref[hh, rows[dr], :] += ws_qs[n][c_len:2 * c_len] + intra[n]
        return tuple(states[n] * gl_ref[dr, hh, pl.ds(cs[dr], 1), :] + upd[n]
                     for n, (dr, hh) in enumerate(seq_chains))

    s_fin = lax.fori_loop(0, n_chunks, step, tuple(s0_ref[dr, hh] for dr, hh in seq_chains))
    for n, (dr, hh) in enumerate(seq_chains):
        sfin_ref[dr, hh] = s_fin[n]

    for hh in range(nhs):
        lanes = slice(hh * LANES, (hh + 1) * LANES)
        o = acc_ref[hh]
        y = o * lax.rsqrt(jnp.mean(o * o, axis=-1, keepdims=True) + EPS) * ng_ref[...]
        z = z_ref[:, lanes]
        o_ref[:, lanes] = y * (z * _sigmoid(z))


def _dn_call(proj, params, s0, s0_map, *, layer, batch, seq, nhs):
    nh = DN_HEADS
    nblk = nh // nhs
    width = nhs * LANES
    n_chunks = seq // DN_TILE
    vec = pl.BlockSpec((None, 1, LANES), lambda b, h: (layer, 0, 0))
    nsub = max(n_chunks, SUBLANES)
    return pl.pallas_call(
        functools.partial(_dn_kernel, seq=seq, nhs=nhs),
        grid=(batch, nblk),
        in_specs=[pl.BlockSpec((seq, width), lambda b, h: (b, COL_DQ * nblk + h)),
                  pl.BlockSpec((seq, width), lambda b, h: (b, (COL_DQ + 1) * nblk + h)),
                  pl.BlockSpec((seq, width), lambda b, h: (b, (COL_DQ + 2) * nblk + h)),
                  pl.BlockSpec((seq, width), lambda b, h: (b, COL_DZ * nblk + h)),
                  pl.BlockSpec((seq, LANES), lambda b, h: (b, COL_SC128)),
                  pl.BlockSpec((None, 3, CONV_W, width), lambda b, h: (layer, 0, 0, h)),
                  vec, vec, vec,
                  pl.BlockSpec((None, None, 2, nhs, DN_DK, DN_DK), s0_map)],
        out_specs=[pl.BlockSpec((seq, width), lambda b, h: (b, h)),
                   pl.BlockSpec((None, 2, nhs, DN_DK, DN_DK), lambda b, h: (b, 0, h, 0, 0))],
        out_shape=[jax.ShapeDtypeStruct((batch * seq, BRANCH_W), F32),
                   jax.ShapeDtypeStruct((batch, 2, nh, DN_DK, DN_DK), F32)],
        scratch_shapes=[pltpu.VMEM((seq + 2 * SUBLANES, LANES), F32),
                        pltpu.VMEM((nhs, seq, LANES), F32),
                        pltpu.VMEM((nhs, seq, LANES), F32),
                        pltpu.VMEM((nhs, seq, LANES), F32),
                        pltpu.VMEM((nhs, seq, LANES), F32),
                        pltpu.VMEM((2, nhs, seq, LANES), F32),
                        pltpu.VMEM((2, nhs, 2 * seq, LANES), BF16),
                        pltpu.VMEM((2, nhs, seq, LANES), BF16),
                        pltpu.VMEM((2, nhs, seq, DN_TILE), BF16),
                        pltpu.VMEM((2, nhs, nsub, LANES), F32),
                        pltpu.VMEM((nhs, seq, LANES), F32)],
        compiler_params=_params(2),
        name="deltanet",
    )(proj, proj, proj, proj, proj, params["dn_cw"], params["dn_alog"], params["dn_dtb"], params["dn_ng"], s0)


def _prepare_params(w_ffn_gate, w_ffn_up, w_ffn_down, w_in, lru_conv_w, lru_conv_b, lru_w_r, lru_b_r,
                    lru_w_i, lru_b_i, lru_lambda, gqa_q_norm, gqa_k_norm, na_rpb, dn_conv_w, dn_a_log,
                    dn_dt_bias, dn_norm_g, w_branch, w_out, norm_g):
    depth = w_in.shape[0]
    offs = np.cumsum((0,) + IN_WIDTHS)
    w_in_t = jnp.swapaxes(w_in, 1, 2).astype(BF16)
    seg = [w_in_t[:, offs[i]:offs[i + 1]] for i in range(len(IN_WIDTHS))]
    (a_x, a_y, b_q, b_k, b_v, c_q, c_k, c_v, d_q, d_k, d_v, d_z, d_b, d_a, g_lin) = seg
    perm = np.asarray(GQA_PERM)
    b_q = b_q.reshape(depth, GQA_HEADS, HEAD_DIM, D_MODEL)[:, perm].reshape(depth, BRANCH_W, D_MODEL)
    pad = jnp.zeros((depth, PROJ_W - 5120 - 2 * LANES - 16, D_MODEL), BF16)
    w_main = jnp.concatenate([a_x, a_y, b_q, c_q, c_k, c_v, d_q, d_k, d_v, d_z, b_k, b_v, d_b, d_a, pad], axis=1)

    def block_diag(wb):
        wb = wb.reshape(depth, 2, LRU_BLOCKS // 2, 2, LRU_BW, LRU_BW)
        z = jnp.zeros_like(wb[:, :, :, 0])
        return jnp.concatenate([jnp.concatenate([wb[:, :, :, 0], z], axis=-1),
                                jnp.concatenate([z, wb[:, :, :, 1]], axis=-1)], axis=-2)

    wr, wi = block_diag(lru_w_r), block_diag(lru_w_i)
    lru_wg = (0.5 * jnp.concatenate([wr[:, 0], wi[:, 0], wr[:, 1], wi[:, 1]], axis=-1)).astype(BF16)
    ncol = BRANCH_W // LANES

    def col_blocks(v):
        return v.reshape(depth, ncol, LANES)

    lru_bg = 0.5 * jnp.concatenate([col_blocks(lru_b_r[:, 0]), col_blocks(lru_b_i[:, 0]),
                                    col_blocks(lru_b_r[:, 1]), col_blocks(lru_b_i[:, 1])], axis=-1)[:, :, None, :]

    wb_b = w_branch[:, 1].reshape(depth, GQA_HEADS, HEAD_DIM, D_MODEL)[:, perm].reshape(depth, BRANCH_W, D_MODEL)
    wb = jnp.stack([w_branch[:, 0], wb_b, w_branch[:, 2], w_branch[:, 3]], axis=1).astype(BF16)

    lane_pad = jnp.zeros((depth, LANES - 4 * DN_HEADS), F32)
    lane_zero = jnp.zeros((depth, 2 * DN_HEADS), F32)
    alog = jnp.concatenate([lane_zero, dn_a_log.reshape(depth, -1), lane_pad], axis=1)[:, None, :]
    dtb = jnp.concatenate([lane_zero, dn_dt_bias.reshape(depth, -1), lane_pad], axis=1)[:, None, :]

    return dict(
        norm_g=norm_g[:, :, None, :],
        wg=w_ffn_gate.astype(BF16), wu=w_ffn_up.astype(BF16), wd=w_ffn_down.astype(BF16),
        w_main=w_main, w_gate=g_lin,
        lru_cw=lru_conv_w, lru_cb=lru_conv_b[:, None, :], lru_wg=lru_wg, lru_bg=lru_bg, lru_lam=lru_lambda,
        gq=jnp.tile(gqa_q_norm, (1, 2))[:, None, :], gk=jnp.tile(gqa_k_norm, (1, 2))[:, None, :],
        na_bias=_na_bias_table(na_rpb),
        dn_cw=dn_conv_w.reshape(depth, CONV_W, 3, BRANCH_W).transpose(0, 2, 1, 3),
        dn_alog=alog, dn_dtb=dtb, dn_ng=dn_norm_g[:, None, :],
        wb=wb, w_out=w_out.astype(BF16),
    )


def _rope_tables(seq):
    pos = jnp.arange(seq)
    half = HEAD_DIM // 2
    inv = jnp.power(ROPE_BASE, -jnp.arange(0, half, 2, dtype=F32) / half)
    ang_r = (pos // GRID_W).astype(F32)[:, None] * inv[None, :]
    ang_c = (pos % GRID_W).astype(F32)[:, None] * inv[None, :]
    cos = jnp.concatenate([jnp.cos(ang_r)] * 2 + [jnp.cos(ang_c)] * 2, axis=-1)
    sin = jnp.concatenate([-jnp.sin(ang_r), jnp.sin(ang_r), -jnp.sin(ang_c), jnp.sin(ang_c)], axis=-1)
    return jnp.tile(cos, (1, 2)), jnp.tile(sin, (1, 2))


def _na_bias_table(rpb):
    qc = np.arange(GRID_W)
    cs = np.clip(qc - NA_WIN_C // 2, 0, GRID_W - NA_WIN_C)
    kc = np.arange(GRID_W)
    inwin = (kc[None, :] >= cs[:, None]) & (kc[None, :] < cs[:, None] + NA_WIN_C)
    coff = kc[None, :] - qc[:, None] + NA_WIN_C - 1
    onehot = (coff[None] == np.arange(2 * NA_WIN_C - 1)[:, None, None]).astype(np.float32)
    t = jnp.einsum("lhrd,dqk->lhrqk", rpb.astype(F32), onehot, precision=lax.Precision.HIGHEST)
    t = jnp.where(inwin, t, NEG_BIG)
    edge = jnp.full_like(t[:, :, :1], NEG_BIG)
    t = jnp.concatenate([edge, t, edge], axis=2)
    return jnp.concatenate([t[:, :, :-1], t[:, :, 1:]], axis=-1)


def _layer(x, mod_all, params, *, batch, seq, latent, layer, caches, tables, final_g, tm):
    row0 = 1 if latent else 0
    x = _ffn_call(x, mod_all, params, final_g, layer=layer, which=0, row0=row0, final=False, tm=tm)
    proj = _inproj_call(x, mod_all, params, layer=layer, row0=row0, tm=tm)

    dn_heads_per_step = 1 if seq > 1024 else DN_HEADS
    if latent:
        cache_ak, cache_av, cache_nk, cache_nv, state_lru, state_delta = caches
        h0, h0_map = state_lru, lambda b, c: (b, layer, 0, c)
        s0, s0_map = state_delta, lambda b, h: (b, layer, 0, h, 0, 0)
    else:
        h0, h0_map = jnp.zeros((1, 1, 2, BRANCH_W), F32), lambda b, c: (0, 0, 0, c)
        s0 = jnp.zeros((1, 1, 2, dn_heads_per_step, DN_DK, DN_DK), F32)
        s0_map = lambda b, h: (0, 0, 0, 0, 0, 0)

    o_a, lru_fin = _lru_call(proj, h0, h0_map, params, layer=layer, batch=batch, seq=seq)

    cos, sin = tables["rope"] if latent else (None, None)
    qn, kn = _prep_call(proj, params, cos, sin, layer=layer, batch=batch, seq=seq, rope=latent)
    tq = min(seq, ATTN_QROWS)
    kv_new = (kn, pl.BlockSpec((seq, LANES), lambda b, i: (b, 0)),
              proj, pl.BlockSpec((seq, LANES), lambda b, i: (b, COL_BV128)))
    if latent:
        gqa_cache = pl.BlockSpec((None, None, PAST_LEN, LANES), lambda b, i: (b, layer, 0, 0))
        sources = [kv_new, (cache_ak, gqa_cache, cache_av, gqa_cache)]
    else:
        sources = [kv_new]
    o_b = _attn_call(qn, 0, sources, batch=batch, seq=seq, tq=tq, nkb=1, qscale=1.0)

    if latent:
        o_c = _na_call(proj, cache_nk, cache_nv, params, layer=layer, batch=batch, seq=seq)
    else:
        src = (proj, pl.BlockSpec((seq, BRANCH_W), lambda b, i: (b, COL_CK)),
               proj, pl.BlockSpec((seq, BRANCH_W), lambda b, i: (b, COL_CV)))
        o_c = _attn_call(proj, COL_CQ, [src], batch=batch, seq=seq, tq=tq, nkb=4, qscale=HEAD_DIM ** -0.5)

    o_d, dn_fin = _dn_call(proj, params, s0, s0_map, layer=layer, batch=batch, seq=seq, nhs=dn_heads_per_step)

    x = _merge_call(x, mod_all, params, (o_a, o_b, o_c, o_d), layer=layer, row0=row0, tm=tm)
    x = _ffn_call(x, mod_all, params, final_g, layer=layer, which=1, row0=row0,
                  final=(layer == DEPTH - 1), tm=tm)

    new_ctx = None
    if not latent:
        new_ctx = (kn.reshape(batch, seq, GQA_KV, HEAD_DIM),
                   proj[:, COL_BV128 * LANES:(COL_BV128 + 1) * LANES].reshape(batch, seq, GQA_KV, HEAD_DIM),
                   proj[:, COL_CK * BRANCH_W:(COL_CK + 1) * BRANCH_W].reshape(batch, seq, NA_HEADS, HEAD_DIM),
                   proj[:, COL_CV * BRANCH_W:(COL_CV + 1) * BRANCH_W].reshape(batch, seq, NA_HEADS, HEAD_DIM),
                   lru_fin, dn_fin)
    return x, new_ctx


def kernel(x_prompt, x_sample, c, cache_attn_k, cache_attn_v, cache_na_k, cache_na_v, state_lru, state_delta, c_ctx, w_mod, b_mod, norm_g, w_ffn_gate, w_ffn_up, w_ffn_down, w_in, lru_conv_w, lru_conv_b, lru_w_r, lru_b_r, lru_w_i, lru_b_i, lru_lambda, gqa_q_norm, gqa_k_norm, na_rpb, dn_conv_w, dn_a_log, dn_dt_bias, dn_norm_g, w_branch, w_out, final_norm_g):
    batch_c, seq_c, _ = x_prompt.shape
    batch_l, seq_l, _ = x_sample.shape
    assert batch_l + 1 <= SUBLANES

    cs = jnp.concatenate([c_ctx[None, :], c, jnp.zeros((SUBLANES - 1 - batch_l, D_MODEL), F32)], axis=0)
    mod_all = _mod_call(cs, w_mod, b_mod).reshape(DEPTH, SUBLANES, N_MOD, D_MODEL)
    params = _prepare_params(w_ffn_gate, w_ffn_up, w_ffn_down, w_in, lru_conv_w, lru_conv_b, lru_w_r, lru_b_r,
                             lru_w_i, lru_b_i, lru_lambda, gqa_q_norm, gqa_k_norm, na_rpb, dn_conv_w, dn_a_log,
                             dn_dt_bias, dn_norm_g, w_branch, w_out, norm_g)

    caches = (cache_attn_k.reshape(batch_l, DEPTH, PAST_LEN, GQA_KV * HEAD_DIM),
              cache_attn_v.reshape(batch_l, DEPTH, PAST_LEN, GQA_KV * HEAD_DIM),
              cache_na_k.reshape(batch_l, DEPTH, PAST_LEN, BRANCH_W),
              cache_na_v.reshape(batch_l, DEPTH, PAST_LEN, BRANCH_W),
              state_lru, state_delta)
    tables = {"rope": _rope_tables(seq_l)}
    final_g = final_norm_g[None, :]

    xc = x_prompt.reshape(1, batch_c * seq_c, D_MODEL)
    xl = x_sample
    ctx_out = []
    for l in range(DEPTH):
        xc, new_ctx = _layer(xc, mod_all, params, batch=batch_c, seq=seq_c, latent=False, layer=l, caches=None,
                             tables=tables, final_g=final_g, tm=DENSE_ROWS)
        ctx_out.append(new_ctx)
        xl, _ = _layer(xl, mod_all, params, batch=batch_l, seq=seq_l, latent=True, layer=l, caches=caches,
                       tables=tables, final_g=final_g, tm=DENSE_ROWS)

    stacked = [jnp.stack([ctx_out[l][i] for l in range(DEPTH)], axis=1) for i in range(6)]
    return (xc.reshape(batch_c, seq_c, D_MODEL), xl, *stacked)
```

```python
import functools

import numpy as np
import jax
import jax.numpy as jnp
from jax import lax
from jax.experimental import pallas as pl
from jax.experimental.pallas import tpu as pltpu

F32 = jnp.float32
BF16 = jnp.bfloat16

D_MODEL = 1024
DEPTH = 4
GRID_W = 64
BRANCH_W = 512
N_MOD = 9
D_FF = 2816
EPS = 1e-6
CONV_W = 4
LRU_BLOCKS = 8
LRU_BW = 64
LRU_C = 8.0
HEAD_DIM = 64
GQA_HEADS = 8
GQA_KV = 2
ROPE_BASE = 10000.0
NA_HEADS = 8
NA_WIN_R = 8
NA_WIN_C = 16
NA_QROWS = 4
NA_KROWS = 12
DN_DK = 128
DN_HEADS = 4
LRU_PITCH_PAD = 8
DN_TILE = 128
DN_CHAINS = 16
PAST_LEN = 512
IN_WIDTHS = (512, 512, 512, 128, 128, 512, 512, 512, 512, 512, 512, 512, 8, 8, 4096)

LANES = 128
SUBLANES = 8
DENSE_ROWS = 512
DENSE_PARTS = 2
ATTN_QROWS = 512
PROJ_W = 5632
COL_BQ, COL_CQ, COL_CK, COL_CV, COL_DQ, COL_DZ = 2, 3, 4, 5, 6, 9
COL_BK128, COL_BV128, COL_SC128 = 40, 41, 42
VMEM_LIMIT = 56 * 1024 * 1024
NEG_BIG = -1e30
GQA_PERM = (0, 4, 1, 5, 2, 6, 3, 7)


def _params(n):
    return pltpu.CompilerParams(dimension_semantics=("arbitrary",) * n, vmem_limit_bytes=VMEM_LIMIT)


def _const_spec(shape):
    nd = len(shape)
    return pl.BlockSpec(shape, lambda *_: (0,) * nd, pipeline_mode=pl.Buffered(1))


def _layer_spec(arr, layer, *sub):
    tail = arr.shape[1 + len(sub):]
    index = (layer,) + tuple(sub) + (0,) * len(tail)
    return pl.BlockSpec((None,) * (1 + len(sub)) + tuple(tail), lambda *_: index, pipeline_mode=pl.Buffered(1))


def _mod_spec(layer, row0):
    return pl.BlockSpec((None, None, N_MOD, D_MODEL), lambda b, i: (layer, row0 + b, 0, 0))


def _dot(a, b):
    return jnp.dot(a, b, preferred_element_type=F32)


def _dot_nt(a, b):
    return lax.dot_general(a, b, (((1,), (1,)), ((), ())), preferred_element_type=F32)


def _dot_tn(a, b):
    return lax.dot_general(a, b, (((0,), (0,)), ((), ())), preferred_element_type=F32)


def _split(x):
    hi = x.astype(BF16)
    lo = (x - hi.astype(F32)).astype(BF16)
    return hi, lo


def _dot3(a, b):
    ah, al = _split(a)
    bh, bl = _split(b)
    return _dot(ah, bh) + (_dot(al, bh) + _dot(ah, bl))


def _sigmoid(x):
    return 0.5 * jnp.tanh(0.5 * x) + 0.5


def _modnorm(x, g, shift, scale):
    ms = jnp.mean(x * x, axis=-1, keepdims=True)
    return (x * lax.rsqrt(ms + EPS) * g) * (1.0 + scale) + shift


def _mod_kernel(c_ref, w_ref, b_ref, o_ref):
    c = c_ref[...]
    o_ref[0] = _dot3(c * _sigmoid(c), w_ref[0]) + b_ref[0]


def _mod_call(cs, w_mod, b_mod):
    tn = 1024
    n = N_MOD * D_MODEL
    return pl.pallas_call(
        _mod_kernel,
        grid=(DEPTH, n // tn),
        in_specs=[pl.BlockSpec((SUBLANES, D_MODEL), lambda l, j: (0, 0)),
                  pl.BlockSpec((1, D_MODEL, tn), lambda l, j: (l, 0, j)),
                  pl.BlockSpec((1, 1, tn), lambda l, j: (l, 0, j))],
        out_specs=pl.BlockSpec((1, SUBLANES, tn), lambda l, j: (l, 0, j)),
        out_shape=jax.ShapeDtypeStruct((DEPTH, SUBLANES, n), F32),
        compiler_params=_params(2),
        name="mod",
    )(cs, w_mod, b_mod.reshape(DEPTH, 1, n))


def _ffn_kernel(x_ref, mod_ref, g_ref, wg_ref, wu_ref, wd_ref, gf_ref, o_ref, *, mi, final):
    mod = mod_ref[...]
    rows = x_ref.shape[1]
    parts = [pl.ds(n * (rows // DENSE_PARTS), rows // DENSE_PARTS) for n in range(DENSE_PARTS)]
    xs = [x_ref[0, p, :] for p in parts]
    hs = [_modnorm(x, g_ref[...], mod[mi:mi + 1], mod[mi + 1:mi + 2]).astype(BF16) for x in xs]
    acts = []
    for h in hs:
        gt = _dot(h, wg_ref[...])
        up = _dot(h, wu_ref[...])
        acts.append((gt * _sigmoid(gt) * up).astype(BF16))
    downs = [_dot(a, wd_ref[...]) for a in acts]
    for p, x, d in zip(parts, xs, downs):
        y = x + 0.5 * mod[mi + 2:mi + 3] * d
        if final:
            ms = jnp.mean(y * y, axis=-1, keepdims=True)
            y = y * lax.rsqrt(ms + EPS) * gf_ref[...]
        o_ref[0, p, :] = y


def _ffn_call(x, mod_all, params, gf, *, layer, which, row0, final, tm):
    nb, rows, _ = x.shape
    return pl.pallas_call(
        functools.partial(_ffn_kernel, mi=6 * which, final=final),
        grid=(nb, rows // tm),
        in_specs=[pl.BlockSpec((1, tm, D_MODEL), lambda b, i: (b, i, 0)),
                  _mod_spec(layer, row0),
                  _layer_spec(params["norm_g"], layer, 2 * which),
                  _layer_spec(params["wg"], layer, which),
                  _layer_spec(params["wu"], layer, which),
                  _layer_spec(params["wd"], layer, which),
                  _const_spec((1, D_MODEL))],
        out_specs=pl.BlockSpec((1, tm, D_MODEL), lambda b, i: (b, i, 0)),
        out_shape=jax.ShapeDtypeStruct(x.shape, F32),
        compiler_params=_params(2),
        name="ffn",
    )(x, mod_all, params["norm_g"], params["wg"], params["wu"], params["wd"], gf)


def _inproj_kernel(x_ref, mod_ref, g_ref, w_ref, o_ref):
    mod = mod_ref[...]
    h = _modnorm(x_ref[0], g_ref[...], mod[3:4], mod[4:5]).astype(BF16)
    o_ref[...] = _dot_nt(h, w_ref[...])


def _inproj_call(x, mod_all, params, *, layer, row0, tm):
    nb, rows, _ = x.shape
    nt = rows // tm
    return pl.pallas_call(
        _inproj_kernel,
        grid=(nb, nt),
        in_specs=[pl.BlockSpec((1, tm, D_MODEL), lambda b, i: (b, i, 0)),
                  _mod_spec(layer, row0),
                  _layer_spec(params["norm_g"], layer, 1),
                  _layer_spec(params["w_main"], layer)],
        out_specs=pl.BlockSpec((tm, PROJ_W), lambda b, i: (b * nt + i, 0)),
        out_shape=jax.ShapeDtypeStruct((nb * rows, PROJ_W), F32),
        compiler_params=_params(2),
        name="inproj",
    )(x, mod_all, params["norm_g"], params["w_main"])


def _merge_kernel(x_ref, mod_ref, g_ref, oa_ref, ob_ref, oc_ref, od_ref, wgate_ref, wb_ref, wout_ref, o_ref):
    x = x_ref[0]
    mod = mod_ref[...]
    h = _modnorm(x, g_ref[...], mod[3:4], mod[4:5]).astype(BF16)
    acc = None
    for n, ref in enumerate((oa_ref, ob_ref, oc_ref, od_ref)):
        gate = _sigmoid(_dot_nt(h, wgate_ref[n * D_MODEL:(n + 1) * D_MODEL, :]))
        term = gate * _dot(ref[...].astype(BF16), wb_ref[n])
        acc = term if acc is None else acc + term
    o_ref[0] = x + mod[5:6] * _dot(acc.astype(BF16), wout_ref[...])


def _merge_call(x, mod_all, params, outs, *, layer, row0, tm):
    nb, rows, _ = x.shape
    nt = rows // tm
    ospec = pl.BlockSpec((tm, BRANCH_W), lambda b, i: (b * nt + i, 0))
    return pl.pallas_call(
        _merge_kernel,
        grid=(nb, nt),
        in_specs=[pl.BlockSpec((1, tm, D_MODEL), lambda b, i: (b, i, 0)),
                  _mod_spec(layer, row0),
                  _layer_spec(params["norm_g"], layer, 1),
                  ospec, ospec, ospec, ospec,
                  _layer_spec(params["w_gate"], layer),
                  _layer_spec(params["wb"], layer),
                  _layer_spec(params["w_out"], layer)],
        out_specs=pl.BlockSpec((1, tm, D_MODEL), lambda b, i: (b, i, 0)),
        out_shape=jax.ShapeDtypeStruct(x.shape, F32),
        compiler_params=_params(2),
        name="merge",
    )(x, mod_all, params["norm_g"], *outs, params["w_gate"], params["wb"], params["w_out"])


def _log_sigmoid(x):
    return jnp.minimum(x, 0.0) - jnp.log1p(jnp.exp(-jnp.abs(x)))


def _gelu_tanh(x):
    return x * (0.5 * (1.0 + jnp.tanh(0.7978845608028654 * (x + 0.044715 * (x * x * x)))))


def _lru_kernel(ax_ref, ay_ref, h0_ref, cw_ref, cb_ref, wg_ref, bg_ref, lam_ref, o_ref, fin_ref,
                xp_ref, a_ref, u_ref, h_ref, p_ref, *, seq):
    lc = seq // SUBLANES
    pitch = lc + LRU_PITCH_PAD
    zero8 = jnp.zeros((SUBLANES, LANES), F32)
    xp_ref[0:SUBLANES, :] = zero8
    xp_ref[SUBLANES + seq:2 * SUBLANES + seq, :] = zero8
    xp_ref[SUBLANES:SUBLANES + seq, :] = ax_ref[...]
    cw = cw_ref[...]
    xa = cb_ref[...] + cw[0:1] * xp_ref[pl.ds(SUBLANES - 2, seq), :]
    for k in range(1, CONV_W):
        xa = xa + cw[k:k + 1] * xp_ref[pl.ds(SUBLANES - 2 + k, seq), :]
    half = _dot(xa.astype(BF16), wg_ref[...]) + bg_ref[...]
    lam = lam_ref[...]
    xh = 0.5 * xa
    for dr in range(2):
        tr = jnp.tanh(half[:, (2 * dr) * LANES:(2 * dr + 1) * LANES])
        ti = jnp.tanh(half[:, (2 * dr + 1) * LANES:(2 * dr + 2) * LANES])
        c_half = (0.5 * LRU_C) * _log_sigmoid(lam[dr:dr + 1])
        log_a = tr * c_half + c_half
        a = jnp.exp(log_a)
        one_m_a2 = -jnp.tanh(log_a) * (a * a + 1.0)
        root = jnp.where(one_m_a2 > 0.0, one_m_a2 * lax.rsqrt(one_m_a2), 0.0)
        u = root * ((ti + 1.0) * xh)
        for k in range(SUBLANES):
            a_ref[dr, pl.ds(k * pitch, lc), :] = a[k * lc:(k + 1) * lc]
            u_ref[dr, pl.ds(k * pitch, lc), :] = u[k * lc:(k + 1) * lc]

    sub = lax.broadcasted_iota(jnp.int32, (SUBLANES, LANES), 0)
    h0 = h0_ref[...]
    hf0 = jnp.where(sub == 0, h0[0:1], 0.0)
    hb0 = jnp.where(sub == SUBLANES - 1, h0[1:2], 0.0)
    ones = jnp.ones((SUBLANES, LANES), F32)

    def body(s, carry):
        hf, pf, hb, pb = carry
        rows_f = pl.ds(s, SUBLANES, stride=pitch)
        rows_b = pl.ds(lc - 1 - s, SUBLANES, stride=pitch)
        af = a_ref[0, rows_f, :]
        hf = af * hf + u_ref[0, rows_f, :]
        pf = af * pf
        h_ref[0, rows_f, :] = hf
        p_ref[0, rows_f, :] = pf
        ab = a_ref[1, rows_b, :]
        hb = ab * hb + u_ref[1, rows_b, :]
        pb = ab * pb
        h_ref[1, rows_b, :] = hb
        p_ref[1, rows_b, :] = pb
        return hf, pf, hb, pb

    hf, pf, hb, pb = lax.fori_loop(0, lc, body, (hf0, ones, hb0, ones), unroll=8)

    cf = [jnp.zeros((1, LANES), F32)]
    for k in range(1, SUBLANES):
        cf.append(hf[k - 1:k] + pf[k - 1:k] * cf[k - 1])
    fin_ref[0:1, :] = hf[SUBLANES - 1:SUBLANES] + pf[SUBLANES - 1:SUBLANES] * cf[SUBLANES - 1]
    cb = [None] * SUBLANES
    cb[SUBLANES - 1] = jnp.zeros((1, LANES), F32)
    for k in range(SUBLANES - 2, -1, -1):
        cb[k] = hb[k + 1:k + 2] + pb[k + 1:k + 2] * cb[k + 1]
    fin_ref[1:2, :] = hb[0:1] + pb[0:1] * cb[0]

    for k in range(SUBLANES):
        rows = pl.ds(k * lc, lc)
        held = pl.ds(k * pitch, lc)
        h = (h_ref[0, held, :] + p_ref[0, held, :] * cf[k]) + (h_ref[1, held, :] + p_ref[1, held, :] * cb[k])
        o_ref[rows, :] = h * _gelu_tanh(ay_ref[rows, :])


def _lru_columns_kernel(ax_ref, ay_ref, h0_ref, cw_ref, cb_ref, wg_ref, bg_ref, lam_ref, o_ref, fin_ref,
                        *scratch, seq, ncb):
    for j in range(ncb):
        lanes = pl.ds(j * LANES, LANES)
        _lru_kernel(ax_ref.at[:, lanes], ay_ref.at[:, lanes], h0_ref.at[:, lanes], cw_ref.at[:, lanes],
                    cb_ref.at[:, lanes], wg_ref.at[j], bg_ref.at[j], lam_ref.at[:, lanes],
                    o_ref.at[:, lanes], fin_ref.at[:, lanes], *scratch, seq=seq)


def _lru_call(proj, h0, h0_map, params, *, layer, batch, seq, ncb):
    ncol = BRANCH_W // (LANES * ncb)
    width = LANES * ncb
    pad_rows = SUBLANES * LRU_PITCH_PAD
    return pl.pallas_call(
        functools.partial(_lru_columns_kernel, seq=seq, ncb=ncb),
        grid=(batch, ncol),
        in_specs=[pl.BlockSpec((seq, width), lambda b, c: (b, c)),
                  pl.BlockSpec((seq, width), lambda b, c: (b, ncol + c)),
                  pl.BlockSpec((None, None, 2, width), h0_map),
                  pl.BlockSpec((None, CONV_W, width), lambda b, c: (layer, 0, c)),
                  pl.BlockSpec((None, 1, width), lambda b, c: (layer, 0, c)),
                  pl.BlockSpec((None, ncb, LANES, 4 * LANES), lambda b, c: (layer, c, 0, 0)),
                  pl.BlockSpec((None, ncb, 1, 4 * LANES), lambda b, c: (layer, c, 0, 0)),
                  pl.BlockSpec((None, 2, width), lambda b, c: (layer, 0, c))],
        out_specs=[pl.BlockSpec((seq, width), lambda b, c: (b, c)),
                   pl.BlockSpec((None, 2, width), lambda b, c: (b, 0, c))],
        out_shape=[jax.ShapeDtypeStruct((batch * seq, BRANCH_W), F32),
                   jax.ShapeDtypeStruct((batch, 2, BRANCH_W), F32)],
        scratch_shapes=[pltpu.VMEM((seq + 2 * SUBLANES, LANES), F32),
                        pltpu.VMEM((2, seq + pad_rows, LANES), F32),
                        pltpu.VMEM((2, seq + pad_rows, LANES), F32),
                        pltpu.VMEM((2, seq + pad_rows, LANES), F32),
                        pltpu.VMEM((2, seq + pad_rows, LANES), F32)],
        compiler_params=_params(2),
        name="lru",
    )(proj, proj, h0, params["lru_cw"], params["lru_cb"], params["lru_wg"], params["lru_bg"], params["lru_lam"])


def _prep_kernel(*refs, rope):
    if rope:
        q_ref, k_ref, gq_ref, gk_ref, cos_ref, sin_ref, qn_ref, kn_ref = refs
    else:
        q_ref, k_ref, gq_ref, gk_ref, qn_ref, kn_ref = refs
    rows = q_ref.shape[0]
    lane = lax.broadcasted_iota(jnp.int32, (rows, LANES), 1)
    lo = lane < HEAD_DIM
    first16 = (lane & 16) == 0

    def head_norm(x, g):
        sq = x * x
        s_lo = jnp.sum(jnp.where(lo, sq, 0.0), axis=-1, keepdims=True)
        s_hi = jnp.sum(jnp.where(lo, 0.0, sq), axis=-1, keepdims=True)
        ms = jnp.where(lo, s_lo, s_hi) * (1.0 / HEAD_DIM)
        return x * lax.rsqrt(ms + EPS) * g

    def rotate(y):
        if not rope:
            return y
        partner = jnp.where(first16, pltpu.roll(y, LANES - 16, 1), pltpu.roll(y, 16, 1))
        return y * cos_ref[...] + partner * sin_ref[...]

    for p in range(BRANCH_W // LANES):
        cols = slice(p * LANES, (p + 1) * LANES)
        qn_ref[:, cols] = rotate(head_norm(q_ref[:, cols], gq_ref[...])) * (HEAD_DIM ** -0.5)
    kn_ref[...] = rotate(head_norm(k_ref[...], gk_ref[...]))


def _prep_call(proj, params, cos, sin, *, layer, batch, seq, rope):
    tab = pl.BlockSpec((seq, LANES), lambda b: (0, 0))
    in_specs = [pl.BlockSpec((seq, BRANCH_W), lambda b: (b, COL_BQ)),
                pl.BlockSpec((seq, LANES), lambda b: (b, COL_BK128)),
                _layer_spec(params["gq"], layer), _layer_spec(params["gk"], layer)]
    args = [proj, proj, params["gq"], params["gk"]]
    if rope:
        in_specs += [tab, tab]
        args += [cos, sin]
    return pl.pallas_call(
        functools.partial(_prep_kernel, rope=rope),
        grid=(batch,),
        in_specs=in_specs,
        out_specs=[pl.BlockSpec((seq, BRANCH_W), lambda b: (b, 0)),
                   pl.BlockSpec((seq, LANES), lambda b: (b, 0))],
        out_shape=[jax.ShapeDtypeStruct((batch * seq, BRANCH_W), F32),
                   jax.ShapeDtypeStruct((batch * seq, LANES), F32)],
        compiler_params=_params(1),
        name="attn_prep",
    )(*args)


def _softmax_pv(scores, values):
    m = None
    for s in scores:
        sm = jnp.max(s, axis=-1, keepdims=True)
        m = sm if m is None else jnp.maximum(m, sm)
    den = None
    out = None
    for s, v in zip(scores, values):
        p = jnp.exp(s - m)
        ps = jnp.sum(p, axis=-1, keepdims=True)
        den = ps if den is None else den + ps
        o = _dot(p.astype(BF16), v)
        out = o if out is None else out + o
    return out / den


def _attn_kernel(*refs, nsrc, nkb, qscale):
    q_ref = refs[0]
    src = refs[1:1 + 2 * nsrc]
    o_ref = refs[1 + 2 * nsrc]
    tq = q_ref.shape[0]
    lo = lax.broadcasted_iota(jnp.int32, (tq, LANES), 1) < HEAD_DIM

    loaded = {}

    def kv(p):
        col = p if nkb != 1 else 0
        if col not in loaded:
            cols = slice(col * LANES, (col + 1) * LANES)
            loaded[col] = ([src[2 * i][:, cols].astype(BF16) for i in range(nsrc)],
                           [src[2 * i + 1][:, cols].astype(BF16) for i in range(nsrc)])
        return loaded[col]

    def scores(unit):
        p, half = unit
        qb = q_ref[:, p * LANES:(p + 1) * LANES]
        if qscale != 1.0:
            qb = qb * qscale
        qm = jnp.where(lo if half == 0 else jnp.logical_not(lo), qb, 0.0).astype(BF16)
        return [_dot_nt(qm, k) for k in kv(p)[0]]

    units = [(p, half) for p in range(BRANCH_W // LANES) for half in range(2)]
    pending = scores(units[0])
    first_half = None
    for n, (p, half) in enumerate(units):
        current = pending
        if n + 1 < len(units):
            pending = scores(units[n + 1])
        out = _softmax_pv(current, kv(p)[1])
        if half == 0:
            first_half = out
        else:
            o_ref[:, p * LANES:(p + 1) * LANES] = jnp.where(lo, first_half, out)


def _attn_call(q_arr, q_col, sources, *, batch, seq, tq, nkb, qscale):
    nq = seq // tq
    in_specs = [pl.BlockSpec((tq, BRANCH_W), lambda b, i: (b * nq + i, q_col))]
    args = [q_arr]
    for k_arr, k_spec, v_arr, v_spec in sources:
        in_specs += [k_spec, v_spec]
        args += [k_arr, v_arr]
    return pl.pallas_call(
        functools.partial(_attn_kernel, nsrc=len(sources), nkb=nkb, qscale=qscale),
        grid=(batch, nq),
        in_specs=in_specs,
        out_specs=pl.BlockSpec((tq, BRANCH_W), lambda b, i: (b * nq + i, 0)),
        out_shape=jax.ShapeDtypeStruct((batch * seq, BRANCH_W), F32),
        compiler_params=_params(2),
        name="attn",
    )(*args)


def _na_row_start(r, rows):
    return jnp.clip(r - NA_WIN_R // 2, 0, rows - NA_WIN_R)


def _na_kernel(q_ref, k_ref, v_ref, ck_ref, cv_ref, bias_ref, o_ref, *, rows):
    r0 = pl.program_id(1) * NA_QROWS
    w0 = jnp.clip(r0 - NA_WIN_R // 2, 0, rows - NA_KROWS)
    win = pl.ds(pl.multiple_of(w0 * GRID_W, GRID_W), NA_KROWS * GRID_W)
    lo = lax.broadcasted_iota(jnp.int32, (NA_QROWS * GRID_W, LANES), 1) < HEAD_DIM
    lo_row = lax.broadcasted_iota(jnp.int32, (GRID_W, LANES), 1) < HEAD_DIM
    ncol = BRANCH_W // LANES
    scores = []
    for p in range(ncol):
        cols = slice(p * LANES, (p + 1) * LANES)
        qb = q_ref[:, cols] * (HEAD_DIM ** -0.5)
        kw = k_ref[win, cols].astype(BF16)
        kc = ck_ref[:, cols].astype(BF16)
        for half in range(2):
            qm = jnp.where(lo if half == 0 else jnp.logical_not(lo), qb, 0.0).astype(BF16)
            scores.append((_dot_nt(qm, kw), _dot_nt(qm, kc)))
    pair_index, pair_mask = [], []
    for j in range(NA_QROWS):
        r = r0 + j
        rs = _na_row_start(r, rows)
        idx_j, mask_j = [], []
        for t in range(NA_KROWS // 2):
            kr = w0 + 2 * t
            rel = kr - r + NA_WIN_R - 1
            idx_j.append(jnp.clip(rel + 1, 0, 2 * NA_WIN_R - 1))
            in_a = jnp.where((kr >= rs) & (kr < rs + NA_WIN_R), 0.0, NEG_BIG)
            in_b = jnp.where((kr + 1 >= rs) & (kr + 1 < rs + NA_WIN_R), 0.0, NEG_BIG)
            mask_j.append(jnp.where(lo_row, in_a, in_b))
        pair_index.append(idx_j)
        pair_mask.append(mask_j)
    probs = []
    for h, (s_loc, s_ctx) in enumerate(scores):
        bias = jnp.concatenate(
            [jnp.concatenate([bias_ref[h, pair_index[j][t]] + pair_mask[j][t] for t in range(NA_KROWS // 2)],
                             axis=-1) for j in range(NA_QROWS)], axis=0)
        s_loc = s_loc + bias
        m = jnp.maximum(jnp.max(s_loc, axis=-1, keepdims=True), jnp.max(s_ctx, axis=-1, keepdims=True))
        p_loc = jnp.exp(s_loc - m)
        p_ctx = jnp.exp(s_ctx - m)
        den = jnp.sum(p_loc, axis=-1, keepdims=True) + jnp.sum(p_ctx, axis=-1, keepdims=True)
        probs.append((p_loc.astype(BF16), p_ctx.astype(BF16), den))
    for p in range(ncol):
        cols = slice(p * LANES, (p + 1) * LANES)
        vw = v_ref[win, cols].astype(BF16)
        vc = cv_ref[:, cols].astype(BF16)
        halves = []
        for half in range(2):
            p_loc, p_ctx, den = probs[2 * p + half]
            halves.append((_dot(p_loc, vw) + _dot(p_ctx, vc)) / den)
        o_ref[:, cols] = jnp.where(lo, halves[0], halves[1])


def _na_call(proj, cache_k, cache_v, params, *, layer, batch, seq):
    rows = seq // GRID_W
    steps = rows // NA_QROWS
    qrows = NA_QROWS * GRID_W
    cache_spec = pl.BlockSpec((None, None, PAST_LEN, BRANCH_W), lambda b, r: (b, layer, 0, 0))
    return pl.pallas_call(
        functools.partial(_na_kernel, rows=rows),
        grid=(batch, steps),
        in_specs=[pl.BlockSpec((qrows, BRANCH_W), lambda b, r: (b * steps + r, COL_CQ)),
                  pl.BlockSpec((seq, BRANCH_W), lambda b, r: (b, COL_CK)),
                  pl.BlockSpec((seq, BRANCH_W), lambda b, r: (b, COL_CV)),
                  cache_spec, cache_spec,
                  _layer_spec(params["na_bias"], layer)],
        out_specs=pl.BlockSpec((qrows, BRANCH_W), lambda b, r: (b * steps + r, 0)),
        out_shape=jax.ShapeDtypeStruct((batch * seq, BRANCH_W), F32),
        compiler_params=_params(2),
        name="na",
    )(proj, proj, proj, cache_k, cache_v, params["na_bias"])


def _split3(x):
    hi = x.astype(BF16)
    r = x - hi.astype(F32)
    mid = r.astype(BF16)
    lo = (r - mid.astype(F32)).astype(BF16)
    return hi, mid, lo


def _dn_kernel(q_ref, k_ref, v_ref, z_ref, sl_ref, cw_ref, alog_ref, dtb_ref, ng_ref, s0_ref,
               o_ref, sfin_ref,
               xp_ref, qn_ref, kn_ref, vn_ref, col_ref, u_ref, wq_ref, kd_ref, in_ref, gl_ref, acc_ref,
               *, seq, nhs):
    c_len = DN_TILE
    n_chunks = seq // c_len

    zero8 = jnp.zeros((SUBLANES, LANES), F32)
    xp_ref[0:SUBLANES, :] = zero8
    xp_ref[SUBLANES + seq:2 * SUBLANES + seq, :] = zero8
    for hh in range(nhs):
        lanes = slice(hh * LANES, (hh + 1) * LANES)
        for j, (src, dst) in enumerate(((q_ref, qn_ref), (k_ref, kn_ref), (v_ref, vn_ref))):
            xp_ref[SUBLANES:SUBLANES + seq, :] = src[:, lanes]
            cw = cw_ref[j][:, lanes]
            y = cw[0:1] * xp_ref[pl.ds(SUBLANES - 2, seq), :]
            for t in range(1, CONV_W):
                y = y + cw[t:t + 1] * xp_ref[pl.ds(SUBLANES - 2 + t, seq), :]
            y = y * _sigmoid(y)
            if j < 2:
                y = y * lax.rsqrt(jnp.sum(y * y, axis=-1, keepdims=True) + EPS)
            if j == 0:
                y = y * (DN_DK ** -0.5)
            dst[hh] = y

    sl = sl_ref[...]
    lane = lax.broadcasted_iota(jnp.int32, (seq, LANES), 1)
    beta_all = _sigmoid(sl)
    xs = sl + dtb_ref[...]
    softplus = jnp.maximum(xs, 0.0) + jnp.log1p(jnp.exp(-jnp.abs(xs)))
    g_all = -jnp.exp(alog_ref[...]) * softplus
    for hh in range(nhs):
        head = pl.program_id(1) * nhs + hh
        cols = jnp.zeros((seq, LANES), F32)
        for dr in range(2):
            beta = jnp.sum(jnp.where(lane == dr * DN_HEADS + head, beta_all, 0.0), axis=-1, keepdims=True)
            g = jnp.sum(jnp.where(lane == 2 * DN_HEADS + dr * DN_HEADS + head, g_all, 0.0),
                        axis=-1, keepdims=True)
            cols = jnp.where(lane == dr, beta, cols)
            cols = jnp.where(lane == 2 + dr, g, cols)
        col_ref[hh] = cols
        acc_ref[hh] = jnp.zeros((seq, LANES), F32)

    ii = lax.broadcasted_iota(jnp.int32, (c_len, c_len), 0)
    jj = lax.broadcasted_iota(jnp.int32, (c_len, c_len), 1)
    eye = jnp.where(ii == jj, 1.0, 0.0).astype(F32)
    causal = (ii >= jj, ii <= jj)
    strict = (ii > jj, ii < jj)
    tri16 = tuple(jnp.where(m, 1.0, 0.0).astype(BF16) for m in causal)
    level_masks = []
    for dr in range(2):
        hi_idx, lo_idx = (ii, jj) if dr == 0 else (jj, ii)
        masks = []
        lvl = 0
        while (1 << lvl) < c_len:
            masks.append(((hi_idx >> (lvl + 1)) == (lo_idx >> (lvl + 1)))
                         & ((hi_idx >> lvl) == (lo_idx >> lvl) + 1))
            lvl += 1
        level_masks.append(masks)

    group = max(1, min(DN_CHAINS // (2 * nhs), n_chunks))
    units = [(hh, t) for hh in range(nhs) for t in range(group)]
    chains = [(m, dr) for m in range(len(units)) for dr in range(2)]
    lane_sq = jj

    def group_body(i, carry):
        cs = [i * group + t for hh, t in units]
        rows = [pl.ds(pl.multiple_of(c * c_len, c_len), c_len) for c in cs]
        q = [qn_ref[hh, rows[m], :] for m, (hh, t) in enumerate(units)]
        k = [kn_ref[hh, rows[m], :] for m, (hh, t) in enumerate(units)]
        v = [vn_ref[hh, rows[m], :] for m, (hh, t) in enumerate(units)]
        blk = [col_ref[hh, rows[m], :] for m, (hh, t) in enumerate(units)]
        q16 = [x.astype(BF16) for x in q]
        k16 = [x.astype(BF16) for x in k]
        qk = [_dot_nt(a, b) for a, b in zip(q16, k16)]
        kk = [_dot_nt(b, b) for b in k16]
        parts = [_split3(x) for x in blk]
        cum = [jnp.where(lane_sq == 2, sum(_dot(tri16[0], p) for p in ps), sum(_dot(tri16[1], p) for p in ps))
               for ps in parts]
        cum_t = [x.T for x in cum]
        gc_col = [cum[m][:, 2 + dr:3 + dr] for m, dr in chains]
        gc_row = [cum_t[m][2 + dr:3 + dr, :] for m, dr in chains]
        beta = [blk[m][:, dr:dr + 1] for m, dr in chains]
        decay = [jnp.where(causal[dr], jnp.exp(jnp.where(causal[dr], gc_col[n] - gc_row[n], 0.0)), 0.0)
                 for n, (m, dr) in enumerate(chains)]
        lm = [jnp.where(strict[dr], (beta[n] * kk[m]) * decay[n], 0.0) for n, (m, dr) in enumerate(chains)]
        lm16 = [x.astype(BF16) for x in lm]
        zero16 = jnp.zeros((c_len, c_len), BF16)
        xs = [eye - jnp.where(level_masks[dr][0], lm[n], 0.0) for n, (m, dr) in enumerate(chains)]
        for lvl in range(1, len(level_masks[0])):
            ys = [_dot(jnp.where(level_masks[dr][lvl], lm16[n], zero16), xs[n].astype(BF16))
                  for n, (m, dr) in enumerate(chains)]
            xs = [xs[n] - _dot(xs[n].astype(BF16), ys[n].astype(BF16)) for n in range(len(chains))]
        tmat = [x.astype(BF16) for x in xs]
        eg = [jnp.exp(x) for x in gc_col]
        g_last = [gc_col[n][c_len - 1:c_len] if dr == 0 else gc_col[n][0:1] for n, (m, dr) in enumerate(chains)]
        us = [_dot(tmat[n], (v[m] * beta[n]).astype(BF16)) for n, (m, dr) in enumerate(chains)]
        ws = [_dot(tmat[n], (k[m] * (beta[n] * eg[n])).astype(BF16)) for n, (m, dr) in enumerate(chains)]
        for n, (m, dr) in enumerate(chains):
            hh, c = units[m][0], cs[m]
            u_ref[dr, hh, rows[m], :] = us[n]
            wq_ref[dr, hh, pl.ds(pl.multiple_of(2 * c * c_len, c_len), c_len), :] = ws[n].astype(BF16)
            wq_ref[dr, hh, pl.ds(pl.multiple_of(2 * c * c_len + c_len, c_len), c_len), :] = (
                q[m] * eg[n]).astype(BF16)
            kd_ref[dr, hh, rows[m], :] = (k[m] * jnp.exp(g_last[n] - gc_col[n])).astype(BF16)
            in_ref[dr, hh, rows[m], :] = (qk[m] * decay[n]).astype(BF16)
            gl_ref[dr, hh, pl.ds(c, 1), :] = jnp.broadcast_to(jnp.exp(g_last[n]), (1, LANES))
        return carry

    lax.fori_loop(0, n_chunks // group, group_body, 0)

    seq_chains = [(dr, hh) for dr in range(2) for hh in range(nhs)]

    def step(i, states):
        cs = [i, n_chunks - 1 - i]
        rows = [pl.ds(pl.multiple_of(c * c_len, c_len), c_len) for c in cs]
        ws_qs = [_dot(wq_ref[dr, hh, pl.ds(pl.multiple_of(2 * cs[dr] * c_len, 2 * c_len), 2 * c_len), :],
                      states[n].astype(BF16)) for n, (dr, hh) in enumerate(seq_chains)]
        v16 = [(u_ref[dr, hh, rows[dr], :] - ws_qs[n][0:c_len]).astype(BF16)
               for n, (dr, hh) in enumerate(seq_chains)]
        intra = [_dot(in_ref[dr, hh, rows[dr], :], v16[n]) for n, (dr, hh) in enumerate(seq_chains)]
        upd = [_dot_tn(kd_ref[dr, hh, rows[dr], :], v16[n]) for n, (dr, hh) in enumerate(seq_chains)]
        for n, (dr, hh) in enumerate(seq_chains):
            acc_ref[hh, rows[dr], :] += ws_qs[n][c_len:2 * c_len] + intra[n]
        return tuple(states[n] * gl_ref[dr, hh, pl.ds(cs[dr], 1), :] + upd[n]
                     for n, (dr, hh) in enumerate(seq_chains))

    s_fin = lax.fori_loop(0, n_chunks, step, tuple(s0_ref[dr, hh] for dr, hh in seq_chains))
    for n, (dr, hh) in enumerate(seq_chains):
        sfin_ref[dr, hh] = s_fin[n]

    for hh in range(nhs):
        lanes = slice(hh * LANES, (hh + 1) * LANES)
        o = acc_ref[hh]
        y = o * lax.rsqrt(jnp.mean(o * o, axis=-1, keepdims=True) + EPS) * ng_ref[...]
        z = z_ref[:, lanes]
        o_ref[:, lanes] = y * (z * _sigmoid(z))


def _dn_call(proj, params, s0, s0_map, *, layer, batch, seq, nhs):
    nh = DN_HEADS
    nblk = nh // nhs
    width = nhs * LANES
    n_chunks = seq // DN_TILE
    vec = pl.BlockSpec((None, 1, LANES), lambda b, h: (layer, 0, 0))
    nsub = max(n_chunks, SUBLANES)
    return pl.pallas_call(
        functools.partial(_dn_kernel, seq=seq, nhs=nhs),
        grid=(batch, nblk),
        in_specs=[pl.BlockSpec((seq, width), lambda b, h: (b, COL_DQ * nblk + h)),
                  pl.BlockSpec((seq, width), lambda b, h: (b, (COL_DQ + 1) * nblk + h)),
                  pl.BlockSpec((seq, width), lambda b, h: (b, (COL_DQ + 2) * nblk + h)),
                  pl.BlockSpec((seq, width), lambda b, h: (b, COL_DZ * nblk + h)),
                  pl.BlockSpec((seq, LANES), lambda b, h: (b, COL_SC128)),
                  pl.BlockSpec((None, 3, CONV_W, width), lambda b, h: (layer, 0, 0, h)),
                  vec, vec, vec,
                  pl.BlockSpec((None, None, 2, nhs, DN_DK, DN_DK), s0_map)],
        out_specs=[pl.BlockSpec((seq, width), lambda b, h: (b, h)),
                   pl.BlockSpec((None, 2, nhs, DN_DK, DN_DK), lambda b, h: (b, 0, h, 0, 0))],
        out_shape=[jax.ShapeDtypeStruct((batch * seq, BRANCH_W), F32),
                   jax.ShapeDtypeStruct((batch, 2, nh, DN_DK, DN_DK), F32)],
        scratch_shapes=[pltpu.VMEM((seq + 2 * SUBLANES, LANES), F32),
                        pltpu.VMEM((nhs, seq, LANES), F32),
                        pltpu.VMEM((nhs, seq, LANES), F32),
                        pltpu.VMEM((nhs, seq, LANES), F32),
                        pltpu.VMEM((nhs, seq, LANES), F32),
                        pltpu.VMEM((2, nhs, seq, LANES), F32),
                        pltpu.VMEM((2, nhs, 2 * seq, LANES), BF16),
                        pltpu.VMEM((2, nhs, seq, LANES), BF16),
                        pltpu.VMEM((2, nhs, seq, DN_TILE), BF16),
                        pltpu.VMEM((2, nhs, nsub, LANES), F32),
                        pltpu.VMEM((nhs, seq, LANES), F32)],
        compiler_params=_params(2),
        name="deltanet",
    )(proj, proj, proj, proj, proj, params["dn_cw"], params["dn_alog"], params["dn_dtb"], params["dn_ng"], s0)


def _prepare_params(w_ffn_gate, w_ffn_up, w_ffn_down, w_in, lru_conv_w, lru_conv_b, lru_w_r, lru_b_r,
                    lru_w_i, lru_b_i, lru_lambda, gqa_q_norm, gqa_k_norm, na_rpb, dn_conv_w, dn_a_log,
                    dn_dt_bias, dn_norm_g, w_branch, w_out, norm_g):
    depth = w_in.shape[0]
    offs = np.cumsum((0,) + IN_WIDTHS)
    w_in_t = jnp.swapaxes(w_in, 1, 2).astype(BF16)
    seg = [w_in_t[:, offs[i]:offs[i + 1]] for i in range(len(IN_WIDTHS))]
    (a_x, a_y, b_q, b_k, b_v, c_q, c_k, c_v, d_q, d_k, d_v, d_z, d_b, d_a, g_lin) = seg
    perm = np.asarray(GQA_PERM)
    b_q = b_q.reshape(depth, GQA_HEADS, HEAD_DIM, D_MODEL)[:, perm].reshape(depth, BRANCH_W, D_MODEL)
    pad = jnp.zeros((depth, PROJ_W - 5120 - 2 * LANES - 16, D_MODEL), BF16)
    w_main = jnp.concatenate([a_x, a_y, b_q, c_q, c_k, c_v, d_q, d_k, d_v, d_z, b_k, b_v, d_b, d_a, pad], axis=1)

    def block_diag(wb):
        wb = wb.reshape(depth, 2, LRU_BLOCKS // 2, 2, LRU_BW, LRU_BW)
        z = jnp.zeros_like(wb[:, :, :, 0])
        return jnp.concatenate([jnp.concatenate([wb[:, :, :, 0], z], axis=-1),
                                jnp.concatenate([z, wb[:, :, :, 1]], axis=-1)], axis=-2)

    wr, wi = block_diag(lru_w_r), block_diag(lru_w_i)
    lru_wg = (0.5 * jnp.concatenate([wr[:, 0], wi[:, 0], wr[:, 1], wi[:, 1]], axis=-1)).astype(BF16)
    ncol = BRANCH_W // LANES

    def col_blocks(v):
        return v.reshape(depth, ncol, LANES)

    lru_bg = 0.5 * jnp.concatenate([col_blocks(lru_b_r[:, 0]), col_blocks(lru_b_i[:, 0]),
                                    col_blocks(lru_b_r[:, 1]), col_blocks(lru_b_i[:, 1])], axis=-1)[:, :, None, :]

    wb_b = w_branch[:, 1].reshape(depth, GQA_HEADS, HEAD_DIM, D_MODEL)[:, perm].reshape(depth, BRANCH_W, D_MODEL)
    wb = jnp.stack([w_branch[:, 0], wb_b, w_branch[:, 2], w_branch[:, 3]], axis=1).astype(BF16)

    lane_pad = jnp.zeros((depth, LANES - 4 * DN_HEADS), F32)
    lane_zero = jnp.zeros((depth, 2 * DN_HEADS), F32)
    alog = jnp.concatenate([lane_zero, dn_a_log.reshape(depth, -1), lane_pad], axis=1)[:, None, :]
    dtb = jnp.concatenate([lane_zero, dn_dt_bias.reshape(depth, -1), lane_pad], axis=1)[:, None, :]

    return dict(
        norm_g=norm_g[:, :, None, :],
        wg=w_ffn_gate.astype(BF16), wu=w_ffn_up.astype(BF16), wd=w_ffn_down.astype(BF16),
        w_main=w_main, w_gate=g_lin,
        lru_cw=lru_conv_w, lru_cb=lru_conv_b[:, None, :], lru_wg=lru_wg, lru_bg=lru_bg, lru_lam=lru_lambda,
        gq=jnp.tile(gqa_q_norm, (1, 2))[:, None, :], gk=jnp.tile(gqa_k_norm, (1, 2))[:, None, :],
        na_bias=_na_bias_table(na_rpb),
        dn_cw=dn_conv_w.reshape(depth, CONV_W, 3, BRANCH_W).transpose(0, 2, 1, 3),
        dn_alog=alog, dn_dtb=dtb, dn_ng=dn_norm_g[:, None, :],
        wb=wb, w_out=w_out.astype(BF16),
    )


def _rope_tables(seq):
    pos = jnp.arange(seq)
    half = HEAD_DIM // 2
    inv = jnp.power(ROPE_BASE, -jnp.arange(0, half, 2, dtype=F32) / half)
    ang_r = (pos // GRID_W).astype(F32)[:, None] * inv[None, :]
    ang_c = (pos % GRID_W).astype(F32)[:, None] * inv[None, :]
    cos = jnp.concatenate([jnp.cos(ang_r)] * 2 + [jnp.cos(ang_c)] * 2, axis=-1)
    sin = jnp.concatenate([-jnp.sin(ang_r), jnp.sin(ang_r), -jnp.sin(ang_c), jnp.sin(ang_c)], axis=-1)
    return jnp.tile(cos, (1, 2)), jnp.tile(sin, (1, 2))


def _na_bias_table(rpb):
    qc = np.arange(GRID_W)
    cs = np.clip(qc - NA_WIN_C // 2, 0, GRID_W - NA_WIN_C)
    kc = np.arange(GRID_W)
    inwin = (kc[None, :] >= cs[:, None]) & (kc[None, :] < cs[:, None] + NA_WIN_C)
    coff = kc[None, :] - qc[:, None] + NA_WIN_C - 1
    onehot = (coff[None] == np.arange(2 * NA_WIN_C - 1)[:, None, None]).astype(np.float32)
    t = jnp.einsum("lhrd,dqk->lhrqk", rpb.astype(F32), onehot, precision=lax.Precision.HIGHEST)
    t = jnp.where(inwin, t, NEG_BIG)
    edge = jnp.full_like(t[:, :, :1], NEG_BIG)
    t = jnp.concatenate([edge, t, edge], axis=2)
    return jnp.concatenate([t[:, :, :-1], t[:, :, 1:]], axis=-1)


def _layer(x, mod_all, params, *, batch, seq, latent, layer, caches, tables, final_g, tm):
    row0 = 1 if latent else 0
    x = _ffn_call(x, mod_all, params, final_g, layer=layer, which=0, row0=row0, final=False, tm=tm)
    proj = _inproj_call(x, mod_all, params, layer=layer, row0=row0, tm=tm)

    dn_heads_per_step = 1 if seq > 1024 else DN_HEADS
    if latent:
        cache_ak, cache_av, cache_nk, cache_nv, state_lru, state_delta = caches
        h0, h0_map = state_lru, lambda b, c: (b, layer, 0, c)
        s0, s0_map = state_delta, lambda b, h: (b, layer, 0, h, 0, 0)
    else:
        h0, h0_map = jnp.zeros((1, 1, 2, BRANCH_W), F32), lambda b, c: (0, 0, 0, c)
        s0 = jnp.zeros((1, 1, 2, dn_heads_per_step, DN_DK, DN_DK), F32)
        s0_map = lambda b, h: (0, 0, 0, 0, 0, 0)

    o_a, lru_fin = _lru_call(proj, h0, h0_map, params, layer=layer, batch=batch, seq=seq,
                             ncb=1 if seq > 1024 else BRANCH_W // LANES)

    cos, sin = tables["rope"] if latent else (None, None)
    qn, kn = _prep_call(proj, params, cos, sin, layer=layer, batch=batch, seq=seq, rope=latent)
    tq = min(seq, ATTN_QROWS)
    kv_new = (kn, pl.BlockSpec((seq, LANES), lambda b, i: (b, 0)),
              proj, pl.BlockSpec((seq, LANES), lambda b, i: (b, COL_BV128)))
    if latent:
        gqa_cache = pl.BlockSpec((None, None, PAST_LEN, LANES), lambda b, i: (b, layer, 0, 0))
        sources = [kv_new, (cache_ak, gqa_cache, cache_av, gqa_cache)]
    else:
        sources = [kv_new]
    o_b = _attn_call(qn, 0, sources, batch=batch, seq=seq, tq=tq, nkb=1, qscale=1.0)

    if latent:
        o_c = _na_call(proj, cache_nk, cache_nv, params, layer=layer, batch=batch, seq=seq)
    else:
        src = (proj, pl.BlockSpec((seq, BRANCH_W), lambda b, i: (b, COL_CK)),
               proj, pl.BlockSpec((seq, BRANCH_W), lambda b, i: (b, COL_CV)))
        o_c = _attn_call(proj, COL_CQ, [src], batch=batch, seq=seq, tq=tq, nkb=4, qscale=HEAD_DIM ** -0.5)

    o_d, dn_fin = _dn_call(proj, params, s0, s0_map, layer=layer, batch=batch, seq=seq, nhs=dn_heads_per_step)

    x = _merge_call(x, mod_all, params, (o_a, o_b, o_c, o_d), layer=layer, row0=row0, tm=tm)
    x = _ffn_call(x, mod_all, params, final_g, layer=layer, which=1, row0=row0,
                  final=(layer == DEPTH - 1), tm=tm)

    new_ctx = None
    if not latent:
        new_ctx = (kn.reshape(batch, seq, GQA_KV, HEAD_DIM),
                   proj[:, COL_BV128 * LANES:(COL_BV128 + 1) * LANES].reshape(batch, seq, GQA_KV, HEAD_DIM),
                   proj[:, COL_CK * BRANCH_W:(COL_CK + 1) * BRANCH_W].reshape(batch, seq, NA_HEADS, HEAD_DIM),
                   proj[:, COL_CV * BRANCH_W:(COL_CV + 1) * BRANCH_W].reshape(batch, seq, NA_HEADS, HEAD_DIM),
                   lru_fin, dn_fin)
    return x, new_ctx


def kernel(x_prompt, x_sample, c, cache_attn_k, cache_attn_v, cache_na_k, cache_na_v, state_lru, state_delta, c_ctx, w_mod, b_mod, norm_g, w_ffn_gate, w_ffn_up, w_ffn_down, w_in, lru_conv_w, lru_conv_b, lru_w_r, lru_b_r, lru_w_i, lru_b_i, lru_lambda, gqa_q_norm, gqa_k_norm, na_rpb, dn_conv_w, dn_a_log, dn_dt_bias, dn_norm_g, w_branch, w_out, final_norm_g):
    batch_c, seq_c, _ = x_prompt.shape
    batch_l, seq_l, _ = x_sample.shape
    assert batch_l + 1 <= SUBLANES

    cs = jnp.concatenate([c_ctx[None, :], c, jnp.zeros((SUBLANES - 1 - batch_l, D_MODEL), F32)], axis=0)
    mod_all = _mod_call(cs, w_mod, b_mod).reshape(DEPTH, SUBLANES, N_MOD, D_MODEL)
    params = _prepare_params(w_ffn_gate, w_ffn_up, w_ffn_down, w_in, lru_conv_w, lru_conv_b, lru_w_r, lru_b_r,
                             lru_w_i, lru_b_i, lru_lambda, gqa_q_norm, gqa_k_norm, na_rpb, dn_conv_w, dn_a_log,
                             dn_dt_bias, dn_norm_g, w_branch, w_out, norm_g)

    caches = (cache_attn_k.reshape(batch_l, DEPTH, PAST_LEN, GQA_KV * HEAD_DIM),
              cache_attn_v.reshape(batch_l, DEPTH, PAST_LEN, GQA_KV * HEAD_DIM),
              cache_na_k.reshape(batch_l, DEPTH, PAST_LEN, BRANCH_W),
              cache_na_v.reshape(batch_l, DEPTH, PAST_LEN, BRANCH_W),
              state_lru, state_delta)
    tables = {"rope": _rope_tables(seq_l)}
    final_g = final_norm_g[None, :]

    xc = x_prompt.reshape(1, batch_c * seq_c, D_MODEL)
    xl = x_sample
    ctx_out = []
    for l in range(DEPTH):
        xc, new_ctx = _layer(xc, mod_all, params, batch=batch_c, seq=seq_c, latent=False, layer=l, caches=None,
                             tables=tables, final_g=final_g, tm=DENSE_ROWS)
        ctx_out.append(new_ctx)
        xl, _ = _layer(xl, mod_all, params, batch=batch_l, seq=seq_l, latent=True, layer=l, caches=caches,
                       tables=tables, final_g=final_g, tm=DENSE_ROWS)

    stacked = [jnp.stack([ctx_out[l][i] for l in range(DEPTH)], axis=1) for i in range(6)]
    return (xc.reshape(batch_c, seq_c, D_MODEL), xl, *stacked)
```

```python
import functools

import numpy as np
import jax
import jax.numpy as jnp
from jax import lax
from jax.experimental import pallas as pl
from jax.experimental.pallas import tpu as pltpu

F32 = jnp.float32
BF16 = jnp.bfloat16

D_MODEL = 1024
DEPTH = 4
GRID_W = 64
BRANCH_W = 512
N_MOD = 9
D_FF = 2816
EPS = 1e-6
CONV_W = 4
LRU_BLOCKS = 8
LRU_BW = 64
LRU_C = 8.0
HEAD_DIM = 64
GQA_HEADS = 8
GQA_KV = 2
ROPE_BASE = 10000.0
NA_HEADS = 8
NA_WIN_R = 8
NA_WIN_C = 16
NA_QROWS = 4
NA_KROWS = 12
DN_DK = 128
DN_HEADS = 4
LRU_PITCH_PAD = 8
DN_TILE = 128
DN_CHAINS = 16
PAST_LEN = 512
IN_WIDTHS = (512, 512, 512, 128, 128, 512, 512, 512, 512, 512, 512, 512, 8, 8, 4096)

LANES = 128
SUBLANES = 8
DENSE_ROWS = 512
DENSE_PARTS = 2
ATTN_QROWS = 512
PREP_ROWS = 512
PROJ_W = 5632
COL_BQ, COL_CQ, COL_CK, COL_CV, COL_DQ, COL_DZ = 2, 3, 4, 5, 6, 9
COL_BK128, COL_BV128, COL_SC128 = 40, 41, 42
VMEM_LIMIT = 56 * 1024 * 1024
NEG_BIG = -1e30
GQA_PERM = (0, 4, 1, 5, 2, 6, 3, 7)


def _params(n):
    return pltpu.CompilerParams(dimension_semantics=("arbitrary",) * n, vmem_limit_bytes=VMEM_LIMIT)


def _const_spec(shape):
    nd = len(shape)
    return pl.BlockSpec(shape, lambda *_: (0,) * nd, pipeline_mode=pl.Buffered(1))


def _layer_spec(arr, layer, *sub):
    tail = arr.shape[1 + len(sub):]
    index = (layer,) + tuple(sub) + (0,) * len(tail)
    return pl.BlockSpec((None,) * (1 + len(sub)) + tuple(tail), lambda *_: index, pipeline_mode=pl.Buffered(1))


def _mod_spec(layer, row0):
    return pl.BlockSpec((None, None, N_MOD, D_MODEL), lambda b, i: (layer, row0 + b, 0, 0))


def _dot(a, b):
    return jnp.dot(a, b, preferred_element_type=F32)


def _dot_nt(a, b):
    return lax.dot_general(a, b, (((1,), (1,)), ((), ())), preferred_element_type=F32)


def _dot_tn(a, b):
    return lax.dot_general(a, b, (((0,), (0,)), ((), ())), preferred_element_type=F32)


def _split(x):
    hi = x.astype(BF16)
    lo = (x - hi.astype(F32)).astype(BF16)
    return hi, lo


def _dot3(a, b):
    ah, al = _split(a)
    bh, bl = _split(b)
    return _dot(ah, bh) + (_dot(al, bh) + _dot(ah, bl))


def _sigmoid(x):
    return 0.5 * jnp.tanh(0.5 * x) + 0.5


def _modnorm(x, g, shift, scale):
    ms = jnp.mean(x * x, axis=-1, keepdims=True)
    return (x * lax.rsqrt(ms + EPS) * g) * (1.0 + scale) + shift


def _mod_kernel(c_ref, w_ref, b_ref, o_ref):
    c = c_ref[...]
    o_ref[0] = _dot3(c * _sigmoid(c), w_ref[0]) + b_ref[0]


def _mod_call(cs, w_mod, b_mod):
    tn = 1024
    n = N_MOD * D_MODEL
    return pl.pallas_call(
        _mod_kernel,
        grid=(DEPTH, n // tn),
        in_specs=[pl.BlockSpec((SUBLANES, D_MODEL), lambda l, j: (0, 0)),
                  pl.BlockSpec((1, D_MODEL, tn), lambda l, j: (l, 0, j)),
                  pl.BlockSpec((1, 1, tn), lambda l, j: (l, 0, j))],
        out_specs=pl.BlockSpec((1, SUBLANES, tn), lambda l, j: (l, 0, j)),
        out_shape=jax.ShapeDtypeStruct((DEPTH, SUBLANES, n), F32),
        compiler_params=_params(2),
        name="mod",
    )(cs, w_mod, b_mod.reshape(DEPTH, 1, n))


def _ffn_kernel(x_ref, mod_ref, g_ref, wg_ref, wu_ref, wd_ref, gf_ref, o_ref, *, mi, final):
    mod = mod_ref[...]
    rows = x_ref.shape[1]
    parts = [pl.ds(n * (rows // DENSE_PARTS), rows // DENSE_PARTS) for n in range(DENSE_PARTS)]
    xs = [x_ref[0, p, :] for p in parts]
    hs = [_modnorm(x, g_ref[...], mod[mi:mi + 1], mod[mi + 1:mi + 2]).astype(BF16) for x in xs]
    acts = []
    for h in hs:
        gt = _dot(h, wg_ref[...])
        up = _dot(h, wu_ref[...])
        acts.append((gt * _sigmoid(gt) * up).astype(BF16))
    downs = [_dot(a, wd_ref[...]) for a in acts]
    for p, x, d in zip(parts, xs, downs):
        y = x + 0.5 * mod[mi + 2:mi + 3] * d
        if final:
            ms = jnp.mean(y * y, axis=-1, keepdims=True)
            y = y * lax.rsqrt(ms + EPS) * gf_ref[...]
        o_ref[0, p, :] = y


def _ffn_call(x, mod_all, params, gf, *, layer, which, row0, final, tm):
    nb, rows, _ = x.shape
    return pl.pallas_call(
        functools.partial(_ffn_kernel, mi=6 * which, final=final),
        grid=(nb, rows // tm),
        in_specs=[pl.BlockSpec((1, tm, D_MODEL), lambda b, i: (b, i, 0)),
                  _mod_spec(layer, row0),
                  _layer_spec(params["norm_g"], layer, 2 * which),
                  _layer_spec(params["wg"], layer, which),
                  _layer_spec(params["wu"], layer, which),
                  _layer_spec(params["wd"], layer, which),
                  _const_spec((1, D_MODEL))],
        out_specs=pl.BlockSpec((1, tm, D_MODEL), lambda b, i: (b, i, 0)),
        out_shape=jax.ShapeDtypeStruct(x.shape, F32),
        compiler_params=_params(2),
        name="ffn",
    )(x, mod_all, params["norm_g"], params["wg"], params["wu"], params["wd"], gf)


def _inproj_kernel(x_ref, mod_ref, g_ref, w_ref, o_ref):
    mod = mod_ref[...]
    h = _modnorm(x_ref[0], g_ref[...], mod[3:4], mod[4:5]).astype(BF16)
    o_ref[...] = _dot_nt(h, w_ref[...])


def _inproj_call(x, mod_all, params, *, layer, row0, tm):
    nb, rows, _ = x.shape
    nt = rows // tm
    return pl.pallas_call(
        _inproj_kernel,
        grid=(nb, nt),
        in_specs=[pl.BlockSpec((1, tm, D_MODEL), lambda b, i: (b, i, 0)),
                  _mod_spec(layer, row0),
                  _layer_spec(params["norm_g"], layer, 1),
                  _layer_spec(params["w_main"], layer)],
        out_specs=pl.BlockSpec((tm, PROJ_W), lambda b, i: (b * nt + i, 0)),
        out_shape=jax.ShapeDtypeStruct((nb * rows, PROJ_W), F32),
        compiler_params=_params(2),
        name="inproj",
    )(x, mod_all, params["norm_g"], params["w_main"])


def _merge_kernel(x_ref, mod_ref, g_ref, oa_ref, ob_ref, oc_ref, od_ref, wgate_ref, wb_ref, wout_ref, o_ref):
    x = x_ref[0]
    mod = mod_ref[...]
    h = _modnorm(x, g_ref[...], mod[3:4], mod[4:5]).astype(BF16)
    acc = None
    for n, ref in enumerate((oa_ref, ob_ref, oc_ref, od_ref)):
        gate = _sigmoid(_dot_nt(h, wgate_ref[n * D_MODEL:(n + 1) * D_MODEL, :]))
        term = gate * _dot(ref[...].astype(BF16), wb_ref[n])
        acc = term if acc is None else acc + term
    o_ref[0] = x + mod[5:6] * _dot(acc.astype(BF16), wout_ref[...])


def _merge_call(x, mod_all, params, outs, *, layer, row0, tm):
    nb, rows, _ = x.shape
    nt = rows // tm
    ospec = pl.BlockSpec((tm, BRANCH_W), lambda b, i: (b * nt + i, 0))
    return pl.pallas_call(
        _merge_kernel,
        grid=(nb, nt),
        in_specs=[pl.BlockSpec((1, tm, D_MODEL), lambda b, i: (b, i, 0)),
                  _mod_spec(layer, row0),
                  _layer_spec(params["norm_g"], layer, 1),
                  ospec, ospec, ospec, ospec,
                  _layer_spec(params["w_gate"], layer),
                  _layer_spec(params["wb"], layer),
                  _layer_spec(params["w_out"], layer)],
        out_specs=pl.BlockSpec((1, tm, D_MODEL), lambda b, i: (b, i, 0)),
        out_shape=jax.ShapeDtypeStruct(x.shape, F32),
        compiler_params=_params(2),
        name="merge",
    )(x, mod_all, params["norm_g"], *outs, params["w_gate"], params["wb"], params["w_out"])


def _log_sigmoid(x):
    return jnp.minimum(x, 0.0) - jnp.log1p(jnp.exp(-jnp.abs(x)))


def _gelu_tanh(x):
    return x * (0.5 * (1.0 + jnp.tanh(0.7978845608028654 * (x + 0.044715 * (x * x * x)))))


def _lru_kernel(ax_ref, ay_ref, h0_ref, cw_ref, cb_ref, wg_ref, bg_ref, lam_ref, o_ref, fin_ref,
                xp_ref, a_ref, u_ref, h_ref, p_ref, *, seq):
    lc = seq // SUBLANES
    pitch = lc + LRU_PITCH_PAD
    zero8 = jnp.zeros((SUBLANES, LANES), F32)
    xp_ref[0:SUBLANES, :] = zero8
    xp_ref[SUBLANES + seq:2 * SUBLANES + seq, :] = zero8
    xp_ref[SUBLANES:SUBLANES + seq, :] = ax_ref[...]
    cw = cw_ref[...]
    xa = cb_ref[...] + cw[0:1] * xp_ref[pl.ds(SUBLANES - 2, seq), :]
    for k in range(1, CONV_W):
        xa = xa + cw[k:k + 1] * xp_ref[pl.ds(SUBLANES - 2 + k, seq), :]
    half = _dot(xa.astype(BF16), wg_ref[...]) + bg_ref[...]
    lam = lam_ref[...]
    xh = 0.5 * xa
    for dr in range(2):
        tr = jnp.tanh(half[:, (2 * dr) * LANES:(2 * dr + 1) * LANES])
        ti = jnp.tanh(half[:, (2 * dr + 1) * LANES:(2 * dr + 2) * LANES])
        c_half = (0.5 * LRU_C) * _log_sigmoid(lam[dr:dr + 1])
        log_a = tr * c_half + c_half
        a = jnp.exp(log_a)
        one_m_a2 = -jnp.tanh(log_a) * (a * a + 1.0)
        root = jnp.where(one_m_a2 > 0.0, one_m_a2 * lax.rsqrt(one_m_a2), 0.0)
        u = root * ((ti + 1.0) * xh)
        for k in range(SUBLANES):
            a_ref[dr, pl.ds(k * pitch, lc), :] = a[k * lc:(k + 1) * lc]
            u_ref[dr, pl.ds(k * pitch, lc), :] = u[k * lc:(k + 1) * lc]

    sub = lax.broadcasted_iota(jnp.int32, (SUBLANES, LANES), 0)
    h0 = h0_ref[...]
    hf0 = jnp.where(sub == 0, h0[0:1], 0.0)
    hb0 = jnp.where(sub == SUBLANES - 1, h0[1:2], 0.0)
    ones = jnp.ones((SUBLANES, LANES), F32)

    def body(s, carry):
        hf, pf, hb, pb = carry
        rows_f = pl.ds(s, SUBLANES, stride=pitch)
        rows_b = pl.ds(lc - 1 - s, SUBLANES, stride=pitch)
        af = a_ref[0, rows_f, :]
        hf = af * hf + u_ref[0, rows_f, :]
        pf = af * pf
        h_ref[0, rows_f, :] = hf
        p_ref[0, rows_f, :] = pf
        ab = a_ref[1, rows_b, :]
        hb = ab * hb + u_ref[1, rows_b, :]
        pb = ab * pb
        h_ref[1, rows_b, :] = hb
        p_ref[1, rows_b, :] = pb
        return hf, pf, hb, pb

    hf, pf, hb, pb = lax.fori_loop(0, lc, body, (hf0, ones, hb0, ones), unroll=8)

    cf = [jnp.zeros((1, LANES), F32)]
    for k in range(1, SUBLANES):
        cf.append(hf[k - 1:k] + pf[k - 1:k] * cf[k - 1])
    fin_ref[0:1, :] = hf[SUBLANES - 1:SUBLANES] + pf[SUBLANES - 1:SUBLANES] * cf[SUBLANES - 1]
    cb = [None] * SUBLANES
    cb[SUBLANES - 1] = jnp.zeros((1, LANES), F32)
    for k in range(SUBLANES - 2, -1, -1):
        cb[k] = hb[k + 1:k + 2] + pb[k + 1:k + 2] * cb[k + 1]
    fin_ref[1:2, :] = hb[0:1] + pb[0:1] * cb[0]

    for k in range(SUBLANES):
        rows = pl.ds(k * lc, lc)
        held = pl.ds(k * pitch, lc)
        h = (h_ref[0, held, :] + p_ref[0, held, :] * cf[k]) + (h_ref[1, held, :] + p_ref[1, held, :] * cb[k])
        o_ref[rows, :] = h * _gelu_tanh(ay_ref[rows, :])


def _lru_columns_kernel(ax_ref, ay_ref, h0_ref, cw_ref, cb_ref, wg_ref, bg_ref, lam_ref, o_ref, fin_ref,
                        *scratch, seq, ncb):
    for j in range(ncb):
        lanes = pl.ds(j * LANES, LANES)
        _lru_kernel(ax_ref.at[:, lanes], ay_ref.at[:, lanes], h0_ref.at[:, lanes], cw_ref.at[:, lanes],
                    cb_ref.at[:, lanes], wg_ref.at[j], bg_ref.at[j], lam_ref.at[:, lanes],
                    o_ref.at[:, lanes], fin_ref.at[:, lanes], *scratch, seq=seq)


def _lru_call(proj, h0, h0_map, params, *, layer, batch, seq, ncb):
    ncol = BRANCH_W // (LANES * ncb)
    width = LANES * ncb
    pad_rows = SUBLANES * LRU_PITCH_PAD
    return pl.pallas_call(
        functools.partial(_lru_columns_kernel, seq=seq, ncb=ncb),
        grid=(batch, ncol),
        in_specs=[pl.BlockSpec((seq, width), lambda b, c: (b, c)),
                  pl.BlockSpec((seq, width), lambda b, c: (b, ncol + c)),
                  pl.BlockSpec((None, None, 2, width), h0_map),
                  pl.BlockSpec((None, CONV_W, width), lambda b, c: (layer, 0, c)),
                  pl.BlockSpec((None, 1, width), lambda b, c: (layer, 0, c)),
                  pl.BlockSpec((None, ncb, LANES, 4 * LANES), lambda b, c: (layer, c, 0, 0)),
                  pl.BlockSpec((None, ncb, 1, 4 * LANES), lambda b, c: (layer, c, 0, 0)),
                  pl.BlockSpec((None, 2, width), lambda b, c: (layer, 0, c))],
        out_specs=[pl.BlockSpec((seq, width), lambda b, c: (b, c)),
                   pl.BlockSpec((None, 2, width), lambda b, c: (b, 0, c))],
        out_shape=[jax.ShapeDtypeStruct((batch * seq, BRANCH_W), F32),
                   jax.ShapeDtypeStruct((batch, 2, BRANCH_W), F32)],
        scratch_shapes=[pltpu.VMEM((seq + 2 * SUBLANES, LANES), F32),
                        pltpu.VMEM((2, seq + pad_rows, LANES), F32),
                        pltpu.VMEM((2, seq + pad_rows, LANES), F32),
                        pltpu.VMEM((2, seq + pad_rows, LANES), F32),
                        pltpu.VMEM((2, seq + pad_rows, LANES), F32)],
        compiler_params=_params(2),
        name="lru",
    )(proj, proj, h0, params["lru_cw"], params["lru_cb"], params["lru_wg"], params["lru_bg"], params["lru_lam"])


def _prep_kernel(*refs, rope):
    if rope:
        q_ref, k_ref, gq_ref, gk_ref, cos_ref, sin_ref, qn_ref, kn_ref = refs
    else:
        q_ref, k_ref, gq_ref, gk_ref, qn_ref, kn_ref = refs
    rows = q_ref.shape[0]
    lane = lax.broadcasted_iota(jnp.int32, (rows, LANES), 1)
    lo = lane < HEAD_DIM
    first16 = (lane & 16) == 0

    def head_norm(x, g):
        sq = x * x
        s_lo = jnp.sum(jnp.where(lo, sq, 0.0), axis=-1, keepdims=True)
        s_hi = jnp.sum(jnp.where(lo, 0.0, sq), axis=-1, keepdims=True)
        ms = jnp.where(lo, s_lo, s_hi) * (1.0 / HEAD_DIM)
        return x * lax.rsqrt(ms + EPS) * g

    def rotate(y):
        if not rope:
            return y
        partner = jnp.where(first16, pltpu.roll(y, LANES - 16, 1), pltpu.roll(y, 16, 1))
        return y * cos_ref[...] + partner * sin_ref[...]

    for p in range(BRANCH_W // LANES):
        cols = slice(p * LANES, (p + 1) * LANES)
        qn_ref[:, cols] = rotate(head_norm(q_ref[:, cols], gq_ref[...])) * (HEAD_DIM ** -0.5)
    kn_ref[...] = rotate(head_norm(k_ref[...], gk_ref[...]))


def _prep_call(proj, params, cos, sin, *, layer, batch, seq, rope):
    rb = min(seq, PREP_ROWS)
    nr = seq // rb
    tab = pl.BlockSpec((rb, LANES), lambda b, i: (i, 0))
    in_specs = [pl.BlockSpec((rb, BRANCH_W), lambda b, i: (b * nr + i, COL_BQ)),
                pl.BlockSpec((rb, LANES), lambda b, i: (b * nr + i, COL_BK128)),
                _layer_spec(params["gq"], layer), _layer_spec(params["gk"], layer)]
    args = [proj, proj, params["gq"], params["gk"]]
    if rope:
        in_specs += [tab, tab]
        args += [cos, sin]
    return pl.pallas_call(
        functools.partial(_prep_kernel, rope=rope),
        grid=(batch, nr),
        in_specs=in_specs,
        out_specs=[pl.BlockSpec((rb, BRANCH_W), lambda b, i: (b * nr + i, 0)),
                   pl.BlockSpec((rb, LANES), lambda b, i: (b * nr + i, 0))],
        out_shape=[jax.ShapeDtypeStruct((batch * seq, BRANCH_W), F32),
                   jax.ShapeDtypeStruct((batch * seq, LANES), F32)],
        compiler_params=_params(2),
        name="attn_prep",
    )(*args)


def _softmax_pv(scores, values):
    m = None
    for s in scores:
        sm = jnp.max(s, axis=-1, keepdims=True)
        m = sm if m is None else jnp.maximum(m, sm)
    den = None
    out = None
    for s, v in zip(scores, values):
        p = jnp.exp(s - m)
        ps = jnp.sum(p, axis=-1, keepdims=True)
        den = ps if den is None else den + ps
        o = _dot(p.astype(BF16), v)
        out = o if out is None else out + o
    return out / den


def _attn_kernel(*refs, nsrc, nkb, qscale):
    q_ref = refs[0]
    src = refs[1:1 + 2 * nsrc]
    o_ref = refs[1 + 2 * nsrc]
    tq = q_ref.shape[0]
    lo = lax.broadcasted_iota(jnp.int32, (tq, LANES), 1) < HEAD_DIM

    loaded = {}

    def kv(p):
        col = p if nkb != 1 else 0
        if col not in loaded:
            cols = slice(col * LANES, (col + 1) * LANES)
            loaded[col] = ([src[2 * i][:, cols].astype(BF16) for i in range(nsrc)],
                           [src[2 * i + 1][:, cols].astype(BF16) for i in range(nsrc)])
        return loaded[col]

    def scores(unit):
        p, half = unit
        qb = q_ref[:, p * LANES:(p + 1) * LANES]
        if qscale != 1.0:
            qb = qb * qscale
        qm = jnp.where(lo if half == 0 else jnp.logical_not(lo), qb, 0.0).astype(BF16)
        return [_dot_nt(qm, k) for k in kv(p)[0]]

    units = [(p, half) for p in range(BRANCH_W // LANES) for half in range(2)]
    pending = scores(units[0])
    first_half = None
    for n, (p, half) in enumerate(units):
        current = pending
        if n + 1 < len(units):
            pending = scores(units[n + 1])
        out = _softmax_pv(current, kv(p)[1])
        if half == 0:
            first_half = out
        else:
            o_ref[:, p * LANES:(p + 1) * LANES] = jnp.where(lo, first_half, out)


def _attn_call(q_arr, q_col, sources, *, batch, seq, tq, nkb, qscale):
    nq = seq // tq
    in_specs = [pl.BlockSpec((tq, BRANCH_W), lambda b, i: (b * nq + i, q_col))]
    args = [q_arr]
    for k_arr, k_spec, v_arr, v_spec in sources:
        in_specs += [k_spec, v_spec]
        args += [k_arr, v_arr]
    return pl.pallas_call(
        functools.partial(_attn_kernel, nsrc=len(sources), nkb=nkb, qscale=qscale),
        grid=(batch, nq),
        in_specs=in_specs,
        out_specs=pl.BlockSpec((tq, BRANCH_W), lambda b, i: (b * nq + i, 0)),
        out_shape=jax.ShapeDtypeStruct((batch * seq, BRANCH_W), F32),
        compiler_params=_params(2),
        name="attn",
    )(*args)


def _na_row_start(r, rows):
    return jnp.clip(r - NA_WIN_R // 2, 0, rows - NA_WIN_R)


def _na_kernel(q_ref, k_ref, v_ref, ck_ref, cv_ref, bias_ref, o_ref, *, rows):
    r0 = pl.program_id(1) * NA_QROWS
    w0 = jnp.clip(r0 - NA_WIN_R // 2, 0, rows - NA_KROWS)
    win = pl.ds(pl.multiple_of(w0 * GRID_W, GRID_W), NA_KROWS * GRID_W)
    lo = lax.broadcasted_iota(jnp.int32, (NA_QROWS * GRID_W, LANES), 1) < HEAD_DIM
    lo_row = lax.broadcasted_iota(jnp.int32, (GRID_W, LANES), 1) < HEAD_DIM
    ncol = BRANCH_W // LANES
    scores = []
    for p in range(ncol):
        cols = slice(p * LANES, (p + 1) * LANES)
        qb = q_ref[:, cols] * (HEAD_DIM ** -0.5)
        kw = k_ref[win, cols].astype(BF16)
        kc = ck_ref[:, cols].astype(BF16)
        for half in range(2):
            qm = jnp.where(lo if half == 0 else jnp.logical_not(lo), qb, 0.0).astype(BF16)
            scores.append((_dot_nt(qm, kw), _dot_nt(qm, kc)))
    pair_index, pair_mask = [], []
    for j in range(NA_QROWS):
        r = r0 + j
        rs = _na_row_start(r, rows)
        idx_j, mask_j = [], []
        for t in range(NA_KROWS // 2):
            kr = w0 + 2 * t
            rel = kr - r + NA_WIN_R - 1
            idx_j.append(jnp.clip(rel + 1, 0, 2 * NA_WIN_R - 1))
            in_a = jnp.where((kr >= rs) & (kr < rs + NA_WIN_R), 0.0, NEG_BIG)
            in_b = jnp.where((kr + 1 >= rs) & (kr + 1 < rs + NA_WIN_R), 0.0, NEG_BIG)
            mask_j.append(jnp.where(lo_row, in_a, in_b))
        pair_index.append(idx_j)
        pair_mask.append(mask_j)
    probs = []
    for h, (s_loc, s_ctx) in enumerate(scores):
        bias = jnp.concatenate(
            [jnp.concatenate([bias_ref[h, pair_index[j][t]] + pair_mask[j][t] for t in range(NA_KROWS // 2)],
                             axis=-1) for j in range(NA_QROWS)], axis=0)
        s_loc = s_loc + bias
        m = jnp.maximum(jnp.max(s_loc, axis=-1, keepdims=True), jnp.max(s_ctx, axis=-1, keepdims=True))
        p_loc = jnp.exp(s_loc - m)
        p_ctx = jnp.exp(s_ctx - m)
        den = jnp.sum(p_loc, axis=-1, keepdims=True) + jnp.sum(p_ctx, axis=-1, keepdims=True)
        probs.append((p_loc.astype(BF16), p_ctx.astype(BF16), den))
    for p in range(ncol):
        cols = slice(p * LANES, (p + 1) * LANES)
        vw = v_ref[win, cols].astype(BF16)
        vc = cv_ref[:, cols].astype(BF16)
        halves = []
        for half in range(2):
            p_loc, p_ctx, den = probs[2 * p + half]
            halves.append((_dot(p_loc, vw) + _dot(p_ctx, vc)) / den)
        o_ref[:, cols] = jnp.where(lo, halves[0], halves[1])


def _na_call(proj, cache_k, cache_v, params, *, layer, batch, seq):
    rows = seq // GRID_W
    steps = rows // NA_QROWS
    qrows = NA_QROWS * GRID_W
    cache_spec = pl.BlockSpec((None, None, PAST_LEN, BRANCH_W), lambda b, r: (b, layer, 0, 0))
    return pl.pallas_call(
        functools.partial(_na_kernel, rows=rows),
        grid=(batch, steps),
        in_specs=[pl.BlockSpec((qrows, BRANCH_W), lambda b, r: (b * steps + r, COL_CQ)),
                  pl.BlockSpec((seq, BRANCH_W), lambda b, r: (b, COL_CK)),
                  pl.BlockSpec((seq, BRANCH_W), lambda b, r: (b, COL_CV)),
                  cache_spec, cache_spec,
                  _layer_spec(params["na_bias"], layer)],
        out_specs=pl.BlockSpec((qrows, BRANCH_W), lambda b, r: (b * steps + r, 0)),
        out_shape=jax.ShapeDtypeStruct((batch * seq, BRANCH_W), F32),
        compiler_params=_params(2),
        name="na",
    )(proj, proj, proj, cache_k, cache_v, params["na_bias"])


def _split3(x):
    hi = x.astype(BF16)
    r = x - hi.astype(F32)
    mid = r.astype(BF16)
    lo = (r - mid.astype(F32)).astype(BF16)
    return hi, mid, lo


def _dn_kernel(q_ref, k_ref, v_ref, z_ref, sl_ref, cw_ref, alog_ref, dtb_ref, ng_ref, s0_ref,
               o_ref, sfin_ref,
               xp_ref, qn_ref, kn_ref, vn_ref, col_ref, u_ref, wq_ref, kd_ref, in_ref, gl_ref, acc_ref,
               *, seq, nhs):
    c_len = DN_TILE
    n_chunks = seq // c_len

    zero8 = jnp.zeros((SUBLANES, LANES), F32)
    xp_ref[0:SUBLANES, :] = zero8
    xp_ref[SUBLANES + seq:2 * SUBLANES + seq, :] = zero8
    for hh in range(nhs):
        lanes = slice(hh * LANES, (hh + 1) * LANES)
        for j, (src, dst) in enumerate(((q_ref, qn_ref), (k_ref, kn_ref), (v_ref, vn_ref))):
            xp_ref[SUBLANES:SUBLANES + seq, :] = src[:, lanes]
            cw = cw_ref[j][:, lanes]
            y = cw[0:1] * xp_ref[pl.ds(SUBLANES - 2, seq), :]
            for t in range(1, CONV_W):
                y = y + cw[t:t + 1] * xp_ref[pl.ds(SUBLANES - 2 + t, seq), :]
            y = y * _sigmoid(y)
            if j < 2:
                y = y * lax.rsqrt(jnp.sum(y * y, axis=-1, keepdims=True) + EPS)
            if j == 0:
                y = y * (DN_DK ** -0.5)
            dst[hh] = y

    sl = sl_ref[...]
    lane = lax.broadcasted_iota(jnp.int32, (seq, LANES), 1)
    beta_all = _sigmoid(sl)
    xs = sl + dtb_ref[...]
    softplus = jnp.maximum(xs, 0.0) + jnp.log1p(jnp.exp(-jnp.abs(xs)))
    g_all = -jnp.exp(alog_ref[...]) * softplus
    for hh in range(nhs):
        head = pl.program_id(1) * nhs + hh
        cols = jnp.zeros((seq, LANES), F32)
        for dr in range(2):
            beta = jnp.sum(jnp.where(lane == dr * DN_HEADS + head, beta_all, 0.0), axis=-1, keepdims=True)
            g = jnp.sum(jnp.where(lane == 2 * DN_HEADS + dr * DN_HEADS + head, g_all, 0.0),
                        axis=-1, keepdims=True)
            cols = jnp.where(lane == dr, beta, cols)
            cols = jnp.where(lane == 2 + dr, g, cols)
        col_ref[hh] = cols
        acc_ref[hh] = jnp.zeros((seq, LANES), F32)

    ii = lax.broadcasted_iota(jnp.int32, (c_len, c_len), 0)
    jj = lax.broadcasted_iota(jnp.int32, (c_len, c_len), 1)
    eye = jnp.where(ii == jj, 1.0, 0.0).astype(F32)
    causal = (ii >= jj, ii <= jj)
    strict = (ii > jj, ii < jj)
    tri16 = tuple(jnp.where(m, 1.0, 0.0).astype(BF16) for m in causal)
    level_masks = []
    for dr in range(2):
        hi_idx, lo_idx = (ii, jj) if dr == 0 else (jj, ii)
        masks = []
        lvl = 0
        while (1 << lvl) < c_len:
            masks.append(((hi_idx >> (lvl + 1)) == (lo_idx >> (lvl + 1)))
                         & ((hi_idx >> lvl) == (lo_idx >> lvl) + 1))
            lvl += 1
        level_masks.append(masks)

    group = max(1, min(DN_CHAINS // (2 * nhs), n_chunks))
    units = [(hh, t) for hh in range(nhs) for t in range(group)]
    chains = [(m, dr) for m in range(len(units)) for dr in range(2)]
    lane_sq = jj

    def group_body(i, carry):
        cs = [i * group + t for hh, t in units]
        rows = [pl.ds(pl.multiple_of(c * c_len, c_len), c_len) for c in cs]
        q = [qn_ref[hh, rows[m], :] for m, (hh, t) in enumerate(units)]
        k = [kn_ref[hh, rows[m], :] for m, (hh, t) in enumerate(units)]
        v = [vn_ref[hh, rows[m], :] for m, (hh, t) in enumerate(units)]
        blk = [col_ref[hh, rows[m], :] for m, (hh, t) in enumerate(units)]
        q16 = [x.astype(BF16) for x in q]
        k16 = [x.astype(BF16) for x in k]
        qk = [_dot_nt(a, b) for a, b in zip(q16, k16)]
        kk = [_dot_nt(b, b) for b in k16]
        parts = [_split3(x) for x in blk]
        cum = [jnp.where(lane_sq == 2, sum(_dot(tri16[0], p) for p in ps), sum(_dot(tri16[1], p) for p in ps))
               for ps in parts]
        cum_t = [x.T for x in cum]
        gc_col = [cum[m][:, 2 + dr:3 + dr] for m, dr in chains]
        gc_row = [cum_t[m][2 + dr:3 + dr, :] for m, dr in chains]
        beta = [blk[m][:, dr:dr + 1] for m, dr in chains]
        decay = [jnp.where(causal[dr], jnp.exp(jnp.where(causal[dr], gc_col[n] - gc_row[n], 0.0)), 0.0)
                 for n, (m, dr) in enumerate(chains)]
        lm = [jnp.where(strict[dr], (beta[n] * kk[m]) * decay[n], 0.0) for n, (m, dr) in enumerate(chains)]
        lm16 = [x.astype(BF16) for x in lm]
        zero16 = jnp.zeros((c_len, c_len), BF16)
        xs = [eye - jnp.where(level_masks[dr][0], lm[n], 0.0) for n, (m, dr) in enumerate(chains)]
        for lvl in range(1, len(level_masks[0])):
            ys = [_dot(jnp.where(level_masks[dr][lvl], lm16[n], zero16), xs[n].astype(BF16))
                  for n, (m, dr) in enumerate(chains)]
            xs = [xs[n] - _dot(xs[n].astype(BF16), ys[n].astype(BF16)) for n in range(len(chains))]
        tmat = [x.astype(BF16) for x in xs]
        eg = [jnp.exp(x) for x in gc_col]
        g_last = [gc_col[n][c_len - 1:c_len] if dr == 0 else gc_col[n][0:1] for n, (m, dr) in enumerate(chains)]
        us = [_dot(tmat[n], (v[m] * beta[n]).astype(BF16)) for n, (m, dr) in enumerate(chains)]
        ws = [_dot(tmat[n], (k[m] * (beta[n] * eg[n])).astype(BF16)) for n, (m, dr) in enumerate(chains)]
        for n, (m, dr) in enumerate(chains):
            hh, c = units[m][0], cs[m]
            u_ref[dr, hh, rows[m], :] = us[n]
            wq_ref[dr, hh, pl.ds(pl.multiple_of(2 * c * c_len, c_len), c_len), :] = ws[n].astype(BF16)
            wq_ref[dr, hh, pl.ds(pl.multiple_of(2 * c * c_len + c_len, c_len), c_len), :] = (
                q[m] * eg[n]).astype(BF16)
            kd_ref[dr, hh, rows[m], :] = (k[m] * jnp.exp(g_last[n] - gc_col[n])).astype(BF16)
            in_ref[dr, hh, rows[m], :] = (qk[m] * decay[n]).astype(BF16)
            gl_ref[dr, hh, pl.ds(c, 1), :] = jnp.broadcast_to(jnp.exp(g_last[n]), (1, LANES))
        return carry

    lax.fori_loop(0, n_chunks // group, group_body, 0)

    seq_chains = [(dr, hh) for dr in range(2) for hh in range(nhs)]

    def step(i, states):
        cs = [i, n_chunks - 1 - i]
        rows = [pl.ds(pl.multiple_of(c * c_len, c_len), c_len) for c in cs]
        ws_qs = [_dot(wq_ref[dr, hh, pl.ds(pl.multiple_of(2 * cs[dr] * c_len, 2 * c_len), 2 * c_len), :],
                      states[n].astype(BF16)) for n, (dr, hh) in enumerate(seq_chains)]
        v16 = [(u_ref[dr, hh, rows[dr], :] - ws_qs[n][0:c_len]).astype(BF16)
               for n, (dr, hh) in enumerate(seq_chains)]
        intra = [_dot(in_ref[dr, hh, rows[dr], :], v16[n]) for n, (dr, hh) in enumerate(seq_chains)]
        upd = [_dot_tn(kd_ref[dr, hh, rows[dr], :], v16[n]) for n, (dr, hh) in enumerate(seq_chains)]
        for n, (dr, hh) in enumerate(seq_chains):
            acc_ref[hh, rows[dr], :] += ws_qs[n][c_len:2 * c_len] + intra[n]
        return tuple(states[n] * gl_ref[dr, hh, pl.ds(cs[dr], 1), :] + upd[n]
                     for n, (dr, hh) in enumerate(seq_chains))

    s_fin = lax.fori_loop(0, n_chunks, step, tuple(s0_ref[dr, hh] for dr, hh in seq_chains))
    for n, (dr, hh) in enumerate(seq_chains):
        sfin_ref[dr, hh] = s_fin[n]

    for hh in range(nhs):
        lanes = slice(hh * LANES, (hh + 1) * LANES)
        o = acc_ref[hh]
        y = o * lax.rsqrt(jnp.mean(o * o, axis=-1, keepdims=True) + EPS) * ng_ref[...]
        z = z_ref[:, lanes]
        o_ref[:, lanes] = y * (z * _sigmoid(z))


def _dn_call(proj, params, s0, s0_map, *, layer, batch, seq, nhs):
    nh = DN_HEADS
    nblk = nh // nhs
    width = nhs * LANES
    n_chunks = seq // DN_TILE
    vec = pl.BlockSpec((None, 1, LANES), lambda b, h: (layer, 0, 0))
    nsub = max(n_chunks, SUBLANES)
    return pl.pallas_call(
        functools.partial(_dn_kernel, seq=seq, nhs=nhs),
        grid=(batch, nblk),
        in_specs=[pl.BlockSpec((seq, width), lambda b, h: (b, COL_DQ * nblk + h)),
                  pl.BlockSpec((seq, width), lambda b, h: (b, (COL_DQ + 1) * nblk + h)),
                  pl.BlockSpec((seq, width), lambda b, h: (b, (COL_DQ + 2) * nblk + h)),
                  pl.BlockSpec((seq, width), lambda b, h: (b, COL_DZ * nblk + h)),
                  pl.BlockSpec((seq, LANES), lambda b, h: (b, COL_SC128)),
                  pl.BlockSpec((None, 3, CONV_W, width), lambda b, h: (layer, 0, 0, h)),
                  vec, vec, vec,
                  pl.BlockSpec((None, None, 2, nhs, DN_DK, DN_DK), s0_map)],
        out_specs=[pl.BlockSpec((seq, width), lambda b, h: (b, h)),
                   pl.BlockSpec((None, 2, nhs, DN_DK, DN_DK), lambda b, h: (b, 0, h, 0, 0))],
        out_shape=[jax.ShapeDtypeStruct((batch * seq, BRANCH_W), F32),
                   jax.ShapeDtypeStruct((batch, 2, nh, DN_DK, DN_DK), F32)],
        scratch_shapes=[pltpu.VMEM((seq + 2 * SUBLANES, LANES), F32),
                        pltpu.VMEM((nhs, seq, LANES), F32),
                        pltpu.VMEM((nhs, seq, LANES), F32),
                        pltpu.VMEM((nhs, seq, LANES), F32),
                        pltpu.VMEM((nhs, seq, LANES), F32),
                        pltpu.VMEM((2, nhs, seq, LANES), F32),
                        pltpu.VMEM((2, nhs, 2 * seq, LANES), BF16),
                        pltpu.VMEM((2, nhs, seq, LANES), BF16),
                        pltpu.VMEM((2, nhs, seq, DN_TILE), BF16),
                        pltpu.VMEM((2, nhs, nsub, LANES), F32),
                        pltpu.VMEM((nhs, seq, LANES), F32)],
        compiler_params=_params(2),
        name="deltanet",
    )(proj, proj, proj, proj, proj, params["dn_cw"], params["dn_alog"], params["dn_dtb"], params["dn_ng"], s0)


def _prepare_params(w_ffn_gate, w_ffn_up, w_ffn_down, w_in, lru_conv_w, lru_conv_b, lru_w_r, lru_b_r,
                    lru_w_i, lru_b_i, lru_lambda, gqa_q_norm, gqa_k_norm, na_rpb, dn_conv_w, dn_a_log,
                    dn_dt_bias, dn_norm_g, w_branch, w_out, norm_g):
    depth = w_in.shape[0]
    offs = np.cumsum((0,) + IN_WIDTHS)
    w_in_t = jnp.swapaxes(w_in, 1, 2).astype(BF16)
    seg = [w_in_t[:, offs[i]:offs[i + 1]] for i in range(len(IN_WIDTHS))]
    (a_x, a_y, b_q, b_k, b_v, c_q, c_k, c_v, d_q, d_k, d_v, d_z, d_b, d_a, g_lin) = seg
    perm = np.asarray(GQA_PERM)
    b_q = b_q.reshape(depth, GQA_HEADS, HEAD_DIM, D_MODEL)[:, perm].reshape(depth, BRANCH_W, D_MODEL)
    pad = jnp.zeros((depth, PROJ_W - 5120 - 2 * LANES - 16, D_MODEL), BF16)
    w_main = jnp.concatenate([a_x, a_y, b_q, c_q, c_k, c_v, d_q, d_k, d_v, d_z, b_k, b_v, d_b, d_a, pad], axis=1)

    def block_diag(wb):
        wb = wb.reshape(depth, 2, LRU_BLOCKS // 2, 2, LRU_BW, LRU_BW)
        z = jnp.zeros_like(wb[:, :, :, 0])
        return jnp.concatenate([jnp.concatenate([wb[:, :, :, 0], z], axis=-1),
                                jnp.concatenate([z, wb[:, :, :, 1]], axis=-1)], axis=-2)

    wr, wi = block_diag(lru_w_r), block_diag(lru_w_i)
    lru_wg = (0.5 * jnp.concatenate([wr[:, 0], wi[:, 0], wr[:, 1], wi[:, 1]], axis=-1)).astype(BF16)
    ncol = BRANCH_W // LANES

    def col_blocks(v):
        return v.reshape(depth, ncol, LANES)

    lru_bg = 0.5 * jnp.concatenate([col_blocks(lru_b_r[:, 0]), col_blocks(lru_b_i[:, 0]),
                                    col_blocks(lru_b_r[:, 1]), col_blocks(lru_b_i[:, 1])], axis=-1)[:, :, None, :]

    wb_b = w_branch[:, 1].reshape(depth, GQA_HEADS, HEAD_DIM, D_MODEL)[:, perm].reshape(depth, BRANCH_W, D_MODEL)
    wb = jnp.stack([w_branch[:, 0], wb_b, w_branch[:, 2], w_branch[:, 3]], axis=1).astype(BF16)

    lane_pad = jnp.zeros((depth, LANES - 4 * DN_HEADS), F32)
    lane_zero = jnp.zeros((depth, 2 * DN_HEADS), F32)
    alog = jnp.concatenate([lane_zero, dn_a_log.reshape(depth, -1), lane_pad], axis=1)[:, None, :]
    dtb = jnp.concatenate([lane_zero, dn_dt_bias.reshape(depth, -1), lane_pad], axis=1)[:, None, :]

    return dict(
        norm_g=norm_g[:, :, None, :],
        wg=w_ffn_gate.astype(BF16), wu=w_ffn_up.astype(BF16), wd=w_ffn_down.astype(BF16),
        w_main=w_main, w_gate=g_lin,
        lru_cw=lru_conv_w, lru_cb=lru_conv_b[:, None, :], lru_wg=lru_wg, lru_bg=lru_bg, lru_lam=lru_lambda,
        gq=jnp.tile(gqa_q_norm, (1, 2))[:, None, :], gk=jnp.tile(gqa_k_norm, (1, 2))[:, None, :],
        na_bias=_na_bias_table(na_rpb),
        dn_cw=dn_conv_w.reshape(depth, CONV_W, 3, BRANCH_W).transpose(0, 2, 1, 3),
        dn_alog=alog, dn_dtb=dtb, dn_ng=dn_norm_g[:, None, :],
        wb=wb, w_out=w_out.astype(BF16),
    )


def _rope_tables(seq):
    pos = jnp.arange(seq)
    half = HEAD_DIM // 2
    inv = jnp.power(ROPE_BASE, -jnp.arange(0, half, 2, dtype=F32) / half)
    ang_r = (pos // GRID_W).astype(F32)[:, None] * inv[None, :]
    ang_c = (pos % GRID_W).astype(F32)[:, None] * inv[None, :]
    cos = jnp.concatenate([jnp.cos(ang_r)] * 2 + [jnp.cos(ang_c)] * 2, axis=-1)
    sin = jnp.concatenate([-jnp.sin(ang_r), jnp.sin(ang_r), -jnp.sin(ang_c), jnp.sin(ang_c)], axis=-1)
    return jnp.tile(cos, (1, 2)), jnp.tile(sin, (1, 2))


def _na_bias_table(rpb):
    qc = np.arange(GRID_W)
    cs = np.clip(qc - NA_WIN_C // 2, 0, GRID_W - NA_WIN_C)
    kc = np.arange(GRID_W)
    inwin = (kc[None, :] >= cs[:, None]) & (kc[None, :] < cs[:, None] + NA_WIN_C)
    coff = kc[None, :] - qc[:, None] + NA_WIN_C - 1
    onehot = (coff[None] == np.arange(2 * NA_WIN_C - 1)[:, None, None]).astype(np.float32)
    t = jnp.einsum("lhrd,dqk->lhrqk", rpb.astype(F32), onehot, precision=lax.Precision.HIGHEST)
    t = jnp.where(inwin, t, NEG_BIG)
    edge = jnp.full_like(t[:, :, :1], NEG_BIG)
    t = jnp.concatenate([edge, t, edge], axis=2)
    return jnp.concatenate([t[:, :, :-1], t[:, :, 1:]], axis=-1)


def _layer(x, mod_all, params, *, batch, seq, latent, layer, caches, tables, final_g, tm):
    row0 = 1 if latent else 0
    x = _ffn_call(x, mod_all, params, final_g, layer=layer, which=0, row0=row0, final=False, tm=tm)
    proj = _inproj_call(x, mod_all, params, layer=layer, row0=row0, tm=tm)

    dn_heads_per_step = 1 if seq > 1024 else DN_HEADS
    if latent:
        cache_ak, cache_av, cache_nk, cache_nv, state_lru, state_delta = caches
        h0, h0_map = state_lru, lambda b, c: (b, layer, 0, c)
        s0, s0_map = state_delta, lambda b, h: (b, layer, 0, h, 0, 0)
    else:
        h0, h0_map = jnp.zeros((1, 1, 2, BRANCH_W), F32), lambda b, c: (0, 0, 0, c)
        s0 = jnp.zeros((1, 1, 2, dn_heads_per_step, DN_DK, DN_DK), F32)
        s0_map = lambda b, h: (0, 0, 0, 0, 0, 0)

    o_a, lru_fin = _lru_call(proj, h0, h0_map, params, layer=layer, batch=batch, seq=seq,
                             ncb=1 if seq > 1024 else BRANCH_W // LANES)

    cos, sin = tables["rope"] if latent else (None, None)
    qn, kn = _prep_call(proj, params, cos, sin, layer=layer, batch=batch, seq=seq, rope=latent)
    tq = min(seq, ATTN_QROWS)
    kv_new = (kn, pl.BlockSpec((seq, LANES), lambda b, i: (b, 0)),
              proj, pl.BlockSpec((seq, LANES), lambda b, i: (b, COL_BV128)))
    if latent:
        gqa_cache = pl.BlockSpec((None, None, PAST_LEN, LANES), lambda b, i: (b, layer, 0, 0))
        sources = [kv_new, (cache_ak, gqa_cache, cache_av, gqa_cache)]
    else:
        sources = [kv_new]
    o_b = _attn_call(qn, 0, sources, batch=batch, seq=seq, tq=tq, nkb=1, qscale=1.0)

    if latent:
        o_c = _na_call(proj, cache_nk, cache_nv, params, layer=layer, batch=batch, seq=seq)
    else:
        src = (proj, pl.BlockSpec((seq, BRANCH_W), lambda b, i: (b, COL_CK)),
               proj, pl.BlockSpec((seq, BRANCH_W), lambda b, i: (b, COL_CV)))
        o_c = _attn_call(proj, COL_CQ, [src], batch=batch, seq=seq, tq=tq, nkb=4, qscale=HEAD_DIM ** -0.5)

    o_d, dn_fin = _dn_call(proj, params, s0, s0_map, layer=layer, batch=batch, seq=seq, nhs=dn_heads_per_step)

    x = _merge_call(x, mod_all, params, (o_a, o_b, o_c, o_d), layer=layer, row0=row0, tm=tm)
    x = _ffn_call(x, mod_all, params, final_g, layer=layer, which=1, row0=row0,
                  final=(layer == DEPTH - 1), tm=tm)

    new_ctx = None
    if not latent:
        new_ctx = (kn.reshape(batch, seq, GQA_KV, HEAD_DIM),
                   proj[:, COL_BV128 * LANES:(COL_BV128 + 1) * LANES].reshape(batch, seq, GQA_KV, HEAD_DIM),
                   proj[:, COL_CK * BRANCH_W:(COL_CK + 1) * BRANCH_W].reshape(batch, seq, NA_HEADS, HEAD_DIM),
                   proj[:, COL_CV * BRANCH_W:(COL_CV + 1) * BRANCH_W].reshape(batch, seq, NA_HEADS, HEAD_DIM),
                   lru_fin, dn_fin)
    return x, new_ctx


def kernel(x_prompt, x_sample, c, cache_attn_k, cache_attn_v, cache_na_k, cache_na_v, state_lru, state_delta, c_ctx, w_mod, b_mod, norm_g, w_ffn_gate, w_ffn_up, w_ffn_down, w_in, lru_conv_w, lru_conv_b, lru_w_r, lru_b_r, lru_w_i, lru_b_i, lru_lambda, gqa_q_norm, gqa_k_norm, na_rpb, dn_conv_w, dn_a_log, dn_dt_bias, dn_norm_g, w_branch, w_out, final_norm_g):
    batch_c, seq_c, _ = x_prompt.shape
    batch_l, seq_l, _ = x_sample.shape
    assert batch_l + 1 <= SUBLANES

    cs = jnp.concatenate([c_ctx[None, :], c, jnp.zeros((SUBLANES - 1 - batch_l, D_MODEL), F32)], axis=0)
    mod_all = _mod_call(cs, w_mod, b_mod).reshape(DEPTH, SUBLANES, N_MOD, D_MODEL)
    params = _prepare_params(w_ffn_gate, w_ffn_up, w_ffn_down, w_in, lru_conv_w, lru_conv_b, lru_w_r, lru_b_r,
                             lru_w_i, lru_b_i, lru_lambda, gqa_q_norm, gqa_k_norm, na_rpb, dn_conv_w, dn_a_log,
                             dn_dt_bias, dn_norm_g, w_branch, w_out, norm_g)

    caches = (cache_attn_k.reshape(batch_l, DEPTH, PAST_LEN, GQA_KV * HEAD_DIM),
              cache_attn_v.reshape(batch_l, DEPTH, PAST_LEN, GQA_KV * HEAD_DIM),
              cache_na_k.reshape(batch_l, DEPTH, PAST_LEN, BRANCH_W),
              cache_na_v.reshape(batch_l, DEPTH, PAST_LEN, BRANCH_W),
              state_lru, state_delta)
    tables = {"rope": _rope_tables(seq_l)}
    final_g = final_norm_g[None, :]

    xc = x_prompt.reshape(1, batch_c * seq_c, D_MODEL)
    xl = x_sample
    ctx_out = []
    for l in range(DEPTH):
        xc, new_ctx = _layer(xc, mod_all, params, batch=batch_c, seq=seq_c, latent=False, layer=l, caches=None,
                             tables=tables, final_g=final_g, tm=DENSE_ROWS)
        ctx_out.append(new_ctx)
        xl, _ = _layer(xl, mod_all, params, batch=batch_l, seq=seq_l, latent=True, layer=l, caches=caches,
                       tables=tables, final_g=final_g, tm=DENSE_ROWS)

    stacked = [jnp.stack([ctx_out[l][i] for l in range(DEPTH)], axis=1) for i in range(6)]
    return (xc.reshape(batch_c, seq_c, D_MODEL), xl, *stacked)
```
